```python
import math
import jax, jax.numpy as jnp
from jax import lax
import numpy as np

D_MODEL = 1024
BATCH = 8
SEQ = 8192
DEPTH = 1

HEAD_DIM = 64
CONV_WIDTH = D_MODEL // 2
CONV_K = 3
N_Q_HEADS = (D_MODEL // 2) // HEAD_DIM
N_KV_HEADS = 2
Q_PER_KV = N_Q_HEADS // N_KV_HEADS
ATTN_WIDTH = N_Q_HEADS * HEAD_DIM
KV_WIDTH = N_KV_HEADS * HEAD_DIM
MIX_WIDTH = CONV_WIDTH + ATTN_WIDTH
IN_WIDTH = 3 * CONV_WIDTH + ATTN_WIDTH + 2 * KV_WIDTH
WINDOW = 128
BLOCK = 128
N_BUCKETS = 32
MAX_DISTANCE = 128
N_EXPERTS = 32
TOP_K = 4
D_FF = D_MODEL
SWIGLU_LIMIT = 7.0
SWIGLU_ALPHA = 1.702
MOE_BLOCK = 128
EPS = 1e-5

kernel_name = "hymba_conv_swa_sink_t5_moe_block"


def rms_norm(x, g):
    xf = x.astype(jnp.float32)
    y = xf * lax.rsqrt(jnp.mean(xf * xf, axis=-1, keepdims=True) + EPS)
    return (y * g.astype(jnp.float32)).astype(x.dtype)


def t5_bucket(rel):
    n = jnp.maximum(rel, 0)
    max_exact = N_BUCKETS // 2
    nf = jnp.maximum(n, 1).astype(jnp.float32)
    large = max_exact + (jnp.log(nf / max_exact) / math.log(MAX_DISTANCE / max_exact)
                         * (N_BUCKETS - max_exact)).astype(jnp.int32)
    large = jnp.minimum(large, N_BUCKETS - 1)
    return jnp.where(n < max_exact, n, large)


def causal_short_conv(u, conv_w):
    S = u.shape[1]
    up = jnp.pad(u, ((0, 0), (CONV_K - 1, 0), (0, 0)))
    y = conv_w[CONV_K - 1] * up[:, CONV_K - 1:CONV_K - 1 + S]
    for j in range(CONV_K - 1):
        y = y + conv_w[j] * up[:, j:j + S]
    return y


def sliding_window_attention(q, k, v, sinks, rel_bias):
    B, S = q.shape[0], q.shape[1]
    nb = S // BLOCK
    qb = q.reshape(B, nb, BLOCK, N_KV_HEADS, Q_PER_KV, HEAD_DIM)

    def with_prev(t):
        tb = t.reshape(B, nb, BLOCK, N_KV_HEADS, HEAD_DIM)
        prev = jnp.pad(tb[:, :-1], ((0, 0), (1, 0), (0, 0), (0, 0), (0, 0)))
        return jnp.concatenate([prev, tb], axis=2)

    kw, vw = with_prev(k), with_prev(v)
    scores = jnp.einsum('bnqkgd,bnskd->bnkgqs', qb, kw,
                        preferred_element_type=jnp.float32) * (HEAD_DIM ** -0.5)

    qi = jnp.arange(BLOCK, dtype=jnp.int32)
    kj = jnp.arange(2 * BLOCK, dtype=jnp.int32)
    rel = qi[:, None] + BLOCK - kj[None, :]
    bias = rel_bias.astype(jnp.float32)[t5_bucket(rel)]
    bias = jnp.transpose(bias, (2, 0, 1)).reshape(N_KV_HEADS, Q_PER_KV, BLOCK, 2 * BLOCK)

    in_window = (rel >= 0) & (rel < WINDOW)
    key_pos = jnp.arange(nb, dtype=jnp.int32)[:, None] * BLOCK - BLOCK + kj[None, :]
    mask = in_window[None] & (key_pos >= 0)[:, None, :]

    logits = jnp.where(mask[None, :, None, None], scores + bias[None, None], -jnp.inf)
    sink = sinks.astype(jnp.float32).reshape(1, 1, N_KV_HEADS, Q_PER_KV, 1, 1)
    m = jnp.maximum(jnp.max(logits, axis=-1, keepdims=True), sink)
    p = jnp.exp(logits - m)
    denom = jnp.sum(p, axis=-1, keepdims=True) + jnp.exp(sink - m)
    out = jnp.einsum('bnkgqs,bnskd->bnqkgd', (p / denom).astype(v.dtype), vw)
    return out.reshape(B, S, ATTN_WIDTH)


def clamped_swiglu(hid):
    x_glu, x_lin = hid[..., ::2], hid[..., 1::2]
    x_glu = jnp.minimum(x_glu, SWIGLU_LIMIT)
    x_lin = jnp.clip(x_lin, -SWIGLU_LIMIT, SWIGLU_LIMIT)
    return x_glu * jax.nn.sigmoid(SWIGLU_ALPHA * x_glu) * (x_lin + 1.0)


def moe_ffn(h, w_router, b_router, w1, b1, w2, b2):
    T = h.shape[0]
    logits = jnp.matmul(h, w_router, preferred_element_type=jnp.float32) + b_router.astype(jnp.float32)
    top_vals, top_idx = lax.top_k(logits, TOP_K)
    gates = jax.nn.softmax(top_vals, axis=-1)

    e_flat = top_idx.reshape(-1).astype(jnp.int32)
    tok_flat = jnp.repeat(jnp.arange(T, dtype=jnp.int32), TOP_K)
    g_flat = gates.reshape(-1)
    order = jnp.argsort(e_flat, stable=True)
    e_s, tok_s, g_s = e_flat[order], tok_flat[order], g_flat[order]

    counts = jnp.bincount(e_flat, length=N_EXPERTS).astype(jnp.int32)
    starts = jnp.cumsum(counts) - counts
    padded = (counts + MOE_BLOCK - 1) // MOE_BLOCK * MOE_BLOCK
    pad_ends = jnp.cumsum(padded)
    pad_starts = pad_ends - padded
    dest = pad_starts[e_s] + (jnp.arange(T * TOP_K, dtype=jnp.int32) - starts[e_s])

    n_blocks = -(-(T * TOP_K + N_EXPERTS * MOE_BLOCK) // MOE_BLOCK)
    n_slots = n_blocks * MOE_BLOCK
    slot_tok = jnp.zeros((n_slots,), jnp.int32).at[dest].set(tok_s)
    slot_gate = jnp.zeros((n_slots,), jnp.float32).at[dest].set(g_s)
    block_start = jnp.arange(n_blocks, dtype=jnp.int32) * MOE_BLOCK
    block_expert = jnp.minimum(jnp.searchsorted(pad_ends, block_start, side='right'),
                               N_EXPERTS - 1).astype(jnp.int32)

    def expert_block(args):
        toks, gts, e = args
        xb = h[toks]
        hid = xb @ w1[e] + b1[e]
        yb = clamped_swiglu(hid) @ w2[e] + b2[e]
        return yb * gts[:, None].astype(yb.dtype)

    y_slots = lax.map(expert_block, (slot_tok.reshape(n_blocks, MOE_BLOCK),
                                     slot_gate.reshape(n_blocks, MOE_BLOCK), block_expert))
    y_slots = y_slots.reshape(n_slots, h.shape[-1])
    return jnp.zeros_like(h).at[slot_tok].add(y_slots.astype(h.dtype))


def setup_inputs(seed: int = 0) -> dict:
    key = jax.random.key(seed)
    ks = jax.random.split(key, 20)
    f32 = jnp.float32
    nrm = lambda k, shape, s: jax.random.normal(k, shape, f32) * s
    return {
        "x": nrm(ks[0], (BATCH, SEQ, D_MODEL), 1.0),
        "norm_mix": 1.0 + nrm(ks[1], (DEPTH, D_MODEL), 0.02),
        "w_in": nrm(ks[2], (DEPTH, D_MODEL, IN_WIDTH), D_MODEL ** -0.5),
        "b_in": nrm(ks[3], (DEPTH, IN_WIDTH), 0.02),
        "conv_w": nrm(ks[4], (DEPTH, CONV_K, CONV_WIDTH), CONV_K ** -0.5),
        "q_norm": 1.0 + nrm(ks[5], (DEPTH, HEAD_DIM), 0.02),
        "k_norm": 1.0 + nrm(ks[6], (DEPTH, HEAD_DIM), 0.02),
        "sinks": nrm(ks[7], (DEPTH, N_Q_HEADS), 1.0),
        "rel_bias": nrm(ks[8], (N_BUCKETS, N_Q_HEADS), 0.1),
        "out_norm_conv": 1.0 + nrm(ks[9], (DEPTH, CONV_WIDTH), 0.02),
        "out_norm_attn": 1.0 + nrm(ks[10], (DEPTH, ATTN_WIDTH), 0.02),
        "w_out": nrm(ks[11], (DEPTH, MIX_WIDTH, D_MODEL), MIX_WIDTH ** -0.5),
        "b_out": nrm(ks[12], (DEPTH, D_MODEL), 0.02),
        "norm_ffn": 1.0 + nrm(ks[13], (DEPTH, D_MODEL), 0.02),
        "w_router": nrm(ks[14], (DEPTH, D_MODEL, N_EXPERTS), D_MODEL ** -0.5),
        "b_router": nrm(ks[15], (DEPTH, N_EXPERTS), 0.01),
        "w1": nrm(ks[16], (DEPTH, N_EXPERTS, D_MODEL, 2 * D_FF), D_MODEL ** -0.5),
        "b1": nrm(ks[17], (DEPTH, N_EXPERTS, 2 * D_FF), 0.02),
        "w2": nrm(ks[18], (DEPTH, N_EXPERTS, D_FF, D_MODEL), D_FF ** -0.5),
        "b2": nrm(ks[19], (DEPTH, N_EXPERTS, D_MODEL), 0.02),
    }


def reference(x, norm_mix, w_in, b_in, conv_w, q_norm, k_norm, sinks, rel_bias,
              out_norm_conv, out_norm_attn, w_out, b_out, norm_ffn,
              w_router, b_router, w1, b1, w2, b2):
    B, S, D = x.shape
    cuts = [CONV_WIDTH, 2 * CONV_WIDTH, 3 * CONV_WIDTH, 3 * CONV_WIDTH + ATTN_WIDTH,
            3 * CONV_WIDTH + ATTN_WIDTH + KV_WIDTH]
    for l in range(DEPTH):
        h = rms_norm(x, norm_mix[l])
        proj = h @ w_in[l] + b_in[l]
        b_gate, c_gate, u, q, k, v = jnp.split(proj, cuts, axis=-1)

        y_conv = b_gate * causal_short_conv(c_gate * u, conv_w[l])

        q = rms_norm(q.reshape(B, S, N_Q_HEADS, HEAD_DIM), q_norm[l])
        k = rms_norm(k.reshape(B, S, N_KV_HEADS, HEAD_DIM), k_norm[l])
        v = v.reshape(B, S, N_KV_HEADS, HEAD_DIM)
        y_attn = sliding_window_attention(q, k, v, sinks[l], rel_bias)

        mix = jnp.concatenate([rms_norm(y_conv, out_norm_conv[l]),
                               rms_norm(y_attn, out_norm_attn[l])], axis=-1)
        x = x + mix @ w_out[l] + b_out[l]

        h2 = rms_norm(x, norm_ffn[l]).reshape(B * S, D)
        x = x + moe_ffn(h2, w_router[l], b_router[l], w1[l], b1[l], w2[l], b2[l]).reshape(B, S, D)
    return x
```

```python
import functools
import math

import jax
import jax.numpy as jnp
import numpy as np
from jax import lax
from jax.experimental import pallas as pl
from jax.experimental.pallas import tpu as pltpu

HEAD_DIM = 64
N_Q_HEADS = 8
N_KV_HEADS = 2
Q_PER_KV = N_Q_HEADS // N_KV_HEADS
CONV_K = 3
WINDOW = 128
ATT_BLOCK = 128
N_BUCKETS = 32
MAX_DISTANCE = 128
N_EXPERTS = 32
TOP_K = 4
SWIGLU_LIMIT = 7.0
SWIGLU_ALPHA = 1.702
EPS = 1e-5
MASK_VALUE = -1e30

LANES = 128
SUBLANES = 8
VMEM_LIMIT_BYTES = 56 * 1024 * 1024

MIX_ROWS = 512
MOE_CHUNK = 4096
MOE_ROWS = 128


def _rms(x, g):
    return x * lax.rsqrt(jnp.mean(x * x, axis=-1, keepdims=True) + EPS) * g


def _half_head_norm(t, gain2, lo):
    t2 = t * t
    s_lo = jnp.sum(jnp.where(lo, t2, 0.0), axis=-1, keepdims=True)
    s_hi = jnp.sum(jnp.where(lo, 0.0, t2), axis=-1, keepdims=True)
    r = jnp.where(lo, lax.rsqrt(s_lo * (1.0 / HEAD_DIM) + EPS), lax.rsqrt(s_hi * (1.0 / HEAD_DIM) + EPS))
    return t * r * gain2


def _mixer_kernel(sinks_ref, relb_ref,
                  x_ref, gmix_ref, win_ref, bin_ref, convw_ref, gq_ref, gk_ref, bucket_ref,
                  gc_ref, ga_ref, wout_ref, bout_ref, gffn_ref, wrt_ref, br_ref,
                  x1_ref, h2_ref, ids_ref, gates_ref,
                  kd_scr, vd_scr, cu_scr, bias_scr, qm_scr):
    rows = x_ref.shape[1]
    d_model = x_ref.shape[2]
    conv_w = d_model // 2
    nblk = rows // ATT_BLOCK
    b = pl.program_id(0)
    s = pl.program_id(1)

    @pl.when((b == 0) & (s == 0))
    def _build_bias():
        bk = bucket_ref[...]
        col = lax.broadcasted_iota(jnp.int32, bk.shape, 1)
        accs = [jnp.full(bk.shape, MASK_VALUE, jnp.float32) for _ in range(N_Q_HEADS)]
        for bb in range(N_BUCKETS):
            hit = bk == bb
            for h in range(N_Q_HEADS):
                accs[h] = jnp.where(hit, relb_ref[bb, h], accs[h])
        for h in range(N_Q_HEADS):
            g, j = divmod(h, Q_PER_KV)
            bias_scr[0, g, pl.ds(j * ATT_BLOCK, ATT_BLOCK), :] = accs[h]
            bias_scr[1, g, pl.ds(j * ATT_BLOCK, ATT_BLOCK), :] = jnp.where(col < ATT_BLOCK, MASK_VALUE, accs[h])

    @pl.when(s == 0)
    def _reset_state():
        kd_scr[:, pl.ds(0, ATT_BLOCK), :] = jnp.zeros((N_KV_HEADS, ATT_BLOCK, LANES), kd_scr.dtype)
        vd_scr[:, pl.ds(0, ATT_BLOCK), :] = jnp.zeros((N_KV_HEADS, ATT_BLOCK, LANES), vd_scr.dtype)
        cu_scr[...] = jnp.zeros(cu_scr.shape, cu_scr.dtype)

    x = x_ref[0]
    h = _rms(x, gmix_ref[...]).astype(jnp.bfloat16)
    proj = jnp.dot(h, win_ref[...], preferred_element_type=jnp.float32) + bin_ref[...]
    b_gate = proj[:, 0:conv_w]
    c_gate = proj[:, conv_w:2 * conv_w]
    u = proj[:, 2 * conv_w:3 * conv_w]
    q = proj[:, 3 * conv_w:4 * conv_w]
    k = proj[:, 4 * conv_w:4 * conv_w + LANES]
    v = proj[:, 4 * conv_w + LANES:4 * conv_w + 2 * LANES]

    cu = c_gate * u
    row = lax.broadcasted_iota(jnp.int32, cu.shape, 0)
    prev2 = cu_scr[SUBLANES - 2:SUBLANES - 1, :]
    prev1 = cu_scr[SUBLANES - 1:SUBLANES, :]
    r1 = jnp.where(row == 0, prev1, pltpu.roll(cu, 1, 0))
    r2 = jnp.where(row == 0, prev2, jnp.where(row == 1, prev1, pltpu.roll(cu, 2, 0)))
    y_conv = b_gate * (convw_ref[2:3, :] * cu + convw_ref[1:2, :] * r1 + convw_ref[0:1, :] * r2)
    cu_scr[...] = cu[rows - SUBLANES:rows, :]

    lane = lax.broadcasted_iota(jnp.int32, (rows, LANES), 1)
    lo = lane < HEAD_DIM
    kn = _half_head_norm(k, gk_ref[...], lo)
    krot = pltpu.roll(kn, HEAD_DIM, 1)
    vrot = pltpu.roll(v, HEAD_DIM, 1)
    kd_scr[0, pl.ds(ATT_BLOCK, rows), :] = jnp.where(lo, kn, krot).astype(kd_scr.dtype)
    kd_scr[1, pl.ds(ATT_BLOCK, rows), :] = jnp.where(lo, krot, kn).astype(kd_scr.dtype)
    vd_scr[0, pl.ds(ATT_BLOCK, rows), :] = jnp.where(lo, v, vrot).astype(vd_scr.dtype)
    vd_scr[1, pl.ds(ATT_BLOCK, rows), :] = jnp.where(lo, vrot, v).astype(vd_scr.dtype)
    for c in range(conv_w // LANES):
        qn = _half_head_norm(q[:, c * LANES:(c + 1) * LANES], gq_ref[...], lo)
        qm_scr[2 * c] = jnp.where(lo, qn, 0.0).astype(qm_scr.dtype)
        qm_scr[2 * c + 1] = jnp.where(lo, 0.0, qn).astype(qm_scr.dtype)

    lo_b = lax.broadcasted_iota(jnp.int32, (ATT_BLOCK, LANES), 1) < HEAD_DIM
    ya_blocks = []
    for n in range(nblk):
        first = jnp.where(s == 0, 1, 0) if n == 0 else 0
        tiles = []
        for g in range(N_KV_HEADS):
            qs = jnp.concatenate([qm_scr[Q_PER_KV * g + j, pl.ds(n * ATT_BLOCK, ATT_BLOCK), :]
                                  for j in range(Q_PER_KV)], axis=0)
            kw = kd_scr[g, pl.ds(n * ATT_BLOCK, 2 * ATT_BLOCK), :]
            vw = vd_scr[g, pl.ds(n * ATT_BLOCK, 2 * ATT_BLOCK), :]
            sc = lax.dot_general(qs, kw, (((1,), (1,)), ((), ())), preferred_element_type=jnp.float32)
            logits = sc + bias_scr[first, g]
            ps, inv = [], []
            for j in range(Q_PER_KV):
                lj = logits[j * ATT_BLOCK:(j + 1) * ATT_BLOCK]
                sink = sinks_ref[Q_PER_KV * g + j]
                m = jnp.maximum(jnp.max(lj, axis=-1, keepdims=True), sink)
                p = jnp.exp(lj - m)
                den = jnp.sum(p, axis=-1, keepdims=True) + jnp.exp(sink - m)
                ps.append(p.astype(jnp.bfloat16))
                inv.append(1.0 / den)
            o = jnp.dot(jnp.concatenate(ps, axis=0), vw, preferred_element_type=jnp.float32)
            on = [o[j * ATT_BLOCK:(j + 1) * ATT_BLOCK] * inv[j] for j in range(Q_PER_KV)]
            tiles.append(jnp.where(lo_b, on[0], on[1]))
            tiles.append(jnp.where(lo_b, on[2], on[3]))
        ya_blocks.append(jnp.concatenate(tiles, axis=-1))
    y_attn = jnp.concatenate(ya_blocks, axis=0)

    for g in range(N_KV_HEADS):
        kd_scr[g, pl.ds(0, ATT_BLOCK), :] = kd_scr[g, pl.ds(rows, ATT_BLOCK), :]
        vd_scr[g, pl.ds(0, ATT_BLOCK), :] = vd_scr[g, pl.ds(rows, ATT_BLOCK), :]

    mix = jnp.concatenate([_rms(y_conv, gc_ref[...]), _rms(y_attn, ga_ref[...])], axis=-1)
    x1 = x + jnp.dot(mix.astype(jnp.bfloat16), wout_ref[...], preferred_element_type=jnp.float32) + bout_ref[...]
    x1_ref[0] = x1
    h2 = _rms(x1, gffn_ref[...])
    h2_ref[0] = h2
    lt = lax.dot_general(wrt_ref[...], h2, (((1,), (1,)), ((), ())), precision=lax.Precision.HIGHEST,
                         preferred_element_type=jnp.float32) + br_ref[...]
    eid = lax.broadcasted_iota(jnp.int32, lt.shape, 0)
    vals, idxs = [], []
    for _ in range(TOP_K):
        m = jnp.max(lt, axis=0, keepdims=True)
        idx = jnp.min(jnp.where(lt == m, eid, N_EXPERTS), axis=0, keepdims=True)
        vals.append(m)
        idxs.append(idx)
        lt = jnp.where(eid == idx, -jnp.inf, lt)
    es = [jnp.exp(vv - vals[0]) for vv in vals]
    tot = es[0] + es[1] + es[2] + es[3]
    ids_ref[0] = jnp.concatenate(idxs, axis=0)
    gates_ref[0] = jnp.concatenate([e / tot for e in es], axis=0)


def _mixer(x, p, bucket):
    bsz, seq, d_model = x.shape
    rows = min(MIX_ROWS, seq)
    ns = seq // rows
    conv_w = d_model // 2
    in_w = p["win"].shape[1]
    full = lambda shape: pl.BlockSpec(shape, lambda b, s: (0,) * len(shape))
    smem = pl.BlockSpec(memory_space=pltpu.SMEM)
    tok_spec = pl.BlockSpec((1, rows, d_model), lambda b, s: (b, s, 0))
    sel_spec = pl.BlockSpec((1, TOP_K, rows), lambda b, s: (b * ns + s, 0, 0))
    return pl.pallas_call(
        _mixer_kernel,
        grid=(bsz, ns),
        in_specs=[smem, smem, tok_spec, full((1, d_model)), full((d_model, in_w)), full((1, in_w)),
                  full((CONV_K, conv_w)), full((1, LANES)), full((1, LANES)), full((ATT_BLOCK, 2 * ATT_BLOCK)),
                  full((1, conv_w)), full((1, conv_w)), full((d_model, d_model)), full((1, d_model)),
                  full((1, d_model)), full((N_EXPERTS, d_model)), full((N_EXPERTS, 1))],
        out_specs=[tok_spec, tok_spec, sel_spec, sel_spec],
        out_shape=[jax.ShapeDtypeStruct((bsz, seq, d_model), jnp.float32),
                   jax.ShapeDtypeStruct((bsz, seq, d_model), jnp.float32),
                   jax.ShapeDtypeStruct((bsz * ns, TOP_K, rows), jnp.int32),
                   jax.ShapeDtypeStruct((bsz * ns, TOP_K, rows), jnp.float32)],
        scratch_shapes=[pltpu.VMEM((N_KV_HEADS, ATT_BLOCK + rows, LANES), jnp.bfloat16),
                        pltpu.VMEM((N_KV_HEADS, ATT_BLOCK + rows, LANES), jnp.bfloat16),
                        pltpu.VMEM((SUBLANES, conv_w), jnp.float32),
                        pltpu.VMEM((2, N_KV_HEADS, Q_PER_KV * ATT_BLOCK, 2 * ATT_BLOCK), jnp.float32),
                        pltpu.VMEM((N_Q_HEADS, rows, LANES), jnp.bfloat16)],
        compiler_params=pltpu.CompilerParams(dimension_semantics=("arbitrary", "arbitrary"),
                                             vmem_limit_bytes=VMEM_LIMIT_BYTES),
        name="mixer",
    )(p["sinks"], p["rel_bias"], x, p["gmix"], p["win"], p["bin"], p["convw"], p["gq2"], p["gk2"], bucket,
      p["gc"], p["ga"], p["wout"], p["bout"], p["gffn"], p["wrt"], p["br"])


def _moe_kernel(nblk_ref, bstart_ref,
                x1_hbm, h_hbm, tok_hbm, gate_hbm, w1_hbm, w2_hbm, b1_ref, b2_ref,
                out_hbm,
                acc, hbuf, w1buf, w2buf, gbuf, rbuf, tok_s, gate_s, io_sem, w_sem, l_sem):
    chunk = hbuf.shape[0] // SUBLANES
    d_model = w1buf.shape[1]
    d_ff = w2buf.shape[1]
    nk = d_model // LANES
    c = pl.program_id(0)
    nchunks = pl.num_programs(0)
    max_blocks = tok_hbm.shape[0] // nchunks
    row0 = pl.multiple_of(c * (chunk * SUBLANES), SUBLANES)

    def weight_copies(e, slot):
        return (pltpu.make_async_copy(w1_hbm.at[e], w1buf.at[slot], w_sem.at[0, slot]),
                pltpu.make_async_copy(w2_hbm.at[e], w2buf.at[slot], w_sem.at[1, slot]))

    def chunk_in_copies():
        return (pltpu.make_async_copy(x1_hbm.at[pl.ds(row0, chunk * SUBLANES)],
                                      acc.at[pl.ds(0, chunk * SUBLANES)], io_sem.at[0]),
                pltpu.make_async_copy(h_hbm.at[pl.ds(row0, chunk * SUBLANES)], hbuf, io_sem.at[1]))

    def list_copies(blk):
        return (pltpu.make_async_copy(tok_hbm.at[blk], tok_s, l_sem.at[0]),
                pltpu.make_async_copy(gate_hbm.at[blk], gate_s, l_sem.at[1]))

    for cp in chunk_in_copies():
        cp.start()

    @pl.when(c == 0)
    def _first_weights():
        for cp in weight_copies(0, 0):
            cp.start()

    acc[pl.ds(chunk * SUBLANES, SUBLANES), :] = jnp.zeros((SUBLANES, LANES), jnp.float32)
    for cp in chunk_in_copies():
        cp.wait()

    def expert_body(e, carry):
        slot = lax.rem(e, 2)
        for cp in weight_copies(e, slot):
            cp.wait()

        @pl.when(e + 1 < N_EXPERTS)
        def _next_expert():
            for cp in weight_copies(e + 1, 1 - slot):
                cp.start()

        @pl.when((e + 1 == N_EXPERTS) & (c + 1 < nchunks))
        def _next_chunk_first_expert():
            for cp in weight_copies(0, 0):
                cp.start()

        b1 = b1_ref[e]
        b2 = b2_ref[e]
        first_blk = c * max_blocks + bstart_ref[c, e]

        def block_body(i, carry2):
            blk = first_blk + i
            for cp in list_copies(blk):
                cp.start()
            for cp in list_copies(blk):
                cp.wait()
            for j in range(MOE_ROWS):
                t = jnp.minimum(tok_s[0, j], chunk - 1)
                src = pl.multiple_of(t * SUBLANES, SUBLANES)
                gbuf[pl.ds(j * SUBLANES, SUBLANES), :] = hbuf[pl.ds(src, SUBLANES), :]
            xs = [gbuf[pl.ds(kk, MOE_ROWS, stride=SUBLANES), :] for kk in range(nk)]
            xb = jnp.concatenate(xs, axis=-1).astype(jnp.bfloat16)
            hid = jnp.dot(xb, w1buf[slot], preferred_element_type=jnp.float32) + b1
            glu = jnp.minimum(hid[:, :d_ff], SWIGLU_LIMIT)
            lin = jnp.clip(hid[:, d_ff:], -SWIGLU_LIMIT, SWIGLU_LIMIT)
            act = glu * jax.nn.sigmoid(SWIGLU_ALPHA * glu) * (lin + 1.0)
            y = jnp.dot(act.astype(jnp.bfloat16), w2buf[slot], preferred_element_type=jnp.float32) + b2
            for kk in range(nk):
                rbuf[pl.ds(kk, MOE_ROWS, stride=SUBLANES), :] = y[:, kk * LANES:(kk + 1) * LANES]
            unroll = 4
            for j0 in range(0, MOE_ROWS, unroll):
                dsts, vals = [], []
                for j in range(j0, j0 + unroll):
                    dst = pl.multiple_of(tok_s[0, j] * SUBLANES, SUBLANES)
                    dsts.append(dst)
                    vals.append(acc[pl.ds(dst, SUBLANES), :]
                                + gate_s[0, j] * rbuf[pl.ds(j * SUBLANES, SUBLANES), :])
                for dst, val in zip(dsts, vals):
                    acc[pl.ds(dst, SUBLANES), :] = val
            return carry2

        lax.fori_loop(0, nblk_ref[c, e], block_body, 0)
        return carry

    lax.fori_loop(0, N_EXPERTS, expert_body, 0)

    out_cp = pltpu.make_async_copy(acc.at[pl.ds(0, chunk * SUBLANES)],
                                   out_hbm.at[pl.ds(row0, chunk * SUBLANES)], io_sem.at[0])
    out_cp.start()
    out_cp.wait()


def _moe(x1r, hr, tok_blocks, gate_blocks, nblk, bstart, w1, w2, b1, b2, chunk):
    total_rows = x1r.shape[0]
    nchunks = total_rows // (chunk * SUBLANES)
    d_model = w1.shape[1]
    d_ff = w2.shape[1]
    anyspace = pl.BlockSpec(memory_space=pl.ANY)
    grid_spec = pltpu.PrefetchScalarGridSpec(
        num_scalar_prefetch=2,
        grid=(nchunks,),
        in_specs=[anyspace, anyspace, anyspace, anyspace, anyspace, anyspace,
                  pl.BlockSpec((N_EXPERTS, 1, 2 * d_ff), lambda c, *_: (0, 0, 0)),
                  pl.BlockSpec((N_EXPERTS, 1, d_model), lambda c, *_: (0, 0, 0))],
        out_specs=anyspace,
        scratch_shapes=[pltpu.VMEM(((chunk + 1) * SUBLANES, LANES), jnp.float32),
                        pltpu.VMEM((chunk * SUBLANES, LANES), jnp.float32),
                        pltpu.VMEM((2, d_model, 2 * d_ff), jnp.bfloat16),
                        pltpu.VMEM((2, d_ff, d_model), jnp.bfloat16),
                        pltpu.VMEM((MOE_ROWS * SUBLANES, LANES), jnp.float32),
                        pltpu.VMEM((MOE_ROWS * SUBLANES, LANES), jnp.float32),
                        pltpu.SMEM((1, MOE_ROWS), jnp.int32),
                        pltpu.SMEM((1, MOE_ROWS), jnp.float32),
                        pltpu.SemaphoreType.DMA((2,)),
                        pltpu.SemaphoreType.DMA((2, 2)),
                        pltpu.SemaphoreType.DMA((2,))],
    )
    return pl.pallas_call(
        _moe_kernel,
        grid_spec=grid_spec,
        out_shape=jax.ShapeDtypeStruct(x1r.shape, jnp.float32),
        compiler_params=pltpu.CompilerParams(dimension_semantics=("arbitrary",),
                                             vmem_limit_bytes=VMEM_LIMIT_BYTES),
        name="moe",
    )(nblk, bstart, x1r, hr, tok_blocks, gate_blocks, w1, w2, b1, b2)


def _t5_bucket_table():
    i = np.arange(ATT_BLOCK)[:, None]
    j = np.arange(2 * ATT_BLOCK)[None, :]
    rel = i + ATT_BLOCK - j
    max_exact = N_BUCKETS // 2
    nf = np.maximum(rel, 1).astype(np.float32)
    large = max_exact + (np.log(nf / max_exact) / np.float32(math.log(MAX_DISTANCE / max_exact))
                         * (N_BUCKETS - max_exact)).astype(np.int32)
    large = np.minimum(large, N_BUCKETS - 1)
    bucket = np.where(rel < max_exact, rel, large)
    return np.where((rel >= 0) & (rel < WINDOW), bucket, -1).astype(np.int32)


def _route_plan(ids, gates, chunk, max_blocks):
    t_total = ids.shape[0]
    nchunks = t_total // chunk
    n_assign = t_total * TOP_K
    e_flat = ids.reshape(-1)
    tok_flat = jnp.repeat(jnp.arange(t_total, dtype=jnp.int32), TOP_K)
    key = (tok_flat // chunk) * N_EXPERTS + e_flat
    order = jnp.argsort(key, stable=True)
    key_s = key[order]
    counts = jnp.bincount(key, length=nchunks * N_EXPERTS).astype(jnp.int32)
    starts = jnp.cumsum(counts) - counts
    padded = (counts + MOE_ROWS - 1) // MOE_ROWS * MOE_ROWS
    padded2 = padded.reshape(nchunks, N_EXPERTS)
    pad_starts = (jnp.cumsum(padded2, axis=1) - padded2).reshape(-1)
    slots_per_chunk = max_blocks * MOE_ROWS
    dest = ((key_s // N_EXPERTS) * slots_per_chunk + pad_starts[key_s]
            + (jnp.arange(n_assign, dtype=jnp.int32) - starts[key_s]))
    n_slots = nchunks * slots_per_chunk
    slot_tok = jnp.full((n_slots,), chunk, jnp.int32).at[dest].set(tok_flat[order] % chunk)
    slot_gate = jnp.zeros((n_slots,), jnp.float32).at[dest].set(gates.reshape(-1)[order])
    nblk = (padded2 // MOE_ROWS).astype(jnp.int32)
    bstart = (pad_starts.reshape(nchunks, N_EXPERTS) // MOE_ROWS).astype(jnp.int32)
    return (slot_tok.reshape(nchunks * max_blocks, 1, MOE_ROWS),
            slot_gate.reshape(nchunks * max_blocks, 1, MOE_ROWS), nblk, bstart)


def kernel(x, norm_mix, w_in, b_in, conv_w, q_norm, k_norm, sinks, rel_bias, out_norm_conv, out_norm_attn,
           w_out, b_out, norm_ffn, w_router, b_router, w1, b1, w2, b2):
    bsz, seq, d_model = x.shape
    depth = w_in.shape[0]
    bf16 = jnp.bfloat16
    bucket = jnp.asarray(_t5_bucket_table())
    t_total = bsz * seq
    chunk = min(MOE_CHUNK, t_total)
    max_blocks = chunk * TOP_K // MOE_ROWS + N_EXPERTS
    rows = min(MIX_ROWS, seq)
    for l in range(depth):
        d_ff = w2.shape[2]
        params = dict(
            sinks=sinks[l], rel_bias=rel_bias, gmix=norm_mix[l][None], win=w_in[l].astype(bf16), bin=b_in[l][None],
            convw=conv_w[l], gq2=jnp.tile(q_norm[l], 2)[None] * (HEAD_DIM ** -0.5), gk2=jnp.tile(k_norm[l], 2)[None],
            gc=out_norm_conv[l][None], ga=out_norm_attn[l][None], wout=w_out[l].astype(bf16), bout=b_out[l][None],
            gffn=norm_ffn[l][None], wrt=w_router[l].T, br=b_router[l][:, None])
        x1, h2, ids_t, gates_t = _mixer(x, params, bucket)
        ids = jnp.transpose(ids_t, (0, 2, 1)).reshape(t_total, TOP_K)
        gates = jnp.transpose(gates_t, (0, 2, 1)).reshape(t_total, TOP_K)
        tok_blocks, gate_blocks, nblk, bstart = _route_plan(ids, gates, chunk, max_blocks)
        w1p = jnp.concatenate([w1[l][:, :, 0::2], w1[l][:, :, 1::2]], axis=-1).astype(bf16)
        b1p = jnp.concatenate([b1[l][:, 0::2], b1[l][:, 1::2]], axis=-1)[:, None, :]
        outr = _moe(x1.reshape(t_total * SUBLANES, LANES), h2.reshape(t_total * SUBLANES, LANES),
                    tok_blocks, gate_blocks, nblk, bstart, w1p, w2[l].astype(bf16), b1p, b2[l][:, None, :], chunk)
        x = outr.reshape(bsz, seq, d_model)
    return x
```

```python
import functools
import math

import jax
import jax.numpy as jnp
import numpy as np
from jax import lax
from jax.experimental import pallas as pl
from jax.experimental.pallas import tpu as pltpu

HEAD_DIM = 64
N_Q_HEADS = 8
N_KV_HEADS = 2
Q_PER_KV = N_Q_HEADS // N_KV_HEADS
CONV_K = 3
WINDOW = 128
ATT_BLOCK = 128
N_BUCKETS = 32
MAX_DISTANCE = 128
N_EXPERTS = 32
TOP_K = 4
SWIGLU_LIMIT = 7.0
SWIGLU_ALPHA = 1.702
EPS = 1e-5
MASK_VALUE = -1e30

LANES = 128
SUBLANES = 8
MXU_COLS = 256
VMEM_LIMIT_BYTES = 56 * 1024 * 1024

MIX_ROWS = 512
MOE_CHUNK = 4096
MOE_ROWS = 128
PREP_ROWS = 256
INVERT_UNROLL = 16
SCATTER_UNROLL = 4


def _rms(x, g):
    return x * lax.rsqrt(jnp.mean(x * x, axis=-1, keepdims=True) + EPS) * g


def _half_head_norm(t, gain2, lo):
    t2 = t * t
    s_lo = jnp.sum(jnp.where(lo, t2, 0.0), axis=-1, keepdims=True)
    s_hi = jnp.sum(jnp.where(lo, 0.0, t2), axis=-1, keepdims=True)
    r = jnp.where(lo, lax.rsqrt(s_lo * (1.0 / HEAD_DIM) + EPS), lax.rsqrt(s_hi * (1.0 / HEAD_DIM) + EPS))
    return t * r * gain2


def _store_row_tiles(ref, val):
    rows = val.shape[0]
    for kk in range(val.shape[1] // LANES):
        ref[pl.ds(kk, rows, stride=SUBLANES), :] = val[:, kk * LANES:(kk + 1) * LANES]


def _wprep_kernel(w1_ref, w2_ref, perm_ref, w1o_ref, w2o_ref):
    perm = perm_ref[...]
    for c in range(w1_ref.shape[2] // MXU_COLS):
        cols = pl.ds(c * MXU_COLS, MXU_COLS)
        t = w1_ref[0, :, cols].astype(jnp.bfloat16)
        w1o_ref[0, :, cols] = jnp.dot(t, perm, preferred_element_type=jnp.float32).astype(jnp.bfloat16)
    w2o_ref[0] = w2_ref[0].astype(jnp.bfloat16)


def _wprep(w1, w2):
    n_e, d_model, two_ff = w1.shape
    d_ff, d_out = w2.shape[1], w2.shape[2]
    assert d_model == d_ff, "one row grid serves both weight tensors"
    j = np.arange(MXU_COLS)
    src = np.where(j < LANES, 2 * j, 2 * (j - LANES) + 1)
    perm = jnp.asarray((np.arange(MXU_COLS)[:, None] == src[None, :]).astype(np.float32), jnp.bfloat16)
    return pl.pallas_call(
        _wprep_kernel,
        grid=(n_e, d_model // PREP_ROWS),
        in_specs=[pl.BlockSpec((1, PREP_ROWS, two_ff), lambda e, r: (e, r, 0)),
                  pl.BlockSpec((1, PREP_ROWS, d_out), lambda e, r: (e, r, 0)),
                  pl.BlockSpec((MXU_COLS, MXU_COLS), lambda e, r: (0, 0))],
        out_specs=[pl.BlockSpec((1, PREP_ROWS, two_ff), lambda e, r: (e, r, 0)),
                   pl.BlockSpec((1, PREP_ROWS, d_out), lambda e, r: (e, r, 0))],
        out_shape=[jax.ShapeDtypeStruct(w1.shape, jnp.bfloat16), jax.ShapeDtypeStruct(w2.shape, jnp.bfloat16)],
        compiler_params=pltpu.CompilerParams(dimension_semantics=("arbitrary", "arbitrary")),
        name="wprep",
    )(w1, w2, perm)


def _mixer_kernel(tiles_per_chunk,
                  sinks_ref, relb_ref,
                  x_ref, gmix_ref, win_ref, bin_ref, convw_ref, gq_ref, gk_ref, bucket_ref,
                  gc_ref, ga_ref, wout_ref, bout_ref, gffn_ref, wrt_ref, br_ref, tri_ref,
                  x1_ref, h2_ref, ids_ref, gates_ref, rank_ref, counts_ref,
                  kd_scr, vd_scr, cu_scr, bias_scr, qm_scr, cnt_scr):
    rows = x_ref.shape[1]
    d_model = x_ref.shape[2]
    conv_w = d_model // 2
    nblk = rows // ATT_BLOCK
    b = pl.program_id(0)
    s = pl.program_id(1)
    tile = b * pl.num_programs(1) + s

    @pl.when((b == 0) & (s == 0))
    def _build_bias():
        bk = bucket_ref[...]
        col = lax.broadcasted_iota(jnp.int32, bk.shape, 1)
        accs = [jnp.full(bk.shape, MASK_VALUE, jnp.float32) for _ in range(N_Q_HEADS)]
        for bb in range(N_BUCKETS):
            hit = bk == bb
            for h in range(N_Q_HEADS):
                accs[h] = jnp.where(hit, relb_ref[bb, h], accs[h])
        for h in range(N_Q_HEADS):
            g, j = divmod(h, Q_PER_KV)
            bias_scr[0, g, pl.ds(j * ATT_BLOCK, ATT_BLOCK), :] = accs[h]
            bias_scr[1, g, pl.ds(j * ATT_BLOCK, ATT_BLOCK), :] = jnp.where(col < ATT_BLOCK, MASK_VALUE, accs[h])

    @pl.when(s == 0)
    def _reset_state():
        kd_scr[:, pl.ds(0, ATT_BLOCK), :] = jnp.zeros((N_KV_HEADS, ATT_BLOCK, LANES), kd_scr.dtype)
        vd_scr[:, pl.ds(0, ATT_BLOCK), :] = jnp.zeros((N_KV_HEADS, ATT_BLOCK, LANES), vd_scr.dtype)
        cu_scr[...] = jnp.zeros(cu_scr.shape, cu_scr.dtype)

    @pl.when(lax.rem(tile, tiles_per_chunk) == 0)
    def _reset_counts():
        cnt_scr[...] = jnp.zeros(cnt_scr.shape, cnt_scr.dtype)

    x = x_ref[0]
    h = _rms(x, gmix_ref[...]).astype(jnp.bfloat16)
    proj = jnp.dot(h, win_ref[...], preferred_element_type=jnp.float32) + bin_ref[...]
    b_gate = proj[:, 0:conv_w]
    c_gate = proj[:, conv_w:2 * conv_w]
    u = proj[:, 2 * conv_w:3 * conv_w]
    q = proj[:, 3 * conv_w:4 * conv_w]
    k = proj[:, 4 * conv_w:4 * conv_w + LANES]
    v = proj[:, 4 * conv_w + LANES:4 * conv_w + 2 * LANES]

    cu = c_gate * u
    row = lax.broadcasted_iota(jnp.int32, cu.shape, 0)
    prev2 = cu_scr[SUBLANES - 2:SUBLANES - 1, :]
    prev1 = cu_scr[SUBLANES - 1:SUBLANES, :]
    r1 = jnp.where(row == 0, prev1, pltpu.roll(cu, 1, 0))
    r2 = jnp.where(row == 0, prev2, jnp.where(row == 1, prev1, pltpu.roll(cu, 2, 0)))
    y_conv = b_gate * (convw_ref[2:3, :] * cu + convw_ref[1:2, :] * r1 + convw_ref[0:1, :] * r2)
    cu_scr[...] = cu[rows - SUBLANES:rows, :]

    lane = lax.broadcasted_iota(jnp.int32, (rows, LANES), 1)
    lo = lane < HEAD_DIM
    kn = _half_head_norm(k, gk_ref[...], lo)
    krot = pltpu.roll(kn, HEAD_DIM, 1)
    vrot = pltpu.roll(v, HEAD_DIM, 1)
    kd_scr[0, pl.ds(ATT_BLOCK, rows), :] = jnp.where(lo, kn, krot).astype(kd_scr.dtype)
    kd_scr[1, pl.ds(ATT_BLOCK, rows), :] = jnp.where(lo, krot, kn).astype(kd_scr.dtype)
    vd_scr[0, pl.ds(ATT_BLOCK, rows), :] = jnp.where(lo, v, vrot).astype(vd_scr.dtype)
    vd_scr[1, pl.ds(ATT_BLOCK, rows), :] = jnp.where(lo, vrot, v).astype(vd_scr.dtype)
    for c in range(conv_w // LANES):
        qn = _half_head_norm(q[:, c * LANES:(c + 1) * LANES], gq_ref[...], lo)
        qm_scr[2 * c] = jnp.where(lo, qn, 0.0).astype(qm_scr.dtype)
        qm_scr[2 * c + 1] = jnp.where(lo, 0.0, qn).astype(qm_scr.dtype)

    lo_b = lax.broadcasted_iota(jnp.int32, (ATT_BLOCK, LANES), 1) < HEAD_DIM
    ya_blocks = []
    for n in range(nblk):
        first = jnp.where(s == 0, 1, 0) if n == 0 else 0
        tiles = []
        for g in range(N_KV_HEADS):
            qs = jnp.concatenate([qm_scr[Q_PER_KV * g + j, pl.ds(n * ATT_BLOCK, ATT_BLOCK), :]
                                  for j in range(Q_PER_KV)], axis=0)
            kw = kd_scr[g, pl.ds(n * ATT_BLOCK, 2 * ATT_BLOCK), :]
            vw = vd_scr[g, pl.ds(n * ATT_BLOCK, 2 * ATT_BLOCK), :]
            sc = lax.dot_general(qs, kw, (((1,), (1,)), ((), ())), preferred_element_type=jnp.float32)
            logits = sc + bias_scr[first, g]
            ps, inv = [], []
            for j in range(Q_PER_KV):
                lj = logits[j * ATT_BLOCK:(j + 1) * ATT_BLOCK]
                sink = sinks_ref[Q_PER_KV * g + j]
                m = jnp.maximum(jnp.max(lj, axis=-1, keepdims=True), sink)
                p = jnp.exp(lj - m)
                den = jnp.sum(p, axis=-1, keepdims=True) + jnp.exp(sink - m)
                ps.append(p.astype(jnp.bfloat16))
                inv.append(1.0 / den)
            o = jnp.dot(jnp.concatenate(ps, axis=0), vw, preferred_element_type=jnp.float32)
            on = [o[j * ATT_BLOCK:(j + 1) * ATT_BLOCK] * inv[j] for j in range(Q_PER_KV)]
            tiles.append(jnp.where(lo_b, on[0], on[1]))
            tiles.append(jnp.where(lo_b, on[2], on[3]))
        ya_blocks.append(jnp.concatenate(tiles, axis=-1))
    y_attn = jnp.concatenate(ya_blocks, axis=0)

    for g in range(N_KV_HEADS):
        kd_scr[g, pl.ds(0, ATT_BLOCK), :] = kd_scr[g, pl.ds(rows, ATT_BLOCK), :]
        vd_scr[g, pl.ds(0, ATT_BLOCK), :] = vd_scr[g, pl.ds(rows, ATT_BLOCK), :]

    mix = jnp.concatenate([_rms(y_conv, gc_ref[...]), _rms(y_attn, ga_ref[...])], axis=-1)
    x1 = x + jnp.dot(mix.astype(jnp.bfloat16), wout_ref[...], preferred_element_type=jnp.float32) + bout_ref[...]
    _store_row_tiles(x1_ref, x1)
    h2 = _rms(x1, gffn_ref[...])
    _store_row_tiles(h2_ref, h2)
    lt = lax.dot_general(wrt_ref[...], h2, (((1,), (1,)), ((), ())), precision=lax.Precision.HIGHEST,
                         preferred_element_type=jnp.float32) + br_ref[...]
    eid = lax.broadcasted_iota(jnp.int32, lt.shape, 0)
    vals, idxs, hits = [], [], []
    for _ in range(TOP_K):
        m = jnp.max(lt, axis=0, keepdims=True)
        idx = jnp.min(jnp.where(lt == m, eid, N_EXPERTS), axis=0, keepdims=True)
        hit = eid == idx
        vals.append(m)
        idxs.append(idx)
        hits.append(hit)
        lt = jnp.where(hit, -jnp.inf, lt)
    es = [jnp.exp(vv - vals[0]) for vv in vals]
    tot = es[0] + es[1] + es[2] + es[3]
    ids_ref[0] = jnp.concatenate(idxs, axis=0)
    gates_ref[0] = jnp.concatenate([e / tot for e in es], axis=0)

    member = jnp.where(hits[0] | hits[1] | hits[2] | hits[3], 1.0, 0.0)
    before = jnp.dot(member.astype(jnp.bfloat16), tri_ref[...], preferred_element_type=jnp.float32)
    running = cnt_scr[...][:, 0:1]
    rank_e = before + running
    rank_ref[0] = jnp.concatenate([jnp.sum(jnp.where(hh, rank_e, 0.0), axis=0, keepdims=True) for hh in hits],
                                  axis=0).astype(jnp.int32)
    new_counts = cnt_scr[...] + jnp.sum(member, axis=1, keepdims=True)
    cnt_scr[...] = new_counts
    counts_ref[0] = new_counts.astype(jnp.int32)


def _mixer(x, p, bucket, chunk):
    bsz, seq, d_model = x.shape
    rows = min(MIX_ROWS, seq)
    ns = seq // rows
    tiles_per_chunk = chunk // rows
    nchunks = bsz * seq // chunk
    conv_w = d_model // 2
    in_w = p["win"].shape[1]
    tri = jnp.asarray(np.triu(np.ones((rows, rows), np.float32), 1), jnp.bfloat16)
    full = lambda shape: pl.BlockSpec(shape, lambda b, s: (0,) * len(shape))
    smem = pl.BlockSpec(memory_space=pltpu.SMEM)
    tok_spec = pl.BlockSpec((1, rows, d_model), lambda b, s: (b, s, 0))
    tile_spec = pl.BlockSpec((rows * SUBLANES, LANES), lambda b, s: (b * ns + s, 0))
    sel_spec = pl.BlockSpec((1, TOP_K, rows), lambda b, s: (b * ns + s, 0, 0))
    cnt_spec = pl.BlockSpec((1, N_EXPERTS, LANES), lambda b, s: ((b * ns + s) // tiles_per_chunk, 0, 0))
    t_total = bsz * seq
    sel_shape = (bsz * ns, TOP_K, rows)
    return pl.pallas_call(
        functools.partial(_mixer_kernel, tiles_per_chunk),
        grid=(bsz, ns),
        in_specs=[smem, smem, tok_spec, full((1, d_model)), full((d_model, in_w)), full((1, in_w)),
                  full((CONV_K, conv_w)), full((1, LANES)), full((1, LANES)), full((ATT_BLOCK, 2 * ATT_BLOCK)),
                  full((1, conv_w)), full((1, conv_w)), full((d_model, d_model)), full((1, d_model)),
                  full((1, d_model)), full((N_EXPERTS, d_model)), full((N_EXPERTS, 1)), full((rows, rows))],
        out_specs=[tile_spec, tile_spec, sel_spec, sel_spec, sel_spec, cnt_spec],
        out_shape=[jax.ShapeDtypeStruct((t_total * SUBLANES, LANES), jnp.float32),
                   jax.ShapeDtypeStruct((t_total * SUBLANES, LANES), jnp.float32),
                   jax.ShapeDtypeStruct(sel_shape, jnp.int32),
                   jax.ShapeDtypeStruct(sel_shape, jnp.float32),
                   jax.ShapeDtypeStruct(sel_shape, jnp.int32),
                   jax.ShapeDtypeStruct((nchunks, N_EXPERTS, LANES), jnp.int32)],
        scratch_shapes=[pltpu.VMEM((N_KV_HEADS, ATT_BLOCK + rows, LANES), jnp.bfloat16),
                        pltpu.VMEM((N_KV_HEADS, ATT_BLOCK + rows, LANES), jnp.bfloat16),
                        pltpu.VMEM((SUBLANES, conv_w), jnp.float32),
                        pltpu.VMEM((2, N_KV_HEADS, Q_PER_KV * ATT_BLOCK, 2 * ATT_BLOCK), jnp.float32),
                        pltpu.VMEM((N_Q_HEADS, rows, LANES), jnp.bfloat16),
                        pltpu.VMEM((N_EXPERTS, LANES), jnp.float32)],
        compiler_params=pltpu.CompilerParams(dimension_semantics=("arbitrary", "arbitrary"),
                                             vmem_limit_bytes=VMEM_LIMIT_BYTES),
        name="mixer",
    )(p["sinks"], p["rel_bias"], x, p["gmix"], p["win"], p["bin"], p["convw"], p["gq2"], p["gk2"], bucket,
      p["gc"], p["ga"], p["wout"], p["bout"], p["gffn"], p["wrt"], p["br"], tri)


def _moe_kernel(nblk_ref, bstart_ref,
                x1_hbm, h_hbm, dest_hbm, gate_hbm, fill_hbm, w1_hbm, w2_hbm, b1_ref, b2_ref,
                out_hbm,
                acc, hbuf, w1buf, w2buf, gbuf, rbuf, dest_s, gate_s, list_s, io_sem, w_sem, l_sem):
    chunk = hbuf.shape[0] // SUBLANES
    n_assign = TOP_K * chunk
    d_model = w1buf.shape[1]
    d_ff = w2buf.shape[1]
    nk = d_model // LANES
    c = pl.program_id(0)
    nchunks = pl.num_programs(0)
    row0 = pl.multiple_of(c * (chunk * SUBLANES), SUBLANES)

    def weight_copies(e, slot):
        return (pltpu.make_async_copy(w1_hbm.at[e], w1buf.at[slot], w_sem.at[0, slot]),
                pltpu.make_async_copy(w2_hbm.at[e], w2buf.at[slot], w_sem.at[1, slot]))

    def chunk_in_copies():
        return (pltpu.make_async_copy(x1_hbm.at[pl.ds(row0, chunk * SUBLANES)],
                                      acc.at[pl.ds(0, chunk * SUBLANES)], io_sem.at[0]),
                pltpu.make_async_copy(h_hbm.at[pl.ds(row0, chunk * SUBLANES)], hbuf, io_sem.at[1]))

    def list_copies():
        return (pltpu.make_async_copy(dest_hbm.at[c], dest_s, l_sem.at[0]),
                pltpu.make_async_copy(gate_hbm.at[c], gate_s.at[:, pl.ds(0, n_assign)], l_sem.at[1]),
                pltpu.make_async_copy(fill_hbm.at[0], list_s, l_sem.at[2]))

    for cp in list_copies() + chunk_in_copies():
        cp.start()

    @pl.when(c == 0)
    def _first_weights():
        for cp in weight_copies(0, 0):
            cp.start()

    gate_s[0, n_assign] = jnp.float32(0.0)
    for cp in list_copies():
        cp.wait()

    def invert_body(i, carry):
        base = i * INVERT_UNROLL
        for jj in range(INVERT_UNROLL):
            a = base + jj
            list_s[0, dest_s[0, a]] = a
        return carry

    lax.fori_loop(0, n_assign // INVERT_UNROLL, invert_body, 0)

    acc[pl.ds(chunk * SUBLANES, SUBLANES), :] = jnp.zeros((SUBLANES, LANES), jnp.float32)
    for cp in chunk_in_copies():
        cp.wait()

    log2_assign = int(math.log2(n_assign))

    def expert_body(e, carry):
        slot = lax.rem(e, 2)
        for cp in weight_copies(e, slot):
            cp.wait()

        @pl.when(e + 1 < N_EXPERTS)
        def _next_expert():
            for cp in weight_copies(e + 1, 1 - slot):
                cp.start()

        @pl.when((e + 1 == N_EXPERTS) & (c + 1 < nchunks))
        def _next_chunk_first_expert():
            for cp in weight_copies(0, 0):
                cp.start()

        b1 = b1_ref[e]
        b2 = b2_ref[e]
        first_slot = bstart_ref[c, e] * MOE_ROWS

        def block_body(i, carry2):
            s0 = first_slot + i * MOE_ROWS
            for j in range(MOE_ROWS):
                t = list_s[0, s0 + j] & (chunk - 1)
                src = pl.multiple_of(t * SUBLANES, SUBLANES)
                gbuf[pl.ds(j * SUBLANES, SUBLANES), :] = hbuf[pl.ds(src, SUBLANES), :]
            xs = [gbuf[pl.ds(kk, MOE_ROWS, stride=SUBLANES), :] for kk in range(nk)]
            xb = jnp.concatenate(xs, axis=-1).astype(jnp.bfloat16)
            hid = jnp.dot(xb, w1buf[slot], preferred_element_type=jnp.float32) + b1
            acts = []
            for cg in range(2 * d_ff // MXU_COLS):
                glu = jnp.minimum(hid[:, cg * MXU_COLS:cg * MXU_COLS + LANES], SWIGLU_LIMIT)
                lin = jnp.clip(hid[:, cg * MXU_COLS + LANES:(cg + 1) * MXU_COLS], -SWIGLU_LIMIT, SWIGLU_LIMIT)
                acts.append(glu * jax.nn.sigmoid(SWIGLU_ALPHA * glu) * (lin + 1.0))
            act = jnp.concatenate(acts, axis=-1).astype(jnp.bfloat16)
            y = jnp.dot(act, w2buf[slot], preferred_element_type=jnp.float32) + b2
            _store_row_tiles(rbuf, y)
            for j0 in range(0, MOE_ROWS, SCATTER_UNROLL):
                dsts, vals = [], []
                for j in range(j0, j0 + SCATTER_UNROLL):
                    a = list_s[0, s0 + j]
                    t = (a & (chunk - 1)) + (a >> log2_assign) * chunk
                    dst = pl.multiple_of(t * SUBLANES, SUBLANES)
                    dsts.append(dst)
                    vals.append(acc[pl.ds(dst, SUBLANES), :]
                                + gate_s[0, a] * rbuf[pl.ds(j * SUBLANES, SUBLANES), :])
                for dst, val in zip(dsts, vals):
                    acc[pl.ds(dst, SUBLANES), :] = val
            return carry2

        lax.fori_loop(0, nblk_ref[c, e], block_body, 0)
        return carry

    lax.fori_loop(0, N_EXPERTS, expert_body, 0)

    out_cp = pltpu.make_async_copy(acc.at[pl.ds(0, chunk * SUBLANES)],
                                   out_hbm.at[pl.ds(row0, chunk * SUBLANES)], io_sem.at[0])
    out_cp.start()
    out_cp.wait()


def _moe(x1r, hr, dest, gates, nblk, bstart, w1, w2, b1, b2, chunk, max_blocks):
    total_rows = x1r.shape[0]
    nchunks = total_rows // (chunk * SUBLANES)
    d_model = w1.shape[1]
    d_ff = w2.shape[1]
    n_assign = TOP_K * chunk
    n_slots = max_blocks * MOE_ROWS
    fill = jnp.full((1, 1, n_slots), n_assign, jnp.int32)
    anyspace = pl.BlockSpec(memory_space=pl.ANY)
    grid_spec = pltpu.PrefetchScalarGridSpec(
        num_scalar_prefetch=2,
        grid=(nchunks,),
        in_specs=[anyspace] * 7 + [pl.BlockSpec((N_EXPERTS, 1, 2 * d_ff), lambda c, *_: (0, 0, 0)),
                                   pl.BlockSpec((N_EXPERTS, 1, d_model), lambda c, *_: (0, 0, 0))],
        out_specs=anyspace,
        scratch_shapes=[pltpu.VMEM(((chunk + 1) * SUBLANES, LANES), jnp.float32),
                        pltpu.VMEM((chunk * SUBLANES, LANES), jnp.float32),
                        pltpu.VMEM((2, d_model, 2 * d_ff), jnp.bfloat16),
                        pltpu.VMEM((2, d_ff, d_model), jnp.bfloat16),
                        pltpu.VMEM((MOE_ROWS * SUBLANES, LANES), jnp.float32),
                        pltpu.VMEM((MOE_ROWS * SUBLANES, LANES), jnp.float32),
                        pltpu.SMEM((1, n_assign), jnp.int32),
                        pltpu.SMEM((1, n_assign + LANES), jnp.float32),
                        pltpu.SMEM((1, n_slots), jnp.int32),
                        pltpu.SemaphoreType.DMA((2,)),
                        pltpu.SemaphoreType.DMA((2, 2)),
                        pltpu.SemaphoreType.DMA((3,))],
    )
    return pl.pallas_call(
        _moe_kernel,
        grid_spec=grid_spec,
        out_shape=jax.ShapeDtypeStruct(x1r.shape, jnp.float32),
        compiler_params=pltpu.CompilerParams(dimension_semantics=("arbitrary",),
                                             vmem_limit_bytes=VMEM_LIMIT_BYTES),
        name="moe",
    )(nblk, bstart, x1r, hr, dest, gates, fill, w1, w2, b1, b2)


def _t5_bucket_table():
    i = np.arange(ATT_BLOCK)[:, None]
    j = np.arange(2 * ATT_BLOCK)[None, :]
    rel = i + ATT_BLOCK - j
    max_exact = N_BUCKETS // 2
    nf = np.maximum(rel, 1).astype(np.float32)
    large = max_exact + (np.log(nf / max_exact) / np.float32(math.log(MAX_DISTANCE / max_exact))
                         * (N_BUCKETS - max_exact)).astype(np.int32)
    large = np.minimum(large, N_BUCKETS - 1)
    bucket = np.where(rel < max_exact, rel, large)
    return np.where((rel >= 0) & (rel < WINDOW), bucket, -1).astype(np.int32)


def _chunk_major(a, nchunks):
    tiles, _, rows = a.shape
    a = a.reshape(nchunks, tiles // nchunks, TOP_K, rows)
    return jnp.transpose(a, (0, 2, 1, 3)).reshape(nchunks, 1, -1)


def kernel(x, norm_mix, w_in, b_in, conv_w, q_norm, k_norm, sinks, rel_bias, out_norm_conv, out_norm_attn,
           w_out, b_out, norm_ffn, w_router, b_router, w1, b1, w2, b2):
    bsz, seq, d_model = x.shape
    depth = w_in.shape[0]
    bf16 = jnp.bfloat16
    bucket = jnp.asarray(_t5_bucket_table())
    t_total = bsz * seq
    chunk = min(MOE_CHUNK, t_total)
    nchunks = t_total // chunk
    max_blocks = chunk * TOP_K // MOE_ROWS + N_EXPERTS
    for l in range(depth):
        d_ff = w2.shape[2]
        params = dict(
            sinks=sinks[l], rel_bias=rel_bias, gmix=norm_mix[l][None], win=w_in[l].astype(bf16), bin=b_in[l][None],
            convw=conv_w[l], gq2=jnp.tile(q_norm[l], 2)[None] * (HEAD_DIM ** -0.5), gk2=jnp.tile(k_norm[l], 2)[None],
            gc=out_norm_conv[l][None], ga=out_norm_attn[l][None], wout=w_out[l].astype(bf16), bout=b_out[l][None],
            gffn=norm_ffn[l][None], wrt=w_router[l].T, br=b_router[l][:, None])
        x1r, hr, ids_t, gates_t, rank_t, counts_t = _mixer(x, params, bucket, chunk)
        counts = counts_t[:, :, 0]
        padded = (counts + MOE_ROWS - 1) // MOE_ROWS * MOE_ROWS
        pad_starts = jnp.cumsum(padded, axis=1) - padded
        nblk = padded // MOE_ROWS
        bstart = pad_starts // MOE_ROWS
        ids_c = _chunk_major(ids_t, nchunks)
        dest = _chunk_major(rank_t, nchunks)
        for e in range(N_EXPERTS):
            dest = dest + jnp.where(ids_c == e, pad_starts[:, e][:, None, None], 0)
        w1p, w2p = _wprep(w1[l], w2[l])
        b1p = jnp.transpose(b1[l].reshape(N_EXPERTS, 2 * d_ff // MXU_COLS, LANES, 2), (0, 1, 3, 2))
        b1p = b1p.reshape(N_EXPERTS, 1, 2 * d_ff)
        outr = _moe(x1r, hr, dest, _chunk_major(gates_t, nchunks), nblk, bstart, w1p, w2p, b1p,
                    b2[l][:, None, :], chunk, max_blocks)
        x = outr.reshape(bsz, seq, d_model)
    return x
```

```python
import functools
import math

import jax
import jax.numpy as jnp
import numpy as np
from jax import lax
from jax.experimental import pallas as pl
from jax.experimental.pallas import tpu as pltpu

HEAD_DIM = 64
N_Q_HEADS = 8
N_KV_HEADS = 2
Q_PER_KV = N_Q_HEADS // N_KV_HEADS
CONV_K = 3
WINDOW = 128
ATT_BLOCK = 128
N_BUCKETS = 32
MAX_DISTANCE = 128
N_EXPERTS = 32
TOP_K = 4
SWIGLU_LIMIT = 7.0
SWIGLU_ALPHA = 1.702
EPS = 1e-5
MASK_VALUE = -1e30

LANES = 128
SUBLANES = 8
MXU_COLS = 256
VMEM_LIMIT_BYTES = 56 * 1024 * 1024

MIX_ROWS = 512
MOE_CHUNK = 4096
MOE_ROWS = 128
PREP_ROWS = 256
INVERT_UNROLL = 16
SCATTER_UNROLL = 4


def _rms(x, g):
    return x * lax.rsqrt(jnp.mean(x * x, axis=-1, keepdims=True) + EPS) * g


def _half_head_norm(t, gain2, lo):
    t2 = t * t
    s_lo = jnp.sum(jnp.where(lo, t2, 0.0), axis=-1, keepdims=True)
    s_hi = jnp.sum(jnp.where(lo, 0.0, t2), axis=-1, keepdims=True)
    r = jnp.where(lo, lax.rsqrt(s_lo * (1.0 / HEAD_DIM) + EPS), lax.rsqrt(s_hi * (1.0 / HEAD_DIM) + EPS))
    return t * r * gain2


def _store_row_tiles(ref, val):
    rows = val.shape[0]
    for kk in range(val.shape[1] // LANES):
        ref[pl.ds(kk, rows, stride=SUBLANES), :] = val[:, kk * LANES:(kk + 1) * LANES]


def _wprep_kernel(w1_ref, w2_ref, perm_ref, w1o_ref, w2o_ref):
    perm = perm_ref[...]
    for c in range(w1_ref.shape[2] // MXU_COLS):
        cols = pl.ds(c * MXU_COLS, MXU_COLS)
        t = w1_ref[0, :, cols].astype(jnp.bfloat16)
        w1o_ref[0, :, cols] = jnp.dot(t, perm, preferred_element_type=jnp.float32).astype(jnp.bfloat16)
    w2o_ref[0] = w2_ref[0].astype(jnp.bfloat16)


def _wprep(w1, w2):
    n_e, d_model, two_ff = w1.shape
    d_ff, d_out = w2.shape[1], w2.shape[2]
    assert d_model == d_ff, "one row grid serves both weight tensors"
    j = np.arange(MXU_COLS)
    src = np.where(j < LANES, 2 * j, 2 * (j - LANES) + 1)
    perm = jnp.asarray((np.arange(MXU_COLS)[:, None] == src[None, :]).astype(np.float32), jnp.bfloat16)
    return pl.pallas_call(
        _wprep_kernel,
        grid=(n_e, d_model // PREP_ROWS),
        in_specs=[pl.BlockSpec((1, PREP_ROWS, two_ff), lambda e, r: (e, r, 0)),
                  pl.BlockSpec((1, PREP_ROWS, d_out), lambda e, r: (e, r, 0)),
                  pl.BlockSpec((MXU_COLS, MXU_COLS), lambda e, r: (0, 0))],
        out_specs=[pl.BlockSpec((1, PREP_ROWS, two_ff), lambda e, r: (e, r, 0)),
                   pl.BlockSpec((1, PREP_ROWS, d_out), lambda e, r: (e, r, 0))],
        out_shape=[jax.ShapeDtypeStruct(w1.shape, jnp.bfloat16), jax.ShapeDtypeStruct(w2.shape, jnp.bfloat16)],
        compiler_params=pltpu.CompilerParams(dimension_semantics=("arbitrary", "arbitrary")),
        name="wprep",
    )(w1, w2, perm)


def _mixer_kernel(tiles_per_chunk,
                  sinks_ref, relb_ref,
                  x_ref, gmix_ref, win_ref, bin_ref, convw_ref, gq_ref, gk_ref, bucket_ref,
                  gc_ref, ga_ref, wout_ref, bout_ref, gffn_ref, wrt_ref, br_ref, tri_ref,
                  x1_ref, h2_ref, ids_ref, gates_ref, rank_ref, counts_ref,
                  kd_scr, vd_scr, cu_scr, bias_scr, qm_scr, cnt_scr):
    rows = x_ref.shape[1]
    d_model = x_ref.shape[2]
    conv_w = d_model // 2
    nblk = rows // ATT_BLOCK
    b = pl.program_id(0)
    s = pl.program_id(1)
    tile = b * pl.num_programs(1) + s

    @pl.when((b == 0) & (s == 0))
    def _build_bias():
        bk = bucket_ref[...]
        col = lax.broadcasted_iota(jnp.int32, bk.shape, 1)
        accs = [jnp.full(bk.shape, MASK_VALUE, jnp.float32) for _ in range(N_Q_HEADS)]
        for bb in range(N_BUCKETS):
            hit = bk == bb
            for h in range(N_Q_HEADS):
                accs[h] = jnp.where(hit, relb_ref[bb, h], accs[h])
        for h in range(N_Q_HEADS):
            g, j = divmod(h, Q_PER_KV)
            bias_scr[0, g, pl.ds(j * ATT_BLOCK, ATT_BLOCK), :] = accs[h]
            bias_scr[1, g, pl.ds(j * ATT_BLOCK, ATT_BLOCK), :] = jnp.where(col < ATT_BLOCK, MASK_VALUE, accs[h])

    @pl.when(s == 0)
    def _reset_state():
        kd_scr[:, pl.ds(0, ATT_BLOCK), :] = jnp.zeros((N_KV_HEADS, ATT_BLOCK, LANES), kd_scr.dtype)
        vd_scr[:, pl.ds(0, ATT_BLOCK), :] = jnp.zeros((N_KV_HEADS, ATT_BLOCK, LANES), vd_scr.dtype)
        cu_scr[...] = jnp.zeros(cu_scr.shape, cu_scr.dtype)

    @pl.when(lax.rem(tile, tiles_per_chunk) == 0)
    def _reset_counts():
        cnt_scr[...] = jnp.zeros(cnt_scr.shape, cnt_scr.dtype)

    x = x_ref[0]
    h = _rms(x, gmix_ref[...]).astype(jnp.bfloat16)
    proj = jnp.dot(h, win_ref[...], preferred_element_type=jnp.float32) + bin_ref[...]
    b_gate = proj[:, 0:conv_w]
    c_gate = proj[:, conv_w:2 * conv_w]
    u = proj[:, 2 * conv_w:3 * conv_w]
    q = proj[:, 3 * conv_w:4 * conv_w]
    k = proj[:, 4 * conv_w:4 * conv_w + LANES]
    v = proj[:, 4 * conv_w + LANES:4 * conv_w + 2 * LANES]

    cu = c_gate * u
    row = lax.broadcasted_iota(jnp.int32, cu.shape, 0)
    prev2 = cu_scr[SUBLANES - 2:SUBLANES - 1, :]
    prev1 = cu_scr[SUBLANES - 1:SUBLANES, :]
    r1 = jnp.where(row == 0, prev1, pltpu.roll(cu, 1, 0))
    r2 = jnp.where(row == 0, prev2, jnp.where(row == 1, prev1, pltpu.roll(cu, 2, 0)))
    y_conv = b_gate * (convw_ref[2:3, :] * cu + convw_ref[1:2, :] * r1 + convw_ref[0:1, :] * r2)
    cu_scr[...] = cu[rows - SUBLANES:rows, :]

    lane = lax.broadcasted_iota(jnp.int32, (rows, LANES), 1)
    lo = lane < HEAD_DIM
    kn = _half_head_norm(k, gk_ref[...], lo)
    krot = pltpu.roll(kn, HEAD_DIM, 1)
    vrot = pltpu.roll(v, HEAD_DIM, 1)
    kd_scr[0, pl.ds(ATT_BLOCK, rows), :] = jnp.where(lo, kn, krot).astype(kd_scr.dtype)
    kd_scr[1, pl.ds(ATT_BLOCK, rows), :] = jnp.where(lo, krot, kn).astype(kd_scr.dtype)
    vd_scr[0, pl.ds(ATT_BLOCK, rows), :] = jnp.where(lo, v, vrot).astype(vd_scr.dtype)
    vd_scr[1, pl.ds(ATT_BLOCK, rows), :] = jnp.where(lo, vrot, v).astype(vd_scr.dtype)
    for c in range(conv_w // LANES):
        qn = _half_head_norm(q[:, c * LANES:(c + 1) * LANES], gq_ref[...], lo)
        qm_scr[2 * c] = jnp.where(lo, qn, 0.0).astype(qm_scr.dtype)
        qm_scr[2 * c + 1] = jnp.where(lo, 0.0, qn).astype(qm_scr.dtype)

    lo_b = lax.broadcasted_iota(jnp.int32, (ATT_BLOCK, LANES), 1) < HEAD_DIM
    ya_blocks = []
    for n in range(nblk):
        first = jnp.where(s == 0, 1, 0) if n == 0 else 0
        tiles = []
        for g in range(N_KV_HEADS):
            qs = jnp.concatenate([qm_scr[Q_PER_KV * g + j, pl.ds(n * ATT_BLOCK, ATT_BLOCK), :]
                                  for j in range(Q_PER_KV)], axis=0)
            kw = kd_scr[g, pl.ds(n * ATT_BLOCK, 2 * ATT_BLOCK), :]
            vw = vd_scr[g, pl.ds(n * ATT_BLOCK, 2 * ATT_BLOCK), :]
            sc = lax.dot_general(qs, kw, (((1,), (1,)), ((), ())), preferred_element_type=jnp.float32)
            logits = sc + bias_scr[first, g]
            ps, inv = [], []
            for j in range(Q_PER_KV):
                lj = logits[j * ATT_BLOCK:(j + 1) * ATT_BLOCK]
                sink = sinks_ref[Q_PER_KV * g + j]
                m = jnp.maximum(jnp.max(lj, axis=-1, keepdims=True), sink)
                p = jnp.exp(lj - m)
                den = jnp.sum(p, axis=-1, keepdims=True) + jnp.exp(sink - m)
                ps.append(p.astype(jnp.bfloat16))
                inv.append(1.0 / den)
            o = jnp.dot(jnp.concatenate(ps, axis=0), vw, preferred_element_type=jnp.float32)
            on = [o[j * ATT_BLOCK:(j + 1) * ATT_BLOCK] * inv[j] for j in range(Q_PER_KV)]
            tiles.append(jnp.where(lo_b, on[0], on[1]))
            tiles.append(jnp.where(lo_b, on[2], on[3]))
        ya_blocks.append(jnp.concatenate(tiles, axis=-1))
    y_attn = jnp.concatenate(ya_blocks, axis=0)

    for g in range(N_KV_HEADS):
        kd_scr[g, pl.ds(0, ATT_BLOCK), :] = kd_scr[g, pl.ds(rows, ATT_BLOCK), :]
        vd_scr[g, pl.ds(0, ATT_BLOCK), :] = vd_scr[g, pl.ds(rows, ATT_BLOCK), :]

    mix = jnp.concatenate([_rms(y_conv, gc_ref[...]), _rms(y_attn, ga_ref[...])], axis=-1)
    x1 = x + jnp.dot(mix.astype(jnp.bfloat16), wout_ref[...], preferred_element_type=jnp.float32) + bout_ref[...]
    _store_row_tiles(x1_ref, x1)
    h2 = _rms(x1, gffn_ref[...])
    _store_row_tiles(h2_ref, h2)
    lt = lax.dot_general(wrt_ref[...], h2, (((1,), (1,)), ((), ())), precision=lax.Precision.HIGHEST,
                         preferred_element_type=jnp.float32) + br_ref[...]
    eid = lax.broadcasted_iota(jnp.int32, lt.shape, 0)
    vals, idxs, hits = [], [], []
    for _ in range(TOP_K):
        m = jnp.max(lt, axis=0, keepdims=True)
        idx = jnp.min(jnp.where(lt == m, eid, N_EXPERTS), axis=0, keepdims=True)
        hit = eid == idx
        vals.append(m)
        idxs.append(idx)
        hits.append(hit)
        lt = jnp.where(hit, -jnp.inf, lt)
    es = [jnp.exp(vv - vals[0]) for vv in vals]
    tot = es[0] + es[1] + es[2] + es[3]
    ids_ref[0] = jnp.concatenate(idxs, axis=0)
    gates_ref[0] = jnp.concatenate([e / tot for e in es], axis=0)

    member = jnp.where(hits[0] | hits[1] | hits[2] | hits[3], 1.0, 0.0)
    before = jnp.dot(member.astype(jnp.bfloat16), tri_ref[...], preferred_element_type=jnp.float32)
    running = cnt_scr[...][:, 0:1]
    rank_e = before + running
    rank_ref[0] = jnp.concatenate([jnp.sum(jnp.where(hh, rank_e, 0.0), axis=0, keepdims=True) for hh in hits],
                                  axis=0).astype(jnp.int32)
    new_counts = cnt_scr[...] + jnp.sum(member, axis=1, keepdims=True)
    cnt_scr[...] = new_counts
    counts_ref[0] = new_counts.astype(jnp.int32)


def _mixer(x, p, bucket, chunk):
    bsz, seq, d_model = x.shape
    rows = min(MIX_ROWS, seq)
    ns = seq // rows
    tiles_per_chunk = chunk // rows
    nchunks = bsz * seq // chunk
    conv_w = d_model // 2
    in_w = p["win"].shape[1]
    tri = jnp.asarray(np.triu(np.ones((rows, rows), np.float32), 1), jnp.bfloat16)
    full = lambda shape: pl.BlockSpec(shape, lambda b, s: (0,) * len(shape))
    smem = pl.BlockSpec(memory_space=pltpu.SMEM)
    tok_spec = pl.BlockSpec((1, rows, d_model), lambda b, s: (b, s, 0))
    tile_spec = pl.BlockSpec((rows * SUBLANES, LANES), lambda b, s: (b * ns + s, 0))
    sel_spec = pl.BlockSpec((1, TOP_K, rows), lambda b, s: (b * ns + s, 0, 0))
    cnt_spec = pl.BlockSpec((1, N_EXPERTS, LANES), lambda b, s: ((b * ns + s) // tiles_per_chunk, 0, 0))
    t_total = bsz * seq
    sel_shape = (bsz * ns, TOP_K, rows)
    return pl.pallas_call(
        functools.partial(_mixer_kernel, tiles_per_chunk),
        grid=(bsz, ns),
        in_specs=[smem, smem, tok_spec, full((1, d_model)), full((d_model, in_w)), full((1, in_w)),
                  full((CONV_K, conv_w)), full((1, LANES)), full((1, LANES)), full((ATT_BLOCK, 2 * ATT_BLOCK)),
                  full((1, conv_w)), full((1, conv_w)), full((d_model, d_model)), full((1, d_model)),
                  full((1, d_model)), full((N_EXPERTS, d_model)), full((N_EXPERTS, 1)), full((rows, rows))],
        out_specs=[tile_spec, tile_spec, sel_spec, sel_spec, sel_spec, cnt_spec],
        out_shape=[jax.ShapeDtypeStruct((t_total * SUBLANES, LANES), jnp.float32),
                   jax.ShapeDtypeStruct((t_total * SUBLANES, LANES), jnp.float32),
                   jax.ShapeDtypeStruct(sel_shape, jnp.int32),
                   jax.ShapeDtypeStruct(sel_shape, jnp.float32),
                   jax.ShapeDtypeStruct(sel_shape, jnp.int32),
                   jax.ShapeDtypeStruct((nchunks, N_EXPERTS, LANES), jnp.int32)],
        scratch_shapes=[pltpu.VMEM((N_KV_HEADS, ATT_BLOCK + rows, LANES), jnp.bfloat16),
                        pltpu.VMEM((N_KV_HEADS, ATT_BLOCK + rows, LANES), jnp.bfloat16),
                        pltpu.VMEM((SUBLANES, conv_w), jnp.float32),
                        pltpu.VMEM((2, N_KV_HEADS, Q_PER_KV * ATT_BLOCK, 2 * ATT_BLOCK), jnp.float32),
                        pltpu.VMEM((N_Q_HEADS, rows, LANES), jnp.bfloat16),
                        pltpu.VMEM((N_EXPERTS, LANES), jnp.float32)],
        compiler_params=pltpu.CompilerParams(dimension_semantics=("arbitrary", "arbitrary"),
                                             vmem_limit_bytes=VMEM_LIMIT_BYTES),
        name="mixer",
    )(p["sinks"], p["rel_bias"], x, p["gmix"], p["win"], p["bin"], p["convw"], p["gq2"], p["gk2"], bucket,
      p["gc"], p["ga"], p["wout"], p["bout"], p["gffn"], p["wrt"], p["br"], tri)


def _moe_kernel(nbc_ref, bexp_ref, bfirst_ref, bslot_ref, bnext_ref, firstexp_ref,
                x1_hbm, h_hbm, dest_hbm, gate_hbm, fill_hbm, w1_hbm, w2_hbm, b1_ref, b2_ref,
                out_hbm,
                acc, hbuf, w1buf, w2buf, gbuf0, gbuf1, rbuf0, rbuf1, dest_s, gate_s, list_s,
                io_sem, w_sem, l_sem):
    chunk = hbuf.shape[0] // SUBLANES
    pad_block = bexp_ref.shape[1] - 1
    n_assign = TOP_K * chunk
    d_model = w1buf.shape[1]
    d_ff = w2buf.shape[1]
    nk = d_model // LANES
    c = pl.program_id(0)
    row0 = pl.multiple_of(c * (chunk * SUBLANES), SUBLANES)

    def weight_copies(e, slot):
        return (pltpu.make_async_copy(w1_hbm.at[e], w1buf.at[slot], w_sem.at[0, slot]),
                pltpu.make_async_copy(w2_hbm.at[e], w2buf.at[slot], w_sem.at[1, slot]))

    def chunk_in_copies():
        return (pltpu.make_async_copy(x1_hbm.at[pl.ds(row0, chunk * SUBLANES)],
                                      acc.at[pl.ds(0, chunk * SUBLANES)], io_sem.at[0]),
                pltpu.make_async_copy(h_hbm.at[pl.ds(row0, chunk * SUBLANES)], hbuf, io_sem.at[1]))

    def list_copies():
        return (pltpu.make_async_copy(dest_hbm.at[c], dest_s, l_sem.at[0]),
                pltpu.make_async_copy(gate_hbm.at[c], gate_s.at[:, pl.ds(0, n_assign)], l_sem.at[1]),
                pltpu.make_async_copy(fill_hbm.at[0], list_s, l_sem.at[2]))

    for cp in list_copies() + chunk_in_copies():
        cp.start()

    @pl.when(c == 0)
    def _first_weights():
        for cp in weight_copies(firstexp_ref[0], 0):
            cp.start()

    gate_s[0, n_assign] = jnp.float32(0.0)
    for cp in list_copies():
        cp.wait()

    def invert_body(i, carry):
        base = i * INVERT_UNROLL
        for jj in range(INVERT_UNROLL):
            a = base + jj
            list_s[0, dest_s[0, a]] = a
        return carry

    lax.fori_loop(0, n_assign // INVERT_UNROLL, invert_body, 0)

    acc[pl.ds(chunk * SUBLANES, SUBLANES), :] = jnp.zeros((SUBLANES, LANES), jnp.float32)
    rbuf1[...] = jnp.zeros(rbuf1.shape, rbuf1.dtype)
    for cp in chunk_in_copies():
        cp.wait()

    log2_assign = int(math.log2(n_assign))

    def weights_step(g):
        @pl.when(bfirst_ref[c, g] == 1)
        def _():
            slot = bslot_ref[c, g]
            for cp in weight_copies(bexp_ref[c, g], slot):
                cp.wait()
            nxt = bnext_ref[c, g]

            @pl.when(nxt >= 0)
            def _():
                for cp in weight_copies(nxt, 1 - slot):
                    cp.start()

    def gather_rows(g, gbuf, j_lo, j_hi):
        s0 = g * MOE_ROWS
        for j in range(j_lo, j_hi):
            t = list_s[0, s0 + j] & (chunk - 1)
            src = pl.multiple_of(t * SUBLANES, SUBLANES)
            gbuf[pl.ds(j * SUBLANES, SUBLANES), :] = hbuf[pl.ds(src, SUBLANES), :]

    def scatter_rows(g, rbuf, j_lo, j_hi):
        s0 = g * MOE_ROWS
        for j0 in range(j_lo, j_hi, SCATTER_UNROLL):
            dsts, vals = [], []
            for j in range(j0, j0 + SCATTER_UNROLL):
                a = list_s[0, s0 + j]
                t = (a & (chunk - 1)) + (a >> log2_assign) * chunk
                dst = pl.multiple_of(t * SUBLANES, SUBLANES)
                dsts.append(dst)
                vals.append(acc[pl.ds(dst, SUBLANES), :] + gate_s[0, a] * rbuf[pl.ds(j * SUBLANES, SUBLANES), :])
            for dst, val in zip(dsts, vals):
                acc[pl.ds(dst, SUBLANES), :] = val

    n_hid = 2 * d_ff // MXU_COLS
    n_out = d_model // MXU_COLS
    rows_hid = MOE_ROWS // 2 // n_hid
    rows_out = MOE_ROWS // 2 // n_out

    def stage(g_gather, gbuf_in, g_compute, gbuf, rbuf, g_scatter, rbuf_out):
        e = bexp_ref[c, g_compute]
        slot = bslot_ref[c, g_compute]
        xs = [gbuf[pl.ds(kk, MOE_ROWS, stride=SUBLANES), :] for kk in range(nk)]
        xb = jnp.concatenate(xs, axis=-1).astype(jnp.bfloat16)
        acts = []
        for cg in range(n_hid):
            cols = pl.ds(cg * MXU_COLS, MXU_COLS)
            hid = jnp.dot(xb, w1buf[slot, :, cols], preferred_element_type=jnp.float32) + b1_ref[e, :, cols]
            glu = jnp.minimum(hid[:, :LANES], SWIGLU_LIMIT)
            lin = jnp.clip(hid[:, LANES:], -SWIGLU_LIMIT, SWIGLU_LIMIT)
            acts.append(glu * jax.nn.sigmoid(SWIGLU_ALPHA * glu) * (lin + 1.0))
            gather_rows(g_gather, gbuf_in, cg * rows_hid, (cg + 1) * rows_hid)
            scatter_rows(g_scatter, rbuf_out, cg * rows_hid, (cg + 1) * rows_hid)
        act = jnp.concatenate(acts, axis=-1).astype(jnp.bfloat16)
        half = MOE_ROWS // 2
        for og in range(n_out):
            cols = pl.ds(og * MXU_COLS, MXU_COLS)
            y = jnp.dot(act, w2buf[slot, :, cols], preferred_element_type=jnp.float32) + b2_ref[e, :, cols]
            for kk in range(MXU_COLS // LANES):
                rbuf[pl.ds(og * (MXU_COLS // LANES) + kk, MOE_ROWS, stride=SUBLANES), :] = (
                    y[:, kk * LANES:(kk + 1) * LANES])
            gather_rows(g_gather, gbuf_in, half + og * rows_out, half + (og + 1) * rows_out)
            scatter_rows(g_scatter, rbuf_out, half + og * rows_out, half + (og + 1) * rows_out)

    gather_rows(0, gbuf0, 0, MOE_ROWS)

    def pair_body(p, carry):
        g0 = 2 * p
        weights_step(g0)
        stage(g0 + 1, gbuf1, g0, gbuf0, rbuf0, jnp.where(p == 0, pad_block, g0 - 1), rbuf1)
        weights_step(g0 + 1)
        stage(g0 + 2, gbuf0, g0 + 1, gbuf1, rbuf1, g0, rbuf0)
        return carry

    npairs = (nbc_ref[c] + 1) // 2
    lax.fori_loop(0, npairs, pair_body, 0)
    scatter_rows(2 * npairs - 1, rbuf1, 0, MOE_ROWS)

    out_cp = pltpu.make_async_copy(acc.at[pl.ds(0, chunk * SUBLANES)],
                                   out_hbm.at[pl.ds(row0, chunk * SUBLANES)], io_sem.at[0])
    out_cp.start()
    out_cp.wait()


def _block_tables(nblk, table_blocks):
    nchunks = nblk.shape[0]
    bend = jnp.cumsum(nblk, axis=1)
    nbc = bend[:, -1]
    g = jnp.arange(table_blocks, dtype=jnp.int32)
    nonempty = nblk > 0
    last_exp = jnp.max(jnp.where(nonempty, jnp.arange(N_EXPERTS, dtype=jnp.int32)[None, :], 0), axis=1)
    bexp = jnp.sum((g[None, :, None] >= bend[:, None, :]).astype(jnp.int32), axis=-1)
    bexp = jnp.where(g[None, :] >= nbc[:, None], last_exp[:, None], bexp)
    bstart = bend - nblk
    bfirst = ((g[None, :] == jnp.take_along_axis(bstart, bexp, axis=1)) & (g[None, :] < nbc[:, None])).astype(jnp.int32)
    flat = nonempty.reshape(-1)
    pair_slot = ((jnp.cumsum(flat.astype(jnp.int32)) - 1) % 2).reshape(nchunks, N_EXPERTS)
    n_pairs = flat.shape[0]
    idx = jnp.where(flat, jnp.arange(n_pairs, dtype=jnp.int32), n_pairs)
    later = jnp.concatenate([lax.cummin(idx[::-1])[::-1][1:], jnp.full((1,), n_pairs, jnp.int32)])
    pair_next = jnp.where(later < n_pairs, later % N_EXPERTS, -1).reshape(nchunks, N_EXPERTS)
    bslot = jnp.take_along_axis(pair_slot, bexp, axis=1)
    bnext = jnp.take_along_axis(pair_next, bexp, axis=1)
    first_exp = (jnp.min(idx) % N_EXPERTS).reshape(1)
    return nbc.astype(jnp.int32), bexp, bfirst, bslot.astype(jnp.int32), bnext.astype(jnp.int32), first_exp


def _moe(x1r, hr, dest, gates, tables, w1, w2, b1, b2, chunk, table_blocks):
    total_rows = x1r.shape[0]
    nchunks = total_rows // (chunk * SUBLANES)
    d_model = w1.shape[1]
    d_ff = w2.shape[1]
    n_assign = TOP_K * chunk
    n_slots = table_blocks * MOE_ROWS
    fill = jnp.full((1, 1, n_slots), n_assign, jnp.int32)
    anyspace = pl.BlockSpec(memory_space=pl.ANY)
    grid_spec = pltpu.PrefetchScalarGridSpec(
        num_scalar_prefetch=6,
        grid=(nchunks,),
        in_specs=[anyspace] * 7 + [pl.BlockSpec((N_EXPERTS, 1, 2 * d_ff), lambda c, *_: (0, 0, 0)),
                                   pl.BlockSpec((N_EXPERTS, 1, d_model), lambda c, *_: (0, 0, 0))],
        out_specs=anyspace,
        scratch_shapes=[pltpu.VMEM(((chunk + 1) * SUBLANES, LANES), jnp.float32),
                        pltpu.VMEM((chunk * SUBLANES, LANES), jnp.float32),
                        pltpu.VMEM((2, d_model, 2 * d_ff), jnp.bfloat16),
                        pltpu.VMEM((2, d_ff, d_model), jnp.bfloat16),
                        pltpu.VMEM((MOE_ROWS * SUBLANES, LANES), jnp.float32),
                        pltpu.VMEM((MOE_ROWS * SUBLANES, LANES), jnp.float32),
                        pltpu.VMEM((MOE_ROWS * SUBLANES, LANES), jnp.float32),
                        pltpu.VMEM((MOE_ROWS * SUBLANES, LANES), jnp.float32),
                        pltpu.SMEM((1, n_assign), jnp.int32),
                        pltpu.SMEM((1, n_assign + LANES), jnp.float32),
                        pltpu.SMEM((1, n_slots), jnp.int32),
                        pltpu.SemaphoreType.DMA((2,)),
                        pltpu.SemaphoreType.DMA((2, 2)),
                        pltpu.SemaphoreType.DMA((3,))],
    )
    return pl.pallas_call(
        _moe_kernel,
        grid_spec=grid_spec,
        out_shape=jax.ShapeDtypeStruct(x1r.shape, jnp.float32),
        compiler_params=pltpu.CompilerParams(dimension_semantics=("arbitrary",),
                                             vmem_limit_bytes=VMEM_LIMIT_BYTES),
        name="moe",
    )(*tables, x1r, hr, dest, gates, fill, w1, w2, b1, b2)


def _t5_bucket_table():
    i = np.arange(ATT_BLOCK)[:, None]
    j = np.arange(2 * ATT_BLOCK)[None, :]
    rel = i + ATT_BLOCK - j
    max_exact = N_BUCKETS // 2
    nf = np.maximum(rel, 1).astype(np.float32)
    large = max_exact + (np.log(nf / max_exact) / np.float32(math.log(MAX_DISTANCE / max_exact))
                         * (N_BUCKETS - max_exact)).astype(np.int32)
    large = np.minimum(large, N_BUCKETS - 1)
    bucket = np.where(rel < max_exact, rel, large)
    return np.where((rel >= 0) & (rel < WINDOW), bucket, -1).astype(np.int32)


def _chunk_major(a, nchunks):
    tiles, _, rows = a.shape
    a = a.reshape(nchunks, tiles // nchunks, TOP_K, rows)
    return jnp.transpose(a, (0, 2, 1, 3)).reshape(nchunks, 1, -1)


def kernel(x, norm_mix, w_in, b_in, conv_w, q_norm, k_norm, sinks, rel_bias, out_norm_conv, out_norm_attn,
           w_out, b_out, norm_ffn, w_router, b_router, w1, b1, w2, b2):
    bsz, seq, d_model = x.shape
    depth = w_in.shape[0]
    bf16 = jnp.bfloat16
    bucket = jnp.asarray(_t5_bucket_table())
    t_total = bsz * seq
    chunk = min(MOE_CHUNK, t_total)
    nchunks = t_total // chunk
    table_blocks = chunk * TOP_K // MOE_ROWS + N_EXPERTS + 2
    for l in range(depth):
        d_ff = w2.shape[2]
        params = dict(
            sinks=sinks[l], rel_bias=rel_bias, gmix=norm_mix[l][None], win=w_in[l].astype(bf16), bin=b_in[l][None],
            convw=conv_w[l], gq2=jnp.tile(q_norm[l], 2)[None] * (HEAD_DIM ** -0.5), gk2=jnp.tile(k_norm[l], 2)[None],
            gc=out_norm_conv[l][None], ga=out_norm_attn[l][None], wout=w_out[l].astype(bf16), bout=b_out[l][None],
            gffn=norm_ffn[l][None], wrt=w_router[l].T, br=b_router[l][:, None])
        x1r, hr, ids_t, gates_t, rank_t, counts_t = _mixer(x, params, bucket, chunk)
        counts = counts_t[:, :, 0]
        padded = (counts + MOE_ROWS - 1) // MOE_ROWS * MOE_ROWS
        pad_starts = jnp.cumsum(padded, axis=1) - padded
        tables = _block_tables(padded // MOE_ROWS, table_blocks)
        ids_c = _chunk_major(ids_t, nchunks)
        dest = _chunk_major(rank_t, nchunks)
        for e in range(N_EXPERTS):
            dest = dest + jnp.where(ids_c == e, pad_starts[:, e:e + 1, None], 0)
        w1p, w2p = _wprep(w1[l], w2[l])
        b1p = jnp.transpose(b1[l].reshape(N_EXPERTS, 2 * d_ff // MXU_COLS, LANES, 2), (0, 1, 3, 2))
        b1p = b1p.reshape(N_EXPERTS, 1, 2 * d_ff)
        outr = _moe(x1r, hr, dest, _chunk_major(gates_t, nchunks), tables, w1p, w2p, b1p,
                    b2[l][:, None, :], chunk, table_blocks)
        x = outr.reshape(bsz, seq, d_model)
    return x
```

```python
import functools
import math

import jax
import jax.numpy as jnp
import numpy as np
from jax import lax
from jax.experimental import pallas as pl
from jax.experimental.pallas import tpu as pltpu

HEAD_DIM = 64
N_Q_HEADS = 8
N_KV_HEADS = 2
Q_PER_KV = N_Q_HEADS // N_KV_HEADS
CONV_K = 3
WINDOW = 128
ATT_BLOCK = 128
N_BUCKETS = 32
MAX_DISTANCE = 128
N_EXPERTS = 32
TOP_K = 4
SWIGLU_LIMIT = 7.0
SWIGLU_ALPHA = 1.702
EPS = 1e-5
MASK_VALUE = -1e30

LANES = 128
SUBLANES = 8
MXU_COLS = 256
VMEM_LIMIT_BYTES = 56 * 1024 * 1024

MIX_ROWS = 512
MOE_CHUNK = 4096
MOE_ROWS = 128
PREP_ROWS = 256
INVERT_UNROLL = 16
SCATTER_UNROLL = 4


def _rms(x, g):
    return x * lax.rsqrt(jnp.mean(x * x, axis=-1, keepdims=True) + EPS) * g


def _half_head_norm(t, gain2, lo):
    t2 = t * t
    s_lo = jnp.sum(jnp.where(lo, t2, 0.0), axis=-1, keepdims=True)
    s_hi = jnp.sum(jnp.where(lo, 0.0, t2), axis=-1, keepdims=True)
    r = jnp.where(lo, lax.rsqrt(s_lo * (1.0 / HEAD_DIM) + EPS), lax.rsqrt(s_hi * (1.0 / HEAD_DIM) + EPS))
    return t * r * gain2


def _store_row_tiles(ref, val):
    rows = val.shape[0]
    for kk in range(val.shape[1] // LANES):
        ref[pl.ds(kk, rows, stride=SUBLANES), :] = val[:, kk * LANES:(kk + 1) * LANES]


def _wprep_kernel(w1_ref, w2_ref, perm_ref, w1o_ref, w2o_ref):
    perm = perm_ref[...]
    for c in range(w1_ref.shape[2] // MXU_COLS):
        cols = pl.ds(c * MXU_COLS, MXU_COLS)
        t = w1_ref[0, :, cols].astype(jnp.bfloat16)
        w1o_ref[0, :, cols] = jnp.dot(t, perm, preferred_element_type=jnp.float32).astype(jnp.bfloat16)
    w2o_ref[0] = w2_ref[0].astype(jnp.bfloat16)


def _wprep(w1, w2):
    n_e, d_model, two_ff = w1.shape
    d_ff, d_out = w2.shape[1], w2.shape[2]
    assert d_model == d_ff, "one row grid serves both weight tensors"
    j = np.arange(MXU_COLS)
    src = np.where(j < LANES, 2 * j, 2 * (j - LANES) + 1)
    perm = jnp.asarray((np.arange(MXU_COLS)[:, None] == src[None, :]).astype(np.float32), jnp.bfloat16)
    return pl.pallas_call(
        _wprep_kernel,
        grid=(n_e, d_model // PREP_ROWS),
        in_specs=[pl.BlockSpec((1, PREP_ROWS, two_ff), lambda e, r: (e, r, 0)),
                  pl.BlockSpec((1, PREP_ROWS, d_out), lambda e, r: (e, r, 0)),
                  pl.BlockSpec((MXU_COLS, MXU_COLS), lambda e, r: (0, 0))],
        out_specs=[pl.BlockSpec((1, PREP_ROWS, two_ff), lambda e, r: (e, r, 0)),
                   pl.BlockSpec((1, PREP_ROWS, d_out), lambda e, r: (e, r, 0))],
        out_shape=[jax.ShapeDtypeStruct(w1.shape, jnp.bfloat16), jax.ShapeDtypeStruct(w2.shape, jnp.bfloat16)],
        compiler_params=pltpu.CompilerParams(dimension_semantics=("arbitrary", "arbitrary")),
        name="wprep",
    )(w1, w2, perm)


def _mixer_kernel(tiles_per_chunk,
                  sinks_ref, relb_ref,
                  x_ref, gmix_ref, win_ref, bin_ref, convw_ref, gq_ref, gk_ref, bucket_ref,
                  gc_ref, ga_ref, wout_ref, bout_ref, gffn_ref, wrt_ref, br_ref, tri_ref,
                  x1_ref, h2_ref, ids_ref, gates_ref, rank_ref, counts_ref,
                  kd_scr, vd_scr, cu_scr, bias_scr, qm_scr, cnt_scr):
    rows = x_ref.shape[1]
    d_model = x_ref.shape[2]
    conv_w = d_model // 2
    nblk = rows // ATT_BLOCK
    b = pl.program_id(0)
    s = pl.program_id(1)
    tile = b * pl.num_programs(1) + s

    @pl.when((b == 0) & (s == 0))
    def _build_bias():
        bk = bucket_ref[...]
        col = lax.broadcasted_iota(jnp.int32, bk.shape, 1)
        accs = [jnp.full(bk.shape, MASK_VALUE, jnp.float32) for _ in range(N_Q_HEADS)]
        for bb in range(N_BUCKETS):
            hit = bk == bb
            for h in range(N_Q_HEADS):
                accs[h] = jnp.where(hit, relb_ref[bb, h], accs[h])
        for h in range(N_Q_HEADS):
            g, j = divmod(h, Q_PER_KV)
            bias_scr[0, g, pl.ds(j * ATT_BLOCK, ATT_BLOCK), :] = accs[h]
            bias_scr[1, g, pl.ds(j * ATT_BLOCK, ATT_BLOCK), :] = jnp.where(col < ATT_BLOCK, MASK_VALUE, accs[h])

    @pl.when(s == 0)
    def _reset_state():
        kd_scr[:, pl.ds(0, ATT_BLOCK), :] = jnp.zeros((N_KV_HEADS, ATT_BLOCK, LANES), kd_scr.dtype)
        vd_scr[:, pl.ds(0, ATT_BLOCK), :] = jnp.zeros((N_KV_HEADS, ATT_BLOCK, LANES), vd_scr.dtype)
        cu_scr[...] = jnp.zeros(cu_scr.shape, cu_scr.dtype)

    @pl.when(lax.rem(tile, tiles_per_chunk) == 0)
    def _reset_counts():
        cnt_scr[...] = jnp.zeros(cnt_scr.shape, cnt_scr.dtype)

    x = x_ref[0]
    h = _rms(x, gmix_ref[...]).astype(jnp.bfloat16)
    proj = jnp.dot(h, win_ref[...], preferred_element_type=jnp.float32) + bin_ref[...]
    b_gate = proj[:, 0:conv_w]
    c_gate = proj[:, conv_w:2 * conv_w]
    u = proj[:, 2 * conv_w:3 * conv_w]
    q = proj[:, 3 * conv_w:4 * conv_w]
    k = proj[:, 4 * conv_w:4 * conv_w + LANES]
    v = proj[:, 4 * conv_w + LANES:4 * conv_w + 2 * LANES]

    cu = c_gate * u
    row = lax.broadcasted_iota(jnp.int32, cu.shape, 0)
    prev2 = cu_scr[SUBLANES - 2:SUBLANES - 1, :]
    prev1 = cu_scr[SUBLANES - 1:SUBLANES, :]
    r1 = jnp.where(row == 0, prev1, pltpu.roll(cu, 1, 0))
    r2 = jnp.where(row == 0, prev2, jnp.where(row == 1, prev1, pltpu.roll(cu, 2, 0)))
    y_conv = b_gate * (convw_ref[2:3, :] * cu + convw_ref[1:2, :] * r1 + convw_ref[0:1, :] * r2)
    cu_scr[...] = cu[rows - SUBLANES:rows, :]

    lane = lax.broadcasted_iota(jnp.int32, (rows, LANES), 1)
    lo = lane < HEAD_DIM
    kn = _half_head_norm(k, gk_ref[...], lo)
    krot = pltpu.roll(kn, HEAD_DIM, 1)
    vrot = pltpu.roll(v, HEAD_DIM, 1)
    kd_scr[0, pl.ds(ATT_BLOCK, rows), :] = jnp.where(lo, kn, krot).astype(kd_scr.dtype)
    kd_scr[1, pl.ds(ATT_BLOCK, rows), :] = jnp.where(lo, krot, kn).astype(kd_scr.dtype)
    vd_scr[0, pl.ds(ATT_BLOCK, rows), :] = jnp.where(lo, v, vrot).astype(vd_scr.dtype)
    vd_scr[1, pl.ds(ATT_BLOCK, rows), :] = jnp.where(lo, vrot, v).astype(vd_scr.dtype)
    for c in range(conv_w // LANES):
        qn = _half_head_norm(q[:, c * LANES:(c + 1) * LANES], gq_ref[...], lo)
        qm_scr[2 * c] = jnp.where(lo, qn, 0.0).astype(qm_scr.dtype)
        qm_scr[2 * c + 1] = jnp.where(lo, 0.0, qn).astype(qm_scr.dtype)

    lo_b = lax.broadcasted_iota(jnp.int32, (ATT_BLOCK, LANES), 1) < HEAD_DIM
    ya_blocks = []
    for n in range(nblk):
        first = jnp.where(s == 0, 1, 0) if n == 0 else 0
        tiles = []
        for g in range(N_KV_HEADS):
            qs = jnp.concatenate([qm_scr[Q_PER_KV * g + j, pl.ds(n * ATT_BLOCK, ATT_BLOCK), :]
                                  for j in range(Q_PER_KV)], axis=0)
            kw = kd_scr[g, pl.ds(n * ATT_BLOCK, 2 * ATT_BLOCK), :]
            vw = vd_scr[g, pl.ds(n * ATT_BLOCK, 2 * ATT_BLOCK), :]
            sc = lax.dot_general(qs, kw, (((1,), (1,)), ((), ())), preferred_element_type=jnp.float32)
            logits = sc + bias_scr[first, g]
            ps, inv = [], []
            for j in range(Q_PER_KV):
                lj = logits[j * ATT_BLOCK:(j + 1) * ATT_BLOCK]
                sink = sinks_ref[Q_PER_KV * g + j]
                m = jnp.maximum(jnp.max(lj, axis=-1, keepdims=True), sink)
                p = jnp.exp(lj - m)
                den = jnp.sum(p, axis=-1, keepdims=True) + jnp.exp(sink - m)
                ps.append(p.astype(jnp.bfloat16))
                inv.append(1.0 / den)
            o = jnp.dot(jnp.concatenate(ps, axis=0), vw, preferred_element_type=jnp.float32)
            on = [o[j * ATT_BLOCK:(j + 1) * ATT_BLOCK] * inv[j] for j in range(Q_PER_KV)]
            tiles.append(jnp.where(lo_b, on[0], on[1]))
            tiles.append(jnp.where(lo_b, on[2], on[3]))
        ya_blocks.append(jnp.concatenate(tiles, axis=-1))
    y_attn = jnp.concatenate(ya_blocks, axis=0)

    for g in range(N_KV_HEADS):
        kd_scr[g, pl.ds(0, ATT_BLOCK), :] = kd_scr[g, pl.ds(rows, ATT_BLOCK), :]
        vd_scr[g, pl.ds(0, ATT_BLOCK), :] = vd_scr[g, pl.ds(rows, ATT_BLOCK), :]

    mix = jnp.concatenate([_rms(y_conv, gc_ref[...]), _rms(y_attn, ga_ref[...])], axis=-1)
    x1 = x + jnp.dot(mix.astype(jnp.bfloat16), wout_ref[...], preferred_element_type=jnp.float32) + bout_ref[...]
    _store_row_tiles(x1_ref, x1)
    h2 = _rms(x1, gffn_ref[...])
    _store_row_tiles(h2_ref, h2)
    lt = lax.dot_general(wrt_ref[...], h2, (((1,), (1,)), ((), ())), precision=lax.Precision.HIGHEST,
                         preferred_element_type=jnp.float32) + br_ref[...]
    eid = lax.broadcasted_iota(jnp.int32, lt.shape, 0)
    vals, idxs, hits = [], [], []
    for _ in range(TOP_K):
        m = jnp.max(lt, axis=0, keepdims=True)
        idx = jnp.min(jnp.where(lt == m, eid, N_EXPERTS), axis=0, keepdims=True)
        hit = eid == idx
        vals.append(m)
        idxs.append(idx)
        hits.append(hit)
        lt = jnp.where(hit, -jnp.inf, lt)
    es = [jnp.exp(vv - vals[0]) for vv in vals]
    tot = es[0] + es[1] + es[2] + es[3]
    ids_ref[0] = jnp.concatenate(idxs, axis=0)
    gates_ref[0] = jnp.concatenate([e / tot for e in es], axis=0)

    member = jnp.where(hits[0] | hits[1] | hits[2] | hits[3], 1.0, 0.0)
    before = jnp.dot(member.astype(jnp.bfloat16), tri_ref[...], preferred_element_type=jnp.float32)
    running = cnt_scr[...][:, 0:1]
    rank_e = before + running
    rank_ref[0] = jnp.concatenate([jnp.sum(jnp.where(hh, rank_e, 0.0), axis=0, keepdims=True) for hh in hits],
                                  axis=0).astype(jnp.int32)
    new_counts = cnt_scr[...] + jnp.sum(member, axis=1, keepdims=True)
    cnt_scr[...] = new_counts
    counts_ref[0] = new_counts.astype(jnp.int32)


def _mixer(x, p, bucket, chunk):
    bsz, seq, d_model = x.shape
    rows = min(MIX_ROWS, seq)
    ns = seq // rows
    tiles_per_chunk = chunk // rows
    nchunks = bsz * seq // chunk
    conv_w = d_model // 2
    in_w = p["win"].shape[1]
    tri = jnp.asarray(np.triu(np.ones((rows, rows), np.float32), 1), jnp.bfloat16)
    full = lambda shape: pl.BlockSpec(shape, lambda b, s: (0,) * len(shape))
    smem = pl.BlockSpec(memory_space=pltpu.SMEM)
    tok_spec = pl.BlockSpec((1, rows, d_model), lambda b, s: (b, s, 0))
    tile_spec = pl.BlockSpec((rows * SUBLANES, LANES), lambda b, s: (b * ns + s, 0))
    sel_spec = pl.BlockSpec((1, TOP_K, rows), lambda b, s: (b * ns + s, 0, 0))
    cnt_spec = pl.BlockSpec((1, N_EXPERTS, LANES), lambda b, s: ((b * ns + s) // tiles_per_chunk, 0, 0))
    t_total = bsz * seq
    sel_shape = (bsz * ns, TOP_K, rows)
    return pl.pallas_call(
        functools.partial(_mixer_kernel, tiles_per_chunk),
        grid=(bsz, ns),
        in_specs=[smem, smem, tok_spec, full((1, d_model)), full((d_model, in_w)), full((1, in_w)),
                  full((CONV_K, conv_w)), full((1, LANES)), full((1, LANES)), full((ATT_BLOCK, 2 * ATT_BLOCK)),
                  full((1, conv_w)), full((1, conv_w)), full((d_model, d_model)), full((1, d_model)),
                  full((1, d_model)), full((N_EXPERTS, d_model)), full((N_EXPERTS, 1)), full((rows, rows))],
        out_specs=[tile_spec, tile_spec, sel_spec, sel_spec, sel_spec, cnt_spec],
        out_shape=[jax.ShapeDtypeStruct((t_total * SUBLANES, LANES), jnp.float32),
                   jax.ShapeDtypeStruct((t_total * SUBLANES, LANES), jnp.float32),
                   jax.ShapeDtypeStruct(sel_shape, jnp.int32),
                   jax.ShapeDtypeStruct(sel_shape, jnp.float32),
                   jax.ShapeDtypeStruct(sel_shape, jnp.int32),
                   jax.ShapeDtypeStruct((nchunks, N_EXPERTS, LANES), jnp.int32)],
        scratch_shapes=[pltpu.VMEM((N_KV_HEADS, ATT_BLOCK + rows, LANES), jnp.bfloat16),
                        pltpu.VMEM((N_KV_HEADS, ATT_BLOCK + rows, LANES), jnp.bfloat16),
                        pltpu.VMEM((SUBLANES, conv_w), jnp.float32),
                        pltpu.VMEM((2, N_KV_HEADS, Q_PER_KV * ATT_BLOCK, 2 * ATT_BLOCK), jnp.float32),
                        pltpu.VMEM((N_Q_HEADS, rows, LANES), jnp.bfloat16),
                        pltpu.VMEM((N_EXPERTS, LANES), jnp.float32)],
        compiler_params=pltpu.CompilerParams(dimension_semantics=("arbitrary", "arbitrary"),
                                             vmem_limit_bytes=VMEM_LIMIT_BYTES),
        name="mixer",
    )(p["sinks"], p["rel_bias"], x, p["gmix"], p["win"], p["bin"], p["convw"], p["gq2"], p["gk2"], bucket,
      p["gc"], p["ga"], p["wout"], p["bout"], p["gffn"], p["wrt"], p["br"], tri)


def _moe_kernel(nbc_ref, bexp_ref, bfirst_ref, bslot_ref, bnext_ref, firstexp_ref,
                x1_hbm, h_hbm, dest_hbm, gate_hbm, tokfill_hbm, gatefill_hbm, w1_hbm, w2_hbm, b1_ref, b2_ref,
                out_hbm,
                acc, hbuf, w1buf, w2buf, gbuf0, gbuf1, rbuf0, rbuf1, dest_s, gate_s, tok_s, gl_s,
                io_sem, w_sem, l_sem):
    chunk = hbuf.shape[0] - 1
    pad_block = bexp_ref.shape[1] - 1
    d_model = w1buf.shape[1]
    d_ff = w2buf.shape[1]
    nk = d_model // LANES
    c = pl.program_id(0)

    def weight_copies(e, slot):
        return (pltpu.make_async_copy(w1_hbm.at[e], w1buf.at[slot], w_sem.at[0, slot]),
                pltpu.make_async_copy(w2_hbm.at[e], w2buf.at[slot], w_sem.at[1, slot]))

    def chunk_in_copies():
        rows = pl.ds(c * chunk, chunk)
        return (pltpu.make_async_copy(x1_hbm.at[rows], acc.at[pl.ds(0, chunk)], io_sem.at[0]),
                pltpu.make_async_copy(h_hbm.at[rows], hbuf.at[pl.ds(0, chunk)], io_sem.at[1]))

    def list_copies():
        return (pltpu.make_async_copy(dest_hbm.at[c], dest_s, l_sem.at[0]),
                pltpu.make_async_copy(gate_hbm.at[c], gate_s, l_sem.at[1]),
                pltpu.make_async_copy(tokfill_hbm, tok_s, l_sem.at[2]),
                pltpu.make_async_copy(gatefill_hbm, gl_s, l_sem.at[3]))

    for cp in list_copies() + chunk_in_copies():
        cp.start()

    @pl.when(c == 0)
    def _first_weights():
        for cp in weight_copies(firstexp_ref[0], 0):
            cp.start()

    for cp in list_copies():
        cp.wait()
    for kk in range(TOP_K):
        def invert_body(i, carry, kk=kk):
            for jj in range(INVERT_UNROLL):
                t = i * INVERT_UNROLL + jj
                a = kk * chunk + t
                d = dest_s[0, a]
                tok_s[0, d] = t
                gl_s[0, d] = gate_s[0, a]
            return carry

        lax.fori_loop(0, chunk // INVERT_UNROLL, invert_body, 0)

    acc[chunk] = jnp.zeros((SUBLANES, LANES), jnp.float32)
    hbuf[chunk] = jnp.zeros((SUBLANES, LANES), jnp.float32)
    rbuf1[...] = jnp.zeros(rbuf1.shape, rbuf1.dtype)
    for cp in chunk_in_copies():
        cp.wait()

    def weights_step(g):
        @pl.when(bfirst_ref[c, g] == 1)
        def _():
            slot = bslot_ref[c, g]
            for cp in weight_copies(bexp_ref[c, g], slot):
                cp.wait()
            nxt = bnext_ref[c, g]

            @pl.when(nxt >= 0)
            def _():
                for cp in weight_copies(nxt, 1 - slot):
                    cp.start()

    def gather_rows(g, gbuf, j_lo, j_hi):
        s0 = g * MOE_ROWS
        for j in range(j_lo, j_hi):
            gbuf[pl.ds(j * SUBLANES, SUBLANES), :] = hbuf[tok_s[0, s0 + j]]

    def scatter_rows(g, rbuf, j_lo, j_hi):
        s0 = g * MOE_ROWS
        for j0 in range(j_lo, j_hi, SCATTER_UNROLL):
            toks, vals = [], []
            for j in range(j0, j0 + SCATTER_UNROLL):
                t = tok_s[0, s0 + j]
                toks.append(t)
                vals.append(acc[t] + gl_s[0, s0 + j] * rbuf[pl.ds(j * SUBLANES, SUBLANES), :])
            for t, val in zip(toks, vals):
                acc[t] = val

    n_hid = 2 * d_ff // MXU_COLS
    n_out = d_model // MXU_COLS
    rows_hid = MOE_ROWS // 2 // n_hid
    rows_out = MOE_ROWS // 2 // n_out

    def stage(g_gather, gbuf_in, g_compute, gbuf, rbuf, g_scatter, rbuf_out):
        e = bexp_ref[c, g_compute]
        slot = bslot_ref[c, g_compute]
        xs = [gbuf[pl.ds(kk, MOE_ROWS, stride=SUBLANES), :] for kk in range(nk)]
        xb = jnp.concatenate(xs, axis=-1).astype(jnp.bfloat16)
        acts = []
        for cg in range(n_hid):
            cols = pl.ds(cg * MXU_COLS, MXU_COLS)
            hid = jnp.dot(xb, w1buf[slot, :, cols], preferred_element_type=jnp.float32) + b1_ref[e, :, cols]
            glu = jnp.minimum(hid[:, :LANES], SWIGLU_LIMIT)
            lin = jnp.clip(hid[:, LANES:], -SWIGLU_LIMIT, SWIGLU_LIMIT)
            acts.append(glu * jax.nn.sigmoid(SWIGLU_ALPHA * glu) * (lin + 1.0))
            gather_rows(g_gather, gbuf_in, cg * rows_hid, (cg + 1) * rows_hid)
            scatter_rows(g_scatter, rbuf_out, cg * rows_hid, (cg + 1) * rows_hid)
        act = jnp.concatenate(acts, axis=-1).astype(jnp.bfloat16)
        half = MOE_ROWS // 2
        for og in range(n_out):
            cols = pl.ds(og * MXU_COLS, MXU_COLS)
            y = jnp.dot(act, w2buf[slot, :, cols], preferred_element_type=jnp.float32) + b2_ref[e, :, cols]
            for kk in range(MXU_COLS // LANES):
                rbuf[pl.ds(og * (MXU_COLS // LANES) + kk, MOE_ROWS, stride=SUBLANES), :] = (
                    y[:, kk * LANES:(kk + 1) * LANES])
            gather_rows(g_gather, gbuf_in, half + og * rows_out, half + (og + 1) * rows_out)
            scatter_rows(g_scatter, rbuf_out, half + og * rows_out, half + (og + 1) * rows_out)

    gather_rows(0, gbuf0, 0, MOE_ROWS)

    def pair_body(p, carry):
        g0 = 2 * p
        weights_step(g0)
        stage(g0 + 1, gbuf1, g0, gbuf0, rbuf0, jnp.where(p == 0, pad_block, g0 - 1), rbuf1)
        weights_step(g0 + 1)
        stage(g0 + 2, gbuf0, g0 + 1, gbuf1, rbuf1, g0, rbuf0)
        return carry

    npairs = (nbc_ref[c] + 1) // 2
    lax.fori_loop(0, npairs, pair_body, 0)
    scatter_rows(2 * npairs - 1, rbuf1, 0, MOE_ROWS)

    out_cp = pltpu.make_async_copy(acc.at[pl.ds(0, chunk)], out_hbm.at[pl.ds(c * chunk, chunk)], io_sem.at[0])
    out_cp.start()
    out_cp.wait()


def _block_tables(nblk, table_blocks):
    nchunks = nblk.shape[0]
    bend = jnp.cumsum(nblk, axis=1)
    nbc = bend[:, -1]
    g = jnp.arange(table_blocks, dtype=jnp.int32)
    nonempty = nblk > 0
    last_exp = jnp.max(jnp.where(nonempty, jnp.arange(N_EXPERTS, dtype=jnp.int32)[None, :], 0), axis=1)
    bexp = jnp.sum((g[None, :, None] >= bend[:, None, :]).astype(jnp.int32), axis=-1)
    bexp = jnp.where(g[None, :] >= nbc[:, None], last_exp[:, None], bexp)
    bstart = bend - nblk
    bfirst = ((g[None, :] == jnp.take_along_axis(bstart, bexp, axis=1)) & (g[None, :] < nbc[:, None])).astype(jnp.int32)
    flat = nonempty.reshape(-1)
    pair_slot = ((jnp.cumsum(flat.astype(jnp.int32)) - 1) % 2).reshape(nchunks, N_EXPERTS)
    n_pairs = flat.shape[0]
    idx = jnp.where(flat, jnp.arange(n_pairs, dtype=jnp.int32), n_pairs)
    later = jnp.concatenate([lax.cummin(idx[::-1])[::-1][1:], jnp.full((1,), n_pairs, jnp.int32)])
    pair_next = jnp.where(later < n_pairs, later % N_EXPERTS, -1).reshape(nchunks, N_EXPERTS)
    bslot = jnp.take_along_axis(pair_slot, bexp, axis=1)
    bnext = jnp.take_along_axis(pair_next, bexp, axis=1)
    first_exp = (jnp.min(idx) % N_EXPERTS).reshape(1)
    return nbc.astype(jnp.int32), bexp, bfirst, bslot.astype(jnp.int32), bnext.astype(jnp.int32), first_exp


def _moe(x1r, hr, dest, gates, tables, w1, w2, b1, b2, chunk, table_blocks):
    t_total = x1r.shape[0]
    nchunks = t_total // chunk
    d_model = w1.shape[1]
    d_ff = w2.shape[1]
    n_assign = TOP_K * chunk
    n_slots = table_blocks * MOE_ROWS
    tokfill = jnp.full((1, n_slots), chunk, jnp.int32)
    gatefill = jnp.zeros((1, n_slots), jnp.float32)
    anyspace = pl.BlockSpec(memory_space=pl.ANY)
    grid_spec = pltpu.PrefetchScalarGridSpec(
        num_scalar_prefetch=6,
        grid=(nchunks,),
        in_specs=[anyspace] * 8 + [pl.BlockSpec((N_EXPERTS, 1, 2 * d_ff), lambda c, *_: (0, 0, 0)),
                                   pl.BlockSpec((N_EXPERTS, 1, d_model), lambda c, *_: (0, 0, 0))],
        out_specs=anyspace,
        scratch_shapes=[pltpu.VMEM((chunk + 1, SUBLANES, LANES), jnp.float32),
                        pltpu.VMEM((chunk + 1, SUBLANES, LANES), jnp.float32),
                        pltpu.VMEM((2, d_model, 2 * d_ff), jnp.bfloat16),
                        pltpu.VMEM((2, d_ff, d_model), jnp.bfloat16),
                        pltpu.VMEM((MOE_ROWS * SUBLANES, LANES), jnp.float32),
                        pltpu.VMEM((MOE_ROWS * SUBLANES, LANES), jnp.float32),
                        pltpu.VMEM((MOE_ROWS * SUBLANES, LANES), jnp.float32),
                        pltpu.VMEM((MOE_ROWS * SUBLANES, LANES), jnp.float32),
                        pltpu.SMEM((1, n_assign), jnp.int32),
                        pltpu.SMEM((1, n_assign), jnp.float32),
                        pltpu.SMEM((1, n_slots), jnp.int32),
                        pltpu.SMEM((1, n_slots), jnp.float32),
                        pltpu.SemaphoreType.DMA((2,)),
                        pltpu.SemaphoreType.DMA((2, 2)),
                        pltpu.SemaphoreType.DMA((4,))],
    )
    return pl.pallas_call(
        _moe_kernel,
        grid_spec=grid_spec,
        out_shape=jax.ShapeDtypeStruct(x1r.shape, jnp.float32),
        compiler_params=pltpu.CompilerParams(dimension_semantics=("arbitrary",),
                                             vmem_limit_bytes=VMEM_LIMIT_BYTES),
        name="moe",
    )(*tables, x1r, hr, dest, gates, tokfill, gatefill, w1, w2, b1, b2)


def _t5_bucket_table():
    i = np.arange(ATT_BLOCK)[:, None]
    j = np.arange(2 * ATT_BLOCK)[None, :]
    rel = i + ATT_BLOCK - j
    max_exact = N_BUCKETS // 2
    nf = np.maximum(rel, 1).astype(np.float32)
    large = max_exact + (np.log(nf / max_exact) / np.float32(math.log(MAX_DISTANCE / max_exact))
                         * (N_BUCKETS - max_exact)).astype(np.int32)
    large = np.minimum(large, N_BUCKETS - 1)
    bucket = np.where(rel < max_exact, rel, large)
    return np.where((rel >= 0) & (rel < WINDOW), bucket, -1).astype(np.int32)


def _chunk_major(a, nchunks):
    tiles, _, rows = a.shape
    a = a.reshape(nchunks, tiles // nchunks, TOP_K, rows)
    return jnp.transpose(a, (0, 2, 1, 3)).reshape(nchunks, 1, -1)


def kernel(x, norm_mix, w_in, b_in, conv_w, q_norm, k_norm, sinks, rel_bias, out_norm_conv, out_norm_attn,
           w_out, b_out, norm_ffn, w_router, b_router, w1, b1, w2, b2):
    bsz, seq, d_model = x.shape
    depth = w_in.shape[0]
    bf16 = jnp.bfloat16
    bucket = jnp.asarray(_t5_bucket_table())
    t_total = bsz * seq
    chunk = min(MOE_CHUNK, t_total)
    nchunks = t_total // chunk
    table_blocks = chunk * TOP_K // MOE_ROWS + N_EXPERTS + 2
    for l in range(depth):
        d_ff = w2.shape[2]
        params = dict(
            sinks=sinks[l], rel_bias=rel_bias, gmix=norm_mix[l][None], win=w_in[l].astype(bf16), bin=b_in[l][None],
            convw=conv_w[l], gq2=jnp.tile(q_norm[l], 2)[None] * (HEAD_DIM ** -0.5), gk2=jnp.tile(k_norm[l], 2)[None],
            gc=out_norm_conv[l][None], ga=out_norm_attn[l][None], wout=w_out[l].astype(bf16), bout=b_out[l][None],
            gffn=norm_ffn[l][None], wrt=w_router[l].T, br=b_router[l][:, None])
        x1r, hr, ids_t, gates_t, rank_t, counts_t = _mixer(x, params, bucket, chunk)
        counts = counts_t[:, :, 0]
        padded = (counts + MOE_ROWS - 1) // MOE_ROWS * MOE_ROWS
        pad_starts = jnp.cumsum(padded, axis=1) - padded
        tables = _block_tables(padded // MOE_ROWS, table_blocks)
        ids_c = _chunk_major(ids_t, nchunks)
        dest = _chunk_major(rank_t, nchunks)
        for e in range(N_EXPERTS):
            dest = dest + jnp.where(ids_c == e, pad_starts[:, e:e + 1, None], 0)
        w1p, w2p = _wprep(w1[l], w2[l])
        b1p = jnp.transpose(b1[l].reshape(N_EXPERTS, 2 * d_ff // MXU_COLS, LANES, 2), (0, 1, 3, 2))
        b1p = b1p.reshape(N_EXPERTS, 1, 2 * d_ff)
        tile3 = (t_total, SUBLANES, LANES)
        outr = _moe(x1r.reshape(tile3), hr.reshape(tile3), dest, _chunk_major(gates_t, nchunks), tables, w1p, w2p,
                    b1p, b2[l][:, None, :], chunk, table_blocks)
        x = outr.reshape(bsz, seq, d_model)
    return x
```

```python
import functools
import math

import jax
import jax.numpy as jnp
import numpy as np
from jax import lax
from jax.experimental import pallas as pl
from jax.experimental.pallas import tpu as pltpu

HEAD_DIM = 64
N_Q_HEADS = 8
N_KV_HEADS = 2
Q_PER_KV = N_Q_HEADS // N_KV_HEADS
CONV_K = 3
WINDOW = 128
ATT_BLOCK = 128
N_BUCKETS = 32
MAX_DISTANCE = 128
N_EXPERTS = 32
TOP_K = 4
SWIGLU_LIMIT = 7.0
SWIGLU_ALPHA = 1.702
EPS = 1e-5
MASK_VALUE = -1e30

LANES = 128
SUBLANES = 8
MXU_COLS = 256
VMEM_LIMIT_BYTES = 56 * 1024 * 1024

MIX_ROWS = 512
MOE_CHUNK = 4096
MOE_ROWS = 256
PREP_ROWS = 256
INVERT_UNROLL = 16
SCATTER_UNROLL = 4


def _rms(x, g):
    return x * lax.rsqrt(jnp.mean(x * x, axis=-1, keepdims=True) + EPS) * g


def _half_head_norm(t, gain2, lo):
    t2 = t * t
    s_lo = jnp.sum(jnp.where(lo, t2, 0.0), axis=-1, keepdims=True)
    s_hi = jnp.sum(jnp.where(lo, 0.0, t2), axis=-1, keepdims=True)
    r = jnp.where(lo, lax.rsqrt(s_lo * (1.0 / HEAD_DIM) + EPS), lax.rsqrt(s_hi * (1.0 / HEAD_DIM) + EPS))
    return t * r * gain2


def _store_row_tiles(ref, val):
    rows = val.shape[0]
    for kk in range(val.shape[1] // LANES):
        ref[pl.ds(kk, rows, stride=SUBLANES), :] = val[:, kk * LANES:(kk + 1) * LANES]


def _wprep_kernel(w1_ref, w2_ref, perm_ref, w1o_ref, w2o_ref):
    perm = perm_ref[...]
    for c in range(w1_ref.shape[2] // MXU_COLS):
        cols = pl.ds(c * MXU_COLS, MXU_COLS)
        t = w1_ref[0, :, cols].astype(jnp.bfloat16)
        w1o_ref[0, :, cols] = jnp.dot(t, perm, preferred_element_type=jnp.float32).astype(jnp.bfloat16)
    w2o_ref[0] = w2_ref[0].astype(jnp.bfloat16)


def _wprep(w1, w2):
    n_e, d_model, two_ff = w1.shape
    d_ff, d_out = w2.shape[1], w2.shape[2]
    assert d_model == d_ff, "one row grid serves both weight tensors"
    j = np.arange(MXU_COLS)
    src = np.where(j < LANES, 2 * j, 2 * (j - LANES) + 1)
    perm = jnp.asarray((np.arange(MXU_COLS)[:, None] == src[None, :]).astype(np.float32), jnp.bfloat16)
    return pl.pallas_call(
        _wprep_kernel,
        grid=(n_e, d_model // PREP_ROWS),
        in_specs=[pl.BlockSpec((1, PREP_ROWS, two_ff), lambda e, r: (e, r, 0)),
                  pl.BlockSpec((1, PREP_ROWS, d_out), lambda e, r: (e, r, 0)),
                  pl.BlockSpec((MXU_COLS, MXU_COLS), lambda e, r: (0, 0))],
        out_specs=[pl.BlockSpec((1, PREP_ROWS, two_ff), lambda e, r: (e, r, 0)),
                   pl.BlockSpec((1, PREP_ROWS, d_out), lambda e, r: (e, r, 0))],
        out_shape=[jax.ShapeDtypeStruct(w1.shape, jnp.bfloat16), jax.ShapeDtypeStruct(w2.shape, jnp.bfloat16)],
        compiler_params=pltpu.CompilerParams(dimension_semantics=("arbitrary", "arbitrary")),
        name="wprep",
    )(w1, w2, perm)


def _mixer_kernel(tiles_per_chunk,
                  sinks_ref, relb_ref,
                  x_ref, gmix_ref, win_ref, bin_ref, convw_ref, gq_ref, gk_ref, bucket_ref,
                  gc_ref, ga_ref, wout_ref, bout_ref, gffn_ref, wrt_ref, br_ref, tri_ref,
                  x1_ref, h2_ref, ids_ref, gates_ref, rank_ref, counts_ref,
                  kd_scr, vd_scr, cu_scr, bias_scr, qm_scr, cnt_scr):
    rows = x_ref.shape[1]
    d_model = x_ref.shape[2]
    conv_w = d_model // 2
    nblk = rows // ATT_BLOCK
    b = pl.program_id(0)
    s = pl.program_id(1)
    tile = b * pl.num_programs(1) + s

    @pl.when((b == 0) & (s == 0))
    def _build_bias():
        bk = bucket_ref[...]
        col = lax.broadcasted_iota(jnp.int32, bk.shape, 1)
        accs = [jnp.full(bk.shape, MASK_VALUE, jnp.float32) for _ in range(N_Q_HEADS)]
        for bb in range(N_BUCKETS):
            hit = bk == bb
            for h in range(N_Q_HEADS):
                accs[h] = jnp.where(hit, relb_ref[bb, h], accs[h])
        for h in range(N_Q_HEADS):
            g, j = divmod(h, Q_PER_KV)
            bias_scr[0, g, pl.ds(j * ATT_BLOCK, ATT_BLOCK), :] = accs[h]
            bias_scr[1, g, pl.ds(j * ATT_BLOCK, ATT_BLOCK), :] = jnp.where(col < ATT_BLOCK, MASK_VALUE, accs[h])

    @pl.when(s == 0)
    def _reset_state():
        kd_scr[:, pl.ds(0, ATT_BLOCK), :] = jnp.zeros((N_KV_HEADS, ATT_BLOCK, LANES), kd_scr.dtype)
        vd_scr[:, pl.ds(0, ATT_BLOCK), :] = jnp.zeros((N_KV_HEADS, ATT_BLOCK, LANES), vd_scr.dtype)
        cu_scr[...] = jnp.zeros(cu_scr.shape, cu_scr.dtype)

    @pl.when(lax.rem(tile, tiles_per_chunk) == 0)
    def _reset_counts():
        cnt_scr[...] = jnp.zeros(cnt_scr.shape, cnt_scr.dtype)

    x = x_ref[0]
    h = _rms(x, gmix_ref[...]).astype(jnp.bfloat16)
    proj = jnp.dot(h, win_ref[...], preferred_element_type=jnp.float32) + bin_ref[...]
    b_gate = proj[:, 0:conv_w]
    c_gate = proj[:, conv_w:2 * conv_w]
    u = proj[:, 2 * conv_w:3 * conv_w]
    q = proj[:, 3 * conv_w:4 * conv_w]
    k = proj[:, 4 * conv_w:4 * conv_w + LANES]
    v = proj[:, 4 * conv_w + LANES:4 * conv_w + 2 * LANES]

    cu = c_gate * u
    row = lax.broadcasted_iota(jnp.int32, cu.shape, 0)
    prev2 = cu_scr[SUBLANES - 2:SUBLANES - 1, :]
    prev1 = cu_scr[SUBLANES - 1:SUBLANES, :]
    r1 = jnp.where(row == 0, prev1, pltpu.roll(cu, 1, 0))
    r2 = jnp.where(row == 0, prev2, jnp.where(row == 1, prev1, pltpu.roll(cu, 2, 0)))
    y_conv = b_gate * (convw_ref[2:3, :] * cu + convw_ref[1:2, :] * r1 + convw_ref[0:1, :] * r2)
    cu_scr[...] = cu[rows - SUBLANES:rows, :]

    lane = lax.broadcasted_iota(jnp.int32, (rows, LANES), 1)
    lo = lane < HEAD_DIM
    kn = _half_head_norm(k, gk_ref[...], lo)
    krot = pltpu.roll(kn, HEAD_DIM, 1)
    vrot = pltpu.roll(v, HEAD_DIM, 1)
    kd_scr[0, pl.ds(ATT_BLOCK, rows), :] = jnp.where(lo, kn, krot).astype(kd_scr.dtype)
    kd_scr[1, pl.ds(ATT_BLOCK, rows), :] = jnp.where(lo, krot, kn).astype(kd_scr.dtype)
    vd_scr[0, pl.ds(ATT_BLOCK, rows), :] = jnp.where(lo, v, vrot).astype(vd_scr.dtype)
    vd_scr[1, pl.ds(ATT_BLOCK, rows), :] = jnp.where(lo, vrot, v).astype(vd_scr.dtype)
    for c in range(conv_w // LANES):
        qn = _half_head_norm(q[:, c * LANES:(c + 1) * LANES], gq_ref[...], lo)
        qm_scr[2 * c] = jnp.where(lo, qn, 0.0).astype(qm_scr.dtype)
        qm_scr[2 * c + 1] = jnp.where(lo, 0.0, qn).astype(qm_scr.dtype)

    lo_b = lax.broadcasted_iota(jnp.int32, (ATT_BLOCK, LANES), 1) < HEAD_DIM
    ya_blocks = []
    for n in range(nblk):
        first = jnp.where(s == 0, 1, 0) if n == 0 else 0
        tiles = []
        for g in range(N_KV_HEADS):
            qs = jnp.concatenate([qm_scr[Q_PER_KV * g + j, pl.ds(n * ATT_BLOCK, ATT_BLOCK), :]
                                  for j in range(Q_PER_KV)], axis=0)
            kw = kd_scr[g, pl.ds(n * ATT_BLOCK, 2 * ATT_BLOCK), :]
            vw = vd_scr[g, pl.ds(n * ATT_BLOCK, 2 * ATT_BLOCK), :]
            sc = lax.dot_general(qs, kw, (((1,), (1,)), ((), ())), preferred_element_type=jnp.float32)
            logits = sc + bias_scr[first, g]
            ps, inv = [], []
            for j in range(Q_PER_KV):
                lj = logits[j * ATT_BLOCK:(j + 1) * ATT_BLOCK]
                sink = sinks_ref[Q_PER_KV * g + j]
                m = jnp.maximum(jnp.max(lj, axis=-1, keepdims=True), sink)
                p = jnp.exp(lj - m)
                den = jnp.sum(p, axis=-1, keepdims=True) + jnp.exp(sink - m)
                ps.append(p.astype(jnp.bfloat16))
                inv.append(1.0 / den)
            o = jnp.dot(jnp.concatenate(ps, axis=0), vw, preferred_element_type=jnp.float32)
            on = [o[j * ATT_BLOCK:(j + 1) * ATT_BLOCK] * inv[j] for j in range(Q_PER_KV)]
            tiles.append(jnp.where(lo_b, on[0], on[1]))
            tiles.append(jnp.where(lo_b, on[2], on[3]))
        ya_blocks.append(jnp.concatenate(tiles, axis=-1))
    y_attn = jnp.concatenate(ya_blocks, axis=0)

    for g in range(N_KV_HEADS):
        kd_scr[g, pl.ds(0, ATT_BLOCK), :] = kd_scr[g, pl.ds(rows, ATT_BLOCK), :]
        vd_scr[g, pl.ds(0, ATT_BLOCK), :] = vd_scr[g, pl.ds(rows, ATT_BLOCK), :]

    mix = jnp.concatenate([_rms(y_conv, gc_ref[...]), _rms(y_attn, ga_ref[...])], axis=-1)
    x1 = x + jnp.dot(mix.astype(jnp.bfloat16), wout_ref[...], preferred_element_type=jnp.float32) + bout_ref[...]
    _store_row_tiles(x1_ref, x1)
    h2 = _rms(x1, gffn_ref[...])
    _store_row_tiles(h2_ref, h2)
    lt = lax.dot_general(wrt_ref[...], h2, (((1,), (1,)), ((), ())), precision=lax.Precision.HIGHEST,
                         preferred_element_type=jnp.float32) + br_ref[...]
    eid = lax.broadcasted_iota(jnp.int32, lt.shape, 0)
    vals, idxs, hits = [], [], []
    for _ in range(TOP_K):
        m = jnp.max(lt, axis=0, keepdims=True)
        idx = jnp.min(jnp.where(lt == m, eid, N_EXPERTS), axis=0, keepdims=True)
        hit = eid == idx
        vals.append(m)
        idxs.append(idx)
        hits.append(hit)
        lt = jnp.where(hit, -jnp.inf, lt)
    es = [jnp.exp(vv - vals[0]) for vv in vals]
    tot = es[0] + es[1] + es[2] + es[3]
    ids_ref[0] = jnp.concatenate(idxs, axis=0)
    gates_ref[0] = jnp.concatenate([e / tot for e in es], axis=0)

    member = jnp.where(hits[0] | hits[1] | hits[2] | hits[3], 1.0, 0.0)
    before = jnp.dot(member.astype(jnp.bfloat16), tri_ref[...], preferred_element_type=jnp.float32)
    running = cnt_scr[...][:, 0:1]
    rank_e = before + running
    rank_ref[0] = jnp.concatenate([jnp.sum(jnp.where(hh, rank_e, 0.0), axis=0, keepdims=True) for hh in hits],
                                  axis=0).astype(jnp.int32)
    new_counts = cnt_scr[...] + jnp.sum(member, axis=1, keepdims=True)
    cnt_scr[...] = new_counts
    counts_ref[0] = new_counts.astype(jnp.int32)


def _mixer(x, p, bucket, chunk):
    bsz, seq, d_model = x.shape
    rows = min(MIX_ROWS, seq)
    ns = seq // rows
    tiles_per_chunk = chunk // rows
    nchunks = bsz * seq // chunk
    conv_w = d_model // 2
    in_w = p["win"].shape[1]
    tri = jnp.asarray(np.triu(np.ones((rows, rows), np.float32), 1), jnp.bfloat16)
    full = lambda shape: pl.BlockSpec(shape, lambda b, s: (0,) * len(shape))
    smem = pl.BlockSpec(memory_space=pltpu.SMEM)
    tok_spec = pl.BlockSpec((1, rows, d_model), lambda b, s: (b, s, 0))
    tile_spec = pl.BlockSpec((rows * SUBLANES, LANES), lambda b, s: (b * ns + s, 0))
    sel_spec = pl.BlockSpec((1, TOP_K, rows), lambda b, s: (b * ns + s, 0, 0))
    cnt_spec = pl.BlockSpec((1, N_EXPERTS, LANES), lambda b, s: ((b * ns + s) // tiles_per_chunk, 0, 0))
    t_total = bsz * seq
    sel_shape = (bsz * ns, TOP_K, rows)
    return pl.pallas_call(
        functools.partial(_mixer_kernel, tiles_per_chunk),
        grid=(bsz, ns),
        in_specs=[smem, smem, tok_spec, full((1, d_model)), full((d_model, in_w)), full((1, in_w)),
                  full((CONV_K, conv_w)), full((1, LANES)), full((1, LANES)), full((ATT_BLOCK, 2 * ATT_BLOCK)),
                  full((1, conv_w)), full((1, conv_w)), full((d_model, d_model)), full((1, d_model)),
                  full((1, d_model)), full((N_EXPERTS, d_model)), full((N_EXPERTS, 1)), full((rows, rows))],
        out_specs=[tile_spec, tile_spec, sel_spec, sel_spec, sel_spec, cnt_spec],
        out_shape=[jax.ShapeDtypeStruct((t_total * SUBLANES, LANES), jnp.float32),
                   jax.ShapeDtypeStruct((t_total * SUBLANES, LANES), jnp.float32),
                   jax.ShapeDtypeStruct(sel_shape, jnp.int32),
                   jax.ShapeDtypeStruct(sel_shape, jnp.float32),
                   jax.ShapeDtypeStruct(sel_shape, jnp.int32),
                   jax.ShapeDtypeStruct((nchunks, N_EXPERTS, LANES), jnp.int32)],
        scratch_shapes=[pltpu.VMEM((N_KV_HEADS, ATT_BLOCK + rows, LANES), jnp.bfloat16),
                        pltpu.VMEM((N_KV_HEADS, ATT_BLOCK + rows, LANES), jnp.bfloat16),
                        pltpu.VMEM((SUBLANES, conv_w), jnp.float32),
                        pltpu.VMEM((2, N_KV_HEADS, Q_PER_KV * ATT_BLOCK, 2 * ATT_BLOCK), jnp.float32),
                        pltpu.VMEM((N_Q_HEADS, rows, LANES), jnp.bfloat16),
                        pltpu.VMEM((N_EXPERTS, LANES), jnp.float32)],
        compiler_params=pltpu.CompilerParams(dimension_semantics=("arbitrary", "arbitrary"),
                                             vmem_limit_bytes=VMEM_LIMIT_BYTES),
        name="mixer",
    )(p["sinks"], p["rel_bias"], x, p["gmix"], p["win"], p["bin"], p["convw"], p["gq2"], p["gk2"], bucket,
      p["gc"], p["ga"], p["wout"], p["bout"], p["gffn"], p["wrt"], p["br"], tri)


def _moe_kernel(nbc_ref, bexp_ref, bfirst_ref, bslot_ref, bnext_ref, firstexp_ref,
                x1_hbm, h_hbm, dest_hbm, gate_hbm, tokfill_hbm, gatefill_hbm, w1_hbm, w2_hbm, b1_ref, b2_ref,
                out_hbm,
                acc, hbuf, w1buf, w2buf, gbuf0, gbuf1, rbuf0, rbuf1, dest_s, gate_s, tok_s, gl_s,
                io_sem, w_sem, l_sem):
    chunk = hbuf.shape[0] - 1
    pad_block = bexp_ref.shape[1] - 1
    d_model = w1buf.shape[1]
    d_ff = w2buf.shape[1]
    nk = d_model // LANES
    c = pl.program_id(0)

    def weight_copies(e, slot):
        return (pltpu.make_async_copy(w1_hbm.at[e], w1buf.at[slot], w_sem.at[0, slot]),
                pltpu.make_async_copy(w2_hbm.at[e], w2buf.at[slot], w_sem.at[1, slot]))

    def chunk_in_copies():
        rows = pl.ds(c * chunk, chunk)
        return (pltpu.make_async_copy(x1_hbm.at[rows], acc.at[pl.ds(0, chunk)], io_sem.at[0]),
                pltpu.make_async_copy(h_hbm.at[rows], hbuf.at[pl.ds(0, chunk)], io_sem.at[1]))

    def list_copies():
        return (pltpu.make_async_copy(dest_hbm.at[c], dest_s, l_sem.at[0]),
                pltpu.make_async_copy(gate_hbm.at[c], gate_s, l_sem.at[1]),
                pltpu.make_async_copy(tokfill_hbm, tok_s, l_sem.at[2]),
                pltpu.make_async_copy(gatefill_hbm, gl_s, l_sem.at[3]))

    for cp in list_copies() + chunk_in_copies():
        cp.start()

    @pl.when(c == 0)
    def _first_weights():
        for cp in weight_copies(firstexp_ref[0], 0):
            cp.start()

    for cp in list_copies():
        cp.wait()
    for kk in range(TOP_K):
        def invert_body(i, carry, kk=kk):
            for jj in range(INVERT_UNROLL):
                t = i * INVERT_UNROLL + jj
                a = kk * chunk + t
                d = dest_s[0, a]
                tok_s[0, d] = t
                gl_s[0, d] = gate_s[0, a]
            return carry

        lax.fori_loop(0, chunk // INVERT_UNROLL, invert_body, 0)

    acc[chunk] = jnp.zeros((SUBLANES, LANES), jnp.float32)
    hbuf[chunk] = jnp.zeros((SUBLANES, LANES), jnp.float32)
    rbuf1[...] = jnp.zeros(rbuf1.shape, rbuf1.dtype)
    for cp in chunk_in_copies():
        cp.wait()

    def weights_step(g):
        @pl.when(bfirst_ref[c, g] == 1)
        def _():
            slot = bslot_ref[c, g]
            for cp in weight_copies(bexp_ref[c, g], slot):
                cp.wait()
            nxt = bnext_ref[c, g]

            @pl.when(nxt >= 0)
            def _():
                for cp in weight_copies(nxt, 1 - slot):
                    cp.start()

    def gather_rows(g, gbuf, j_lo, j_hi):
        s0 = g * MOE_ROWS
        for j in range(j_lo, j_hi):
            gbuf[pl.ds(j * SUBLANES, SUBLANES), :] = hbuf[tok_s[0, s0 + j]]

    def scatter_rows(g, rbuf, j_lo, j_hi):
        s0 = g * MOE_ROWS
        for j0 in range(j_lo, j_hi, SCATTER_UNROLL):
            toks, vals = [], []
            for j in range(j0, j0 + SCATTER_UNROLL):
                t = tok_s[0, s0 + j]
                toks.append(t)
                vals.append(acc[t] + gl_s[0, s0 + j] * rbuf[pl.ds(j * SUBLANES, SUBLANES), :])
            for t, val in zip(toks, vals):
                acc[t] = val

    n_hid = 2 * d_ff // MXU_COLS
    n_out = d_model // MXU_COLS
    rows_hid = MOE_ROWS // 2 // n_hid
    rows_out = MOE_ROWS // 2 // n_out

    def stage(g_gather, gbuf_in, g_compute, gbuf, rbuf, g_scatter, rbuf_out):
        e = bexp_ref[c, g_compute]
        slot = bslot_ref[c, g_compute]
        xs = [gbuf[pl.ds(kk, MOE_ROWS, stride=SUBLANES), :] for kk in range(nk)]
        xb = jnp.concatenate(xs, axis=-1).astype(jnp.bfloat16)
        acts = []
        for cg in range(n_hid):
            cols = pl.ds(cg * MXU_COLS, MXU_COLS)
            hid = jnp.dot(xb, w1buf[slot, :, cols], preferred_element_type=jnp.float32) + b1_ref[e, :, cols]
            glu = jnp.minimum(hid[:, :LANES], SWIGLU_LIMIT)
            lin = jnp.clip(hid[:, LANES:], -SWIGLU_LIMIT, SWIGLU_LIMIT)
            acts.append(glu * jax.nn.sigmoid(SWIGLU_ALPHA * glu) * (lin + 1.0))
            gather_rows(g_gather, gbuf_in, cg * rows_hid, (cg + 1) * rows_hid)
            scatter_rows(g_scatter, rbuf_out, cg * rows_hid, (cg + 1) * rows_hid)
        act = jnp.concatenate(acts, axis=-1).astype(jnp.bfloat16)
        half = MOE_ROWS // 2
        for og in range(n_out):
            cols = pl.ds(og * MXU_COLS, MXU_COLS)
            y = jnp.dot(act, w2buf[slot, :, cols], preferred_element_type=jnp.float32) + b2_ref[e, :, cols]
            for kk in range(MXU_COLS // LANES):
                rbuf[pl.ds(og * (MXU_COLS // LANES) + kk, MOE_ROWS, stride=SUBLANES), :] = (
                    y[:, kk * LANES:(kk + 1) * LANES])
            gather_rows(g_gather, gbuf_in, half + og * rows_out, half + (og + 1) * rows_out)
            scatter_rows(g_scatter, rbuf_out, half + og * rows_out, half + (og + 1) * rows_out)

    gather_rows(0, gbuf0, 0, MOE_ROWS)

    def pair_body(p, carry):
        g0 = 2 * p
        weights_step(g0)
        stage(g0 + 1, gbuf1, g0, gbuf0, rbuf0, jnp.where(p == 0, pad_block, g0 - 1), rbuf1)
        weights_step(g0 + 1)
        stage(g0 + 2, gbuf0, g0 + 1, gbuf1, rbuf1, g0, rbuf0)
        return carry

    npairs = (nbc_ref[c] + 1) // 2
    lax.fori_loop(0, npairs, pair_body, 0)
    scatter_rows(2 * npairs - 1, rbuf1, 0, MOE_ROWS)

    out_cp = pltpu.make_async_copy(acc.at[pl.ds(0, chunk)], out_hbm.at[pl.ds(c * chunk, chunk)], io_sem.at[0])
    out_cp.start()
    out_cp.wait()


def _block_tables(nblk, table_blocks):
    nchunks = nblk.shape[0]
    bend = jnp.cumsum(nblk, axis=1)
    nbc = bend[:, -1]
    g = jnp.arange(table_blocks, dtype=jnp.int32)
    nonempty = nblk > 0
    last_exp = jnp.max(jnp.where(nonempty, jnp.arange(N_EXPERTS, dtype=jnp.int32)[None, :], 0), axis=1)
    bexp = jnp.sum((g[None, :, None] >= bend[:, None, :]).astype(jnp.int32), axis=-1)
    bexp = jnp.where(g[None, :] >= nbc[:, None], last_exp[:, None], bexp)
    bstart = bend - nblk
    bfirst = ((g[None, :] == jnp.take_along_axis(bstart, bexp, axis=1)) & (g[None, :] < nbc[:, None])).astype(jnp.int32)
    flat = nonempty.reshape(-1)
    pair_slot = ((jnp.cumsum(flat.astype(jnp.int32)) - 1) % 2).reshape(nchunks, N_EXPERTS)
    n_pairs = flat.shape[0]
    idx = jnp.where(flat, jnp.arange(n_pairs, dtype=jnp.int32), n_pairs)
    later = jnp.concatenate([lax.cummin(idx[::-1])[::-1][1:], jnp.full((1,), n_pairs, jnp.int32)])
    pair_next = jnp.where(later < n_pairs, later % N_EXPERTS, -1).reshape(nchunks, N_EXPERTS)
    bslot = jnp.take_along_axis(pair_slot, bexp, axis=1)
    bnext = jnp.take_along_axis(pair_next, bexp, axis=1)
    first_exp = (jnp.min(idx) % N_EXPERTS).reshape(1)
    return nbc.astype(jnp.int32), bexp, bfirst, bslot.astype(jnp.int32), bnext.astype(jnp.int32), first_exp


def _moe(x1r, hr, dest, gates, tables, w1, w2, b1, b2, chunk, table_blocks):
    t_total = x1r.shape[0]
    nchunks = t_total // chunk
    d_model = w1.shape[1]
    d_ff = w2.shape[1]
    n_assign = TOP_K * chunk
    n_slots = table_blocks * MOE_ROWS
    tokfill = jnp.full((1, n_slots), chunk, jnp.int32)
    gatefill = jnp.zeros((1, n_slots), jnp.float32)
    anyspace = pl.BlockSpec(memory_space=pl.ANY)
    grid_spec = pltpu.PrefetchScalarGridSpec(
        num_scalar_prefetch=6,
        grid=(nchunks,),
        in_specs=[anyspace] * 8 + [pl.BlockSpec((N_EXPERTS, 1, 2 * d_ff), lambda c, *_: (0, 0, 0)),
                                   pl.BlockSpec((N_EXPERTS, 1, d_model), lambda c, *_: (0, 0, 0))],
        out_specs=anyspace,
        scratch_shapes=[pltpu.VMEM((chunk + 1, SUBLANES, LANES), jnp.float32),
                        pltpu.VMEM((chunk + 1, SUBLANES, LANES), jnp.float32),
                        pltpu.VMEM((2, d_model, 2 * d_ff), jnp.bfloat16),
                        pltpu.VMEM((2, d_ff, d_model), jnp.bfloat16),
                        pltpu.VMEM((MOE_ROWS * SUBLANES, LANES), jnp.float32),
                        pltpu.VMEM((MOE_ROWS * SUBLANES, LANES), jnp.float32),
                        pltpu.VMEM((MOE_ROWS * SUBLANES, LANES), jnp.float32),
                        pltpu.VMEM((MOE_ROWS * SUBLANES, LANES), jnp.float32),
                        pltpu.SMEM((1, n_assign), jnp.int32),
                        pltpu.SMEM((1, n_assign), jnp.float32),
                        pltpu.SMEM((1, n_slots), jnp.int32),
                        pltpu.SMEM((1, n_slots), jnp.float32),
                        pltpu.SemaphoreType.DMA((2,)),
                        pltpu.SemaphoreType.DMA((2, 2)),
                        pltpu.SemaphoreType.DMA((4,))],
    )
    return pl.pallas_call(
        _moe_kernel,
        grid_spec=grid_spec,
        out_shape=jax.ShapeDtypeStruct(x1r.shape, jnp.float32),
        compiler_params=pltpu.CompilerParams(dimension_semantics=("arbitrary",),
                                             vmem_limit_bytes=VMEM_LIMIT_BYTES),
        name="moe",
    )(*tables, x1r, hr, dest, gates, tokfill, gatefill, w1, w2, b1, b2)


def _t5_bucket_table():
    i = np.arange(ATT_BLOCK)[:, None]
    j = np.arange(2 * ATT_BLOCK)[None, :]
    rel = i + ATT_BLOCK - j
    max_exact = N_BUCKETS // 2
    nf = np.maximum(rel, 1).astype(np.float32)
    large = max_exact + (np.log(nf / max_exact) / np.float32(math.log(MAX_DISTANCE / max_exact))
                         * (N_BUCKETS - max_exact)).astype(np.int32)
    large = np.minimum(large, N_BUCKETS - 1)
    bucket = np.where(rel < max_exact, rel, large)
    return np.where((rel >= 0) & (rel < WINDOW), bucket, -1).astype(np.int32)


def _chunk_major(a, nchunks):
    tiles, _, rows = a.shape
    a = a.reshape(nchunks, tiles // nchunks, TOP_K, rows)
    return jnp.transpose(a, (0, 2, 1, 3)).reshape(nchunks, 1, -1)


def kernel(x, norm_mix, w_in, b_in, conv_w, q_norm, k_norm, sinks, rel_bias, out_norm_conv, out_norm_attn,
           w_out, b_out, norm_ffn, w_router, b_router, w1, b1, w2, b2):
    bsz, seq, d_model = x.shape
    depth = w_in.shape[0]
    bf16 = jnp.bfloat16
    bucket = jnp.asarray(_t5_bucket_table())
    t_total = bsz * seq
    chunk = min(MOE_CHUNK, t_total)
    nchunks = t_total // chunk
    table_blocks = chunk * TOP_K // MOE_ROWS + N_EXPERTS + 2
    for l in range(depth):
        d_ff = w2.shape[2]
        params = dict(
            sinks=sinks[l], rel_bias=rel_bias, gmix=norm_mix[l][None], win=w_in[l].astype(bf16), bin=b_in[l][None],
            convw=conv_w[l], gq2=jnp.tile(q_norm[l], 2)[None] * (HEAD_DIM ** -0.5), gk2=jnp.tile(k_norm[l], 2)[None],
            gc=out_norm_conv[l][None], ga=out_norm_attn[l][None], wout=w_out[l].astype(bf16), bout=b_out[l][None],
            gffn=norm_ffn[l][None], wrt=w_router[l].T, br=b_router[l][:, None])
        x1r, hr, ids_t, gates_t, rank_t, counts_t = _mixer(x, params, bucket, chunk)
        counts = counts_t[:, :, 0]
        padded = (counts + MOE_ROWS - 1) // MOE_ROWS * MOE_ROWS
        pad_starts = jnp.cumsum(padded, axis=1) - padded
        tables = _block_tables(padded // MOE_ROWS, table_blocks)
        ids_c = _chunk_major(ids_t, nchunks)
        dest = _chunk_major(rank_t, nchunks)
        for e in range(N_EXPERTS):
            dest = dest + jnp.where(ids_c == e, pad_starts[:, e:e + 1, None], 0)
        w1p, w2p = _wprep(w1[l], w2[l])
        b1p = jnp.transpose(b1[l].reshape(N_EXPERTS, 2 * d_ff // MXU_COLS, LANES, 2), (0, 1, 3, 2))
        b1p = b1p.reshape(N_EXPERTS, 1, 2 * d_ff)
        tile3 = (t_total, SUBLANES, LANES)
        outr = _moe(x1r.reshape(tile3), hr.reshape(tile3), dest, _chunk_major(gates_t, nchunks), tables, w1p, w2p,
                    b1p, b2[l][:, None, :], chunk, table_blocks)
        x = outr.reshape(bsz, seq, d_model)
    return x
```

```python
import functools
import math

import jax
import jax.numpy as jnp
import numpy as np
from jax import lax
from jax.experimental import pallas as pl
from jax.experimental.pallas import tpu as pltpu

HEAD_DIM = 64
N_Q_HEADS = 8
N_KV_HEADS = 2
Q_PER_KV = N_Q_HEADS // N_KV_HEADS
CONV_K = 3
WINDOW = 128
ATT_BLOCK = 128
N_BUCKETS = 32
MAX_DISTANCE = 128
N_EXPERTS = 32
TOP_K = 4
TOP_K_SHIFT = 2
SWIGLU_LIMIT = 7.0
SWIGLU_ALPHA = 1.702
EPS = 1e-5
MASK_VALUE = -1e30

LANES = 128
SUBLANES = 8
MXU_COLS = 256
VMEM_LIMIT_BYTES = 56 * 1024 * 1024

MIX_ROWS = 512
MOE_CHUNK = 4096
MOE_ROWS = 128
OUT_ROWS = 128
PREP_ROWS = 256
INVERT_UNROLL = 16
SCATTER_UNROLL = 4


def _rms(x, g):
    return x * lax.rsqrt(jnp.mean(x * x, axis=-1, keepdims=True) + EPS) * g


def _half_head_norm(t, gain2, lo):
    t2 = t * t
    s_lo = jnp.sum(jnp.where(lo, t2, 0.0), axis=-1, keepdims=True)
    s_hi = jnp.sum(jnp.where(lo, 0.0, t2), axis=-1, keepdims=True)
    r = jnp.where(lo, lax.rsqrt(s_lo * (1.0 / HEAD_DIM) + EPS), lax.rsqrt(s_hi * (1.0 / HEAD_DIM) + EPS))
    return t * r * gain2


def _store_row_tiles(ref, val):
    rows = val.shape[0]
    for kk in range(val.shape[1] // LANES):
        ref[pl.ds(kk, rows, stride=SUBLANES), :] = val[:, kk * LANES:(kk + 1) * LANES]


def _wprep_kernel(w1_ref, w2_ref, perm_ref, w1o_ref, w2o_ref):
    perm = perm_ref[...]
    for c in range(w1_ref.shape[2] // MXU_COLS):
        cols = pl.ds(c * MXU_COLS, MXU_COLS)
        t = w1_ref[0, :, cols].astype(jnp.bfloat16)
        w1o_ref[0, :, cols] = jnp.dot(t, perm, preferred_element_type=jnp.float32).astype(jnp.bfloat16)
    w2o_ref[0] = w2_ref[0].astype(jnp.bfloat16)


def _wprep(w1, w2):
    n_e, d_model, two_ff = w1.shape
    d_ff, d_out = w2.shape[1], w2.shape[2]
    assert d_model == d_ff, "one row grid serves both weight tensors"
    j = np.arange(MXU_COLS)
    src = np.where(j < LANES, 2 * j, 2 * (j - LANES) + 1)
    perm = jnp.asarray((np.arange(MXU_COLS)[:, None] == src[None, :]).astype(np.float32), jnp.bfloat16)
    return pl.pallas_call(
        _wprep_kernel,
        grid=(n_e, d_model // PREP_ROWS),
        in_specs=[pl.BlockSpec((1, PREP_ROWS, two_ff), lambda e, r: (e, r, 0)),
                  pl.BlockSpec((1, PREP_ROWS, d_out), lambda e, r: (e, r, 0)),
                  pl.BlockSpec((MXU_COLS, MXU_COLS), lambda e, r: (0, 0))],
        out_specs=[pl.BlockSpec((1, PREP_ROWS, two_ff), lambda e, r: (e, r, 0)),
                   pl.BlockSpec((1, PREP_ROWS, d_out), lambda e, r: (e, r, 0))],
        out_shape=[jax.ShapeDtypeStruct(w1.shape, jnp.bfloat16), jax.ShapeDtypeStruct(w2.shape, jnp.bfloat16)],
        compiler_params=pltpu.CompilerParams(dimension_semantics=("arbitrary", "arbitrary")),
        name="wprep",
    )(w1, w2, perm)


def _mixer_kernel(tiles_per_chunk,
                  sinks_ref, relb_ref,
                  x_ref, gmix_ref, win_ref, bin_ref, convw_ref, gq_ref, gk_ref, bucket_ref,
                  gc_ref, ga_ref, wout_ref, bout_ref, gffn_ref, wrt_ref, br_ref, tri_ref,
                  x1_ref, h2_ref, ids_ref, gates_ref, rank_ref, counts_ref,
                  kd_scr, vd_scr, cu_scr, bias_scr, qm_scr, cnt_scr):
    rows = x_ref.shape[1]
    d_model = x_ref.shape[2]
    conv_w = d_model // 2
    nblk = rows // ATT_BLOCK
    b = pl.program_id(0)
    s = pl.program_id(1)
    tile = b * pl.num_programs(1) + s

    @pl.when((b == 0) & (s == 0))
    def _build_bias():
        bk = bucket_ref[...]
        col = lax.broadcasted_iota(jnp.int32, bk.shape, 1)
        accs = [jnp.full(bk.shape, MASK_VALUE, jnp.float32) for _ in range(N_Q_HEADS)]
        for bb in range(N_BUCKETS):
            hit = bk == bb
            for h in range(N_Q_HEADS):
                accs[h] = jnp.where(hit, relb_ref[bb, h], accs[h])
        for h in range(N_Q_HEADS):
            g, j = divmod(h, Q_PER_KV)
            bias_scr[0, g, pl.ds(j * ATT_BLOCK, ATT_BLOCK), :] = accs[h]
            bias_scr[1, g, pl.ds(j * ATT_BLOCK, ATT_BLOCK), :] = jnp.where(col < ATT_BLOCK, MASK_VALUE, accs[h])

    @pl.when(s == 0)
    def _reset_state():
        kd_scr[:, pl.ds(0, ATT_BLOCK), :] = jnp.zeros((N_KV_HEADS, ATT_BLOCK, LANES), kd_scr.dtype)
        vd_scr[:, pl.ds(0, ATT_BLOCK), :] = jnp.zeros((N_KV_HEADS, ATT_BLOCK, LANES), vd_scr.dtype)
        cu_scr[...] = jnp.zeros(cu_scr.shape, cu_scr.dtype)

    @pl.when(lax.rem(tile, tiles_per_chunk) == 0)
    def _reset_counts():
        cnt_scr[...] = jnp.zeros(cnt_scr.shape, cnt_scr.dtype)

    x = x_ref[0]
    h = _rms(x, gmix_ref[...]).astype(jnp.bfloat16)
    proj = jnp.dot(h, win_ref[...], preferred_element_type=jnp.float32) + bin_ref[...]
    b_gate = proj[:, 0:conv_w]
    c_gate = proj[:, conv_w:2 * conv_w]
    u = proj[:, 2 * conv_w:3 * conv_w]
    q = proj[:, 3 * conv_w:4 * conv_w]
    k = proj[:, 4 * conv_w:4 * conv_w + LANES]
    v = proj[:, 4 * conv_w + LANES:4 * conv_w + 2 * LANES]

    cu = c_gate * u
    row = lax.broadcasted_iota(jnp.int32, cu.shape, 0)
    prev2 = cu_scr[SUBLANES - 2:SUBLANES - 1, :]
    prev1 = cu_scr[SUBLANES - 1:SUBLANES, :]
    r1 = jnp.where(row == 0, prev1, pltpu.roll(cu, 1, 0))
    r2 = jnp.where(row == 0, prev2, jnp.where(row == 1, prev1, pltpu.roll(cu, 2, 0)))
    y_conv = b_gate * (convw_ref[2:3, :] * cu + convw_ref[1:2, :] * r1 + convw_ref[0:1, :] * r2)
    cu_scr[...] = cu[rows - SUBLANES:rows, :]

    lane = lax.broadcasted_iota(jnp.int32, (rows, LANES), 1)
    lo = lane < HEAD_DIM
    kn = _half_head_norm(k, gk_ref[...], lo)
    krot = pltpu.roll(kn, HEAD_DIM, 1)
    vrot = pltpu.roll(v, HEAD_DIM, 1)
    kd_scr[0, pl.ds(ATT_BLOCK, rows), :] = jnp.where(lo, kn, krot).astype(kd_scr.dtype)
    kd_scr[1, pl.ds(ATT_BLOCK, rows), :] = jnp.where(lo, krot, kn).astype(kd_scr.dtype)
    vd_scr[0, pl.ds(ATT_BLOCK, rows), :] = jnp.where(lo, v, vrot).astype(vd_scr.dtype)
    vd_scr[1, pl.ds(ATT_BLOCK, rows), :] = jnp.where(lo, vrot, v).astype(vd_scr.dtype)
    for c in range(conv_w // LANES):
        qn = _half_head_norm(q[:, c * LANES:(c + 1) * LANES], gq_ref[...], lo)
        qm_scr[2 * c] = jnp.where(lo, qn, 0.0).astype(qm_scr.dtype)
        qm_scr[2 * c + 1] = jnp.where(lo, 0.0, qn).astype(qm_scr.dtype)

    lo_b = lax.broadcasted_iota(jnp.int32, (ATT_BLOCK, LANES), 1) < HEAD_DIM
    ya_blocks = []
    for n in range(nblk):
        first = jnp.where(s == 0, 1, 0) if n == 0 else 0
        tiles = []
        for g in range(N_KV_HEADS):
            qs = jnp.concatenate([qm_scr[Q_PER_KV * g + j, pl.ds(n * ATT_BLOCK, ATT_BLOCK), :]
                                  for j in range(Q_PER_KV)], axis=0)
            kw = kd_scr[g, pl.ds(n * ATT_BLOCK, 2 * ATT_BLOCK), :]
            vw = vd_scr[g, pl.ds(n * ATT_BLOCK, 2 * ATT_BLOCK), :]
            sc = lax.dot_general(qs, kw, (((1,), (1,)), ((), ())), preferred_element_type=jnp.float32)
            logits = sc + bias_scr[first, g]
            ps, inv = [], []
            for j in range(Q_PER_KV):
                lj = logits[j * ATT_BLOCK:(j + 1) * ATT_BLOCK]
                sink = sinks_ref[Q_PER_KV * g + j]
                m = jnp.maximum(jnp.max(lj, axis=-1, keepdims=True), sink)
                p = jnp.exp(lj - m)
                den = jnp.sum(p, axis=-1, keepdims=True) + jnp.exp(sink - m)
                ps.append(p.astype(jnp.bfloat16))
                inv.append(1.0 / den)
            o = jnp.dot(jnp.concatenate(ps, axis=0), vw, preferred_element_type=jnp.float32)
            on = [o[j * ATT_BLOCK:(j + 1) * ATT_BLOCK] * inv[j] for j in range(Q_PER_KV)]
            tiles.append(jnp.where(lo_b, on[0], on[1]))
            tiles.append(jnp.where(lo_b, on[2], on[3]))
        ya_blocks.append(jnp.concatenate(tiles, axis=-1))
    y_attn = jnp.concatenate(ya_blocks, axis=0)

    for g in range(N_KV_HEADS):
        kd_scr[g, pl.ds(0, ATT_BLOCK), :] = kd_scr[g, pl.ds(rows, ATT_BLOCK), :]
        vd_scr[g, pl.ds(0, ATT_BLOCK), :] = vd_scr[g, pl.ds(rows, ATT_BLOCK), :]

    mix = jnp.concatenate([_rms(y_conv, gc_ref[...]), _rms(y_attn, ga_ref[...])], axis=-1)
    x1 = x + jnp.dot(mix.astype(jnp.bfloat16), wout_ref[...], preferred_element_type=jnp.float32) + bout_ref[...]
    _store_row_tiles(x1_ref, x1)
    h2 = _rms(x1, gffn_ref[...])
    _store_row_tiles(h2_ref, h2)
    h_hi = h2.astype(jnp.bfloat16)
    h_lo = (h2 - h_hi.astype(jnp.float32)).astype(jnp.bfloat16)
    wr = wrt_ref[...]
    w_hi = wr.astype(jnp.bfloat16)
    w_lo = (wr - w_hi.astype(jnp.float32)).astype(jnp.bfloat16)
    nt = (((1,), (1,)), ((), ()))
    lt = (lax.dot_general(w_hi, h_hi, nt, preferred_element_type=jnp.float32)
          + lax.dot_general(w_hi, h_lo, nt, preferred_element_type=jnp.float32)
          + lax.dot_general(w_lo, h_hi, nt, preferred_element_type=jnp.float32)) + br_ref[...]
    eid = lax.broadcasted_iota(jnp.int32, lt.shape, 0)
    vals, idxs, hits = [], [], []
    for _ in range(TOP_K):
        m = jnp.max(lt, axis=0, keepdims=True)
        idx = jnp.min(jnp.where(lt == m, eid, N_EXPERTS), axis=0, keepdims=True)
        hit = eid == idx
        vals.append(m)
        idxs.append(idx)
        hits.append(hit)
        lt = jnp.where(hit, -jnp.inf, lt)
    es = [jnp.exp(vv - vals[0]) for vv in vals]
    tot = es[0] + es[1] + es[2] + es[3]
    ids_ref[0] = jnp.concatenate(idxs, axis=0)
    gates_ref[0] = jnp.concatenate([e / tot for e in es], axis=0)

    member = jnp.where(hits[0] | hits[1] | hits[2] | hits[3], 1.0, 0.0)
    before = jnp.dot(member.astype(jnp.bfloat16), tri_ref[...], preferred_element_type=jnp.float32)
    running = cnt_scr[...][:, 0:1]
    rank_e = before + running
    rank_ref[0] = jnp.concatenate([jnp.sum(jnp.where(hh, rank_e, 0.0), axis=0, keepdims=True) for hh in hits],
                                  axis=0).astype(jnp.int32)
    new_counts = cnt_scr[...] + jnp.sum(member, axis=1, keepdims=True)
    cnt_scr[...] = new_counts
    counts_ref[0] = new_counts.astype(jnp.int32)


def _mixer(x, p, bucket, chunk):
    bsz, seq, d_model = x.shape
    rows = min(MIX_ROWS, seq)
    ns = seq // rows
    tiles_per_chunk = chunk // rows
    nchunks = bsz * seq // chunk
    conv_w = d_model // 2
    in_w = p["win"].shape[1]
    tri = jnp.asarray(np.triu(np.ones((rows, rows), np.float32), 1), jnp.bfloat16)
    full = lambda shape: pl.BlockSpec(shape, lambda b, s: (0,) * len(shape))
    smem = pl.BlockSpec(memory_space=pltpu.SMEM)
    tok_spec = pl.BlockSpec((1, rows, d_model), lambda b, s: (b, s, 0))
    tile_spec = pl.BlockSpec((rows * SUBLANES, LANES), lambda b, s: (b * ns + s, 0))
    sel_spec = pl.BlockSpec((1, TOP_K, rows), lambda b, s: (b * ns + s, 0, 0))
    cnt_spec = pl.BlockSpec((1, N_EXPERTS, LANES), lambda b, s: ((b * ns + s) // tiles_per_chunk, 0, 0))
    t_total = bsz * seq
    sel_shape = (bsz * ns, TOP_K, rows)
    return pl.pallas_call(
        functools.partial(_mixer_kernel, tiles_per_chunk),
        grid=(bsz, ns),
        in_specs=[smem, smem, tok_spec, full((1, d_model)), full((d_model, in_w)), full((1, in_w)),
                  full((CONV_K, conv_w)), full((1, LANES)), full((1, LANES)), full((ATT_BLOCK, 2 * ATT_BLOCK)),
                  full((1, conv_w)), full((1, conv_w)), full((d_model, d_model)), full((1, d_model)),
                  full((1, d_model)), full((N_EXPERTS, d_model)), full((N_EXPERTS, 1)), full((rows, rows))],
        out_specs=[tile_spec, tile_spec, sel_spec, sel_spec, sel_spec, cnt_spec],
        out_shape=[jax.ShapeDtypeStruct((t_total * SUBLANES, LANES), jnp.float32),
                   jax.ShapeDtypeStruct((t_total * SUBLANES, LANES), jnp.float32),
                   jax.ShapeDtypeStruct(sel_shape, jnp.int32),
                   jax.ShapeDtypeStruct(sel_shape, jnp.float32),
                   jax.ShapeDtypeStruct(sel_shape, jnp.int32),
                   jax.ShapeDtypeStruct((nchunks, N_EXPERTS, LANES), jnp.int32)],
        scratch_shapes=[pltpu.VMEM((N_KV_HEADS, ATT_BLOCK + rows, LANES), jnp.bfloat16),
                        pltpu.VMEM((N_KV_HEADS, ATT_BLOCK + rows, LANES), jnp.bfloat16),
                        pltpu.VMEM((SUBLANES, conv_w), jnp.float32),
                        pltpu.VMEM((2, N_KV_HEADS, Q_PER_KV * ATT_BLOCK, 2 * ATT_BLOCK), jnp.float32),
                        pltpu.VMEM((N_Q_HEADS, rows, LANES), jnp.bfloat16),
                        pltpu.VMEM((N_EXPERTS, LANES), jnp.float32)],
        compiler_params=pltpu.CompilerParams(dimension_semantics=("arbitrary", "arbitrary"),
                                             vmem_limit_bytes=VMEM_LIMIT_BYTES),
        name="mixer",
    )(p["sinks"], p["rel_bias"], x, p["gmix"], p["win"], p["bin"], p["convw"], p["gq2"], p["gk2"], bucket,
      p["gc"], p["ga"], p["wout"], p["bout"], p["gffn"], p["wrt"], p["br"], tri)


def _moe_kernel(nbc_ref, bexp_ref, bfirst_ref, bslot_ref, bnext_ref, firstexp_ref,
                x1_hbm, h_hbm, dest_hbm, gate_hbm, fill_hbm, w1_hbm, w2_hbm, b1_ref, b2_ref,
                out_hbm,
                acc, hbuf, w1buf, w2buf, gbuf0, gbuf1, rbuf0, rbuf1, obuf, dest_s, gate_s, list_s,
                io_sem, w_sem, l_sem, o_sem):
    chunk = hbuf.shape[0] // SUBLANES - 1
    chunk_rows = chunk * SUBLANES
    n_assign = TOP_K * chunk
    pad_block = bexp_ref.shape[1] - 1
    d_model = w1buf.shape[1]
    d_ff = w2buf.shape[1]
    nk = d_model // LANES
    c = pl.program_id(0)
    row0 = pl.multiple_of(c * chunk_rows, SUBLANES)

    def weight_copies(e, slot):
        return (pltpu.make_async_copy(w1_hbm.at[e], w1buf.at[slot], w_sem.at[0, slot]),
                pltpu.make_async_copy(w2_hbm.at[e], w2buf.at[slot], w_sem.at[1, slot]))

    def chunk_in_copies():
        return (pltpu.make_async_copy(x1_hbm.at[pl.ds(row0, chunk_rows)], acc.at[pl.ds(0, chunk_rows)], io_sem.at[0]),
                pltpu.make_async_copy(h_hbm.at[pl.ds(row0, chunk_rows)], hbuf.at[pl.ds(0, chunk_rows)], io_sem.at[1]))

    def list_copies():
        return (pltpu.make_async_copy(dest_hbm.at[c], dest_s, l_sem.at[0]),
                pltpu.make_async_copy(gate_hbm.at[c], gate_s.at[:, pl.ds(0, n_assign)], l_sem.at[1]),
                pltpu.make_async_copy(fill_hbm, list_s, l_sem.at[2]))

    for cp in list_copies() + chunk_in_copies():
        cp.start()

    @pl.when(c == 0)
    def _first_weights():
        for cp in weight_copies(firstexp_ref[0], 0):
            cp.start()

    gate_s[0, n_assign] = jnp.float32(0.0)
    for cp in list_copies():
        cp.wait()

    def invert_body(i, carry):
        for jj in range(INVERT_UNROLL):
            a = i * INVERT_UNROLL + jj
            list_s[0, dest_s[0, a]] = a
        return carry

    lax.fori_loop(0, n_assign // INVERT_UNROLL, invert_body, 0)

    acc[pl.ds(chunk_rows, SUBLANES), :] = jnp.zeros((SUBLANES, LANES), jnp.float32)
    hbuf[pl.ds(chunk_rows, SUBLANES), :] = jnp.zeros((SUBLANES, LANES), jnp.float32)
    rbuf1[...] = jnp.zeros(rbuf1.shape, rbuf1.dtype)
    for cp in chunk_in_copies():
        cp.wait()

    def token_rows(a):
        return pl.ds(pl.multiple_of((a >> TOP_K_SHIFT) * SUBLANES, SUBLANES), SUBLANES)

    def weights_step(g):
        @pl.when(bfirst_ref[c, g] == 1)
        def _():
            slot = bslot_ref[c, g]
            for cp in weight_copies(bexp_ref[c, g], slot):
                cp.wait()
            nxt = bnext_ref[c, g]

            @pl.when(nxt >= 0)
            def _():
                for cp in weight_copies(nxt, 1 - slot):
                    cp.start()

    def gather_rows(g, gbuf, j_lo, j_hi):
        s0 = g * MOE_ROWS
        for j in range(j_lo, j_hi):
            gbuf[pl.ds(j * SUBLANES, SUBLANES), :] = hbuf[token_rows(list_s[0, s0 + j]), :]

    def scatter_rows(g, rbuf, j_lo, j_hi):
        s0 = g * MOE_ROWS
        for j0 in range(j_lo, j_hi, SCATTER_UNROLL):
            dsts, vals = [], []
            for j in range(j0, j0 + SCATTER_UNROLL):
                a = list_s[0, s0 + j]
                dst = token_rows(a)
                dsts.append(dst)
                vals.append(acc[dst, :] + gate_s[0, a] * rbuf[pl.ds(j * SUBLANES, SUBLANES), :])
            for dst, val in zip(dsts, vals):
                acc[dst, :] = val

    n_hid = 2 * d_ff // MXU_COLS
    n_out = d_model // MXU_COLS
    rows_hid = MOE_ROWS // 2 // n_hid
    rows_out = MOE_ROWS // 2 // n_out

    def stage(g_gather, gbuf_in, g_compute, gbuf, rbuf, g_scatter, rbuf_out):
        e = bexp_ref[c, g_compute]
        slot = bslot_ref[c, g_compute]
        xs = [gbuf[pl.ds(kk, MOE_ROWS, stride=SUBLANES), :] for kk in range(nk)]
        xb = jnp.concatenate(xs, axis=-1).astype(jnp.bfloat16)
        acts = []
        for cg in range(n_hid):
            cols = pl.ds(cg * MXU_COLS, MXU_COLS)
            hid = jnp.dot(xb, w1buf[slot, :, cols], preferred_element_type=jnp.float32) + b1_ref[e, :, cols]
            glu = jnp.minimum(hid[:, :LANES], SWIGLU_LIMIT)
            lin = jnp.clip(hid[:, LANES:], -SWIGLU_LIMIT, SWIGLU_LIMIT)
            acts.append(glu * jax.nn.sigmoid(SWIGLU_ALPHA * glu) * (lin + 1.0))
            gather_rows(g_gather, gbuf_in, cg * rows_hid, (cg + 1) * rows_hid)
            scatter_rows(g_scatter, rbuf_out, cg * rows_hid, (cg + 1) * rows_hid)
        act = jnp.concatenate(acts, axis=-1).astype(jnp.bfloat16)
        half = MOE_ROWS // 2
        for og in range(n_out):
            cols = pl.ds(og * MXU_COLS, MXU_COLS)
            y = jnp.dot(act, w2buf[slot, :, cols], preferred_element_type=jnp.float32) + b2_ref[e, :, cols]
            for kk in range(MXU_COLS // LANES):
                rbuf[pl.ds(og * (MXU_COLS // LANES) + kk, MOE_ROWS, stride=SUBLANES), :] = (
                    y[:, kk * LANES:(kk + 1) * LANES])
            gather_rows(g_gather, gbuf_in, half + og * rows_out, half + (og + 1) * rows_out)
            scatter_rows(g_scatter, rbuf_out, half + og * rows_out, half + (og + 1) * rows_out)

    gather_rows(0, gbuf0, 0, MOE_ROWS)

    def pair_body(p, carry):
        g0 = 2 * p
        weights_step(g0)
        stage(g0 + 1, gbuf1, g0, gbuf0, rbuf0, jnp.where(p == 0, pad_block, g0 - 1), rbuf1)
        weights_step(g0 + 1)
        stage(g0 + 2, gbuf0, g0 + 1, gbuf1, rbuf1, g0, rbuf0)
        return carry

    npairs = (nbc_ref[c] + 1) // 2
    lax.fori_loop(0, npairs, pair_body, 0)
    scatter_rows(2 * npairs - 1, rbuf1, 0, MOE_ROWS)

    def out_copy(grp, buf):
        return pltpu.make_async_copy(obuf.at[buf], out_hbm.at[pl.ds(c * chunk + grp * OUT_ROWS, OUT_ROWS)],
                                     o_sem.at[buf])

    def out_body(gp, carry):
        for buf in range(2):
            grp = 2 * gp + buf

            @pl.when(gp > 0)
            def _():
                out_copy(grp - 2, buf).wait()

            first = grp * (OUT_ROWS * SUBLANES)
            obuf[buf] = jnp.concatenate(
                [acc[pl.ds(first + kk, OUT_ROWS, stride=SUBLANES), :] for kk in range(nk)], axis=-1)
            out_copy(grp, buf).start()
        return carry

    n_groups = chunk // OUT_ROWS
    lax.fori_loop(0, n_groups // 2, out_body, 0)
    for buf in range(2):
        out_copy(n_groups - 2 + buf, buf).wait()


def _block_tables(nblk, table_blocks):
    nchunks = nblk.shape[0]
    bend = jnp.cumsum(nblk, axis=1)
    nbc = bend[:, -1]
    g = jnp.arange(table_blocks, dtype=jnp.int32)
    nonempty = nblk > 0
    last_exp = jnp.max(jnp.where(nonempty, jnp.arange(N_EXPERTS, dtype=jnp.int32)[None, :], 0), axis=1)
    bexp = jnp.sum((g[None, :, None] >= bend[:, None, :]).astype(jnp.int32), axis=-1)
    bexp = jnp.where(g[None, :] >= nbc[:, None], last_exp[:, None], bexp)
    bstart = bend - nblk
    bfirst = ((g[None, :] == jnp.take_along_axis(bstart, bexp, axis=1)) & (g[None, :] < nbc[:, None])).astype(jnp.int32)
    flat = nonempty.reshape(-1)
    pair_slot = ((jnp.cumsum(flat.astype(jnp.int32)) - 1) % 2).reshape(nchunks, N_EXPERTS)
    n_pairs = flat.shape[0]
    idx = jnp.where(flat, jnp.arange(n_pairs, dtype=jnp.int32), n_pairs)
    later = jnp.concatenate([lax.cummin(idx[::-1])[::-1][1:], jnp.full((1,), n_pairs, jnp.int32)])
    pair_next = jnp.where(later < n_pairs, later % N_EXPERTS, -1).reshape(nchunks, N_EXPERTS)
    bslot = jnp.take_along_axis(pair_slot, bexp, axis=1)
    bnext = jnp.take_along_axis(pair_next, bexp, axis=1)
    first_exp = (jnp.min(idx) % N_EXPERTS).reshape(1)
    return nbc.astype(jnp.int32), bexp, bfirst, bslot.astype(jnp.int32), bnext.astype(jnp.int32), first_exp


def _moe(x1r, hr, dest, gates, tables, w1, w2, b1, b2, chunk, table_blocks):
    t_total = x1r.shape[0] // SUBLANES
    nchunks = t_total // chunk
    d_model = w1.shape[1]
    d_ff = w2.shape[1]
    n_assign = TOP_K * chunk
    n_slots = table_blocks * MOE_ROWS
    fill = jnp.full((1, n_slots), n_assign, jnp.int32)
    anyspace = pl.BlockSpec(memory_space=pl.ANY)
    grid_spec = pltpu.PrefetchScalarGridSpec(
        num_scalar_prefetch=6,
        grid=(nchunks,),
        in_specs=[anyspace] * 7 + [pl.BlockSpec((N_EXPERTS, 1, 2 * d_ff), lambda c, *_: (0, 0, 0)),
                                   pl.BlockSpec((N_EXPERTS, 1, d_model), lambda c, *_: (0, 0, 0))],
        out_specs=anyspace,
        scratch_shapes=[pltpu.VMEM(((chunk + 1) * SUBLANES, LANES), jnp.float32),
                        pltpu.VMEM(((chunk + 1) * SUBLANES, LANES), jnp.float32),
                        pltpu.VMEM((2, d_model, 2 * d_ff), jnp.bfloat16),
                        pltpu.VMEM((2, d_ff, d_model), jnp.bfloat16),
                        pltpu.VMEM((MOE_ROWS * SUBLANES, LANES), jnp.float32),
                        pltpu.VMEM((MOE_ROWS * SUBLANES, LANES), jnp.float32),
                        pltpu.VMEM((MOE_ROWS * SUBLANES, LANES), jnp.float32),
                        pltpu.VMEM((MOE_ROWS * SUBLANES, LANES), jnp.float32),
                        pltpu.VMEM((2, OUT_ROWS, d_model), jnp.float32),
                        pltpu.SMEM((1, n_assign), jnp.int32),
                        pltpu.SMEM((1, n_assign + LANES), jnp.float32),
                        pltpu.SMEM((1, n_slots), jnp.int32),
                        pltpu.SemaphoreType.DMA((2,)),
                        pltpu.SemaphoreType.DMA((2, 2)),
                        pltpu.SemaphoreType.DMA((3,)),
                        pltpu.SemaphoreType.DMA((2,))],
    )
    return pl.pallas_call(
        _moe_kernel,
        grid_spec=grid_spec,
        out_shape=jax.ShapeDtypeStruct((t_total, d_model), jnp.float32),
        compiler_params=pltpu.CompilerParams(dimension_semantics=("arbitrary",),
                                             vmem_limit_bytes=VMEM_LIMIT_BYTES),
        name="moe",
    )(*tables, x1r, hr, dest, gates, fill, w1, w2, b1, b2)


def _t5_bucket_table():
    i = np.arange(ATT_BLOCK)[:, None]
    j = np.arange(2 * ATT_BLOCK)[None, :]
    rel = i + ATT_BLOCK - j
    max_exact = N_BUCKETS // 2
    nf = np.maximum(rel, 1).astype(np.float32)
    large = max_exact + (np.log(nf / max_exact) / np.float32(math.log(MAX_DISTANCE / max_exact))
                         * (N_BUCKETS - max_exact)).astype(np.int32)
    large = np.minimum(large, N_BUCKETS - 1)
    bucket = np.where(rel < max_exact, rel, large)
    return np.where((rel >= 0) & (rel < WINDOW), bucket, -1).astype(np.int32)


def _chunk_major(a, nchunks):
    return jnp.transpose(a, (0, 2, 1)).reshape(nchunks, 1, -1)


def kernel(x, norm_mix, w_in, b_in, conv_w, q_norm, k_norm, sinks, rel_bias, out_norm_conv, out_norm_attn,
           w_out, b_out, norm_ffn, w_router, b_router, w1, b1, w2, b2):
    bsz, seq, d_model = x.shape
    depth = w_in.shape[0]
    bf16 = jnp.bfloat16
    bucket = jnp.asarray(_t5_bucket_table())
    t_total = bsz * seq
    chunk = min(MOE_CHUNK, t_total)
    nchunks = t_total // chunk
    table_blocks = chunk * TOP_K // MOE_ROWS + N_EXPERTS + 2
    for l in range(depth):
        d_ff = w2.shape[2]
        params = dict(
            sinks=sinks[l], rel_bias=rel_bias, gmix=norm_mix[l][None], win=w_in[l].astype(bf16), bin=b_in[l][None],
            convw=conv_w[l], gq2=jnp.tile(q_norm[l], 2)[None] * (HEAD_DIM ** -0.5), gk2=jnp.tile(k_norm[l], 2)[None],
            gc=out_norm_conv[l][None], ga=out_norm_attn[l][None], wout=w_out[l].astype(bf16), bout=b_out[l][None],
            gffn=norm_ffn[l][None], wrt=w_router[l].T, br=b_router[l][:, None])
        x1r, hr, ids_t, gates_t, rank_t, counts_t = _mixer(x, params, bucket, chunk)
        counts = counts_t[:, :, 0]
        padded = (counts + MOE_ROWS - 1) // MOE_ROWS * MOE_ROWS
        pad_starts = jnp.cumsum(padded, axis=1) - padded
        tables = _block_tables(padded // MOE_ROWS, table_blocks)
        ids_c = _chunk_major(ids_t, nchunks)
        dest = _chunk_major(rank_t, nchunks)
        for e in range(N_EXPERTS):
            dest = dest + jnp.where(ids_c == e, pad_starts[:, e:e + 1, None], 0)
        w1p, w2p = _wprep(w1[l], w2[l])
        b1p = jnp.transpose(b1[l].reshape(N_EXPERTS, 2 * d_ff // MXU_COLS, LANES, 2), (0, 1, 3, 2))
        b1p = b1p.reshape(N_EXPERTS, 1, 2 * d_ff)
        out = _moe(x1r, hr, dest, _chunk_major(gates_t, nchunks), tables, w1p, w2p, b1p, b2[l][:, None, :],
                   chunk, table_blocks)
        x = out.reshape(bsz, seq, d_model)
    return x
```

```python
import functools
import math

import jax
import jax.numpy as jnp
import numpy as np
from jax import lax
from jax.experimental import pallas as pl
from jax.experimental.pallas import tpu as pltpu

HEAD_DIM = 64
N_Q_HEADS = 8
N_KV_HEADS = 2
Q_PER_KV = N_Q_HEADS // N_KV_HEADS
CONV_K = 3
WINDOW = 128
ATT_BLOCK = 128
N_BUCKETS = 32
MAX_DISTANCE = 128
N_EXPERTS = 32
TOP_K = 4
TOP_K_SHIFT = 2
SWIGLU_LIMIT = 7.0
SWIGLU_ALPHA = 1.702
EPS = 1e-5
MASK_VALUE = -1e30

LANES = 128
SUBLANES = 8
MXU_COLS = 256
VMEM_LIMIT_BYTES = 56 * 1024 * 1024

MIX_ROWS = 512
MOE_CHUNK = 4096
MOE_ROWS = 128
OUT_ROWS = 128
PREP_ROWS = 256
INVERT_UNROLL = 16
SCATTER_UNROLL = 4


def _rms(x, g):
    return x * lax.rsqrt(jnp.mean(x * x, axis=-1, keepdims=True) + EPS) * g


def _half_head_norm(t, gain2, lo):
    t2 = t * t
    s_lo = jnp.sum(jnp.where(lo, t2, 0.0), axis=-1, keepdims=True)
    s_hi = jnp.sum(jnp.where(lo, 0.0, t2), axis=-1, keepdims=True)
    r = jnp.where(lo, lax.rsqrt(s_lo * (1.0 / HEAD_DIM) + EPS), lax.rsqrt(s_hi * (1.0 / HEAD_DIM) + EPS))
    return t * r * gain2


def _store_row_tiles(ref, val):
    rows = val.shape[0]
    for kk in range(val.shape[1] // LANES):
        ref[pl.ds(kk, rows, stride=SUBLANES), :] = val[:, kk * LANES:(kk + 1) * LANES]


def _wprep_kernel(w1_ref, w2_ref, perm_ref, w1o_ref, w2o_ref):
    perm = perm_ref[...]
    for c in range(w1_ref.shape[2] // MXU_COLS):
        cols = pl.ds(c * MXU_COLS, MXU_COLS)
        t = w1_ref[0, :, cols].astype(jnp.bfloat16)
        w1o_ref[0, :, cols] = jnp.dot(t, perm, preferred_element_type=jnp.float32).astype(jnp.bfloat16)
    w2o_ref[0] = w2_ref[0].astype(jnp.bfloat16)


def _wprep(w1, w2):
    n_e, d_model, two_ff = w1.shape
    d_ff, d_out = w2.shape[1], w2.shape[2]
    assert d_model == d_ff, "one row grid serves both weight tensors"
    j = np.arange(MXU_COLS)
    src = np.where(j < LANES, 2 * j, 2 * (j - LANES) + 1)
    perm = jnp.asarray((np.arange(MXU_COLS)[:, None] == src[None, :]).astype(np.float32), jnp.bfloat16)
    return pl.pallas_call(
        _wprep_kernel,
        grid=(n_e, d_model // PREP_ROWS),
        in_specs=[pl.BlockSpec((1, PREP_ROWS, two_ff), lambda e, r: (e, r, 0)),
                  pl.BlockSpec((1, PREP_ROWS, d_out), lambda e, r: (e, r, 0)),
                  pl.BlockSpec((MXU_COLS, MXU_COLS), lambda e, r: (0, 0))],
        out_specs=[pl.BlockSpec((1, PREP_ROWS, two_ff), lambda e, r: (e, r, 0)),
                   pl.BlockSpec((1, PREP_ROWS, d_out), lambda e, r: (e, r, 0))],
        out_shape=[jax.ShapeDtypeStruct(w1.shape, jnp.bfloat16), jax.ShapeDtypeStruct(w2.shape, jnp.bfloat16)],
        compiler_params=pltpu.CompilerParams(dimension_semantics=("arbitrary", "arbitrary")),
        name="wprep",
    )(w1, w2, perm)


def _mixer_kernel(tiles_per_chunk,
                  sinks_ref, relb_ref,
                  x_ref, gmix_ref, win_ref, bin_ref, convw_ref, gq_ref, gk_ref, bucket_ref,
                  gc_ref, ga_ref, wout_ref, bout_ref, gffn_ref, wrt_ref, br_ref, tri_ref,
                  x1_ref, h2_ref, ids_ref, gates_ref, rank_ref, counts_ref,
                  kd_scr, vd_scr, cu_scr, bias_scr, qm_scr, cnt_scr):
    rows = x_ref.shape[1]
    d_model = x_ref.shape[2]
    conv_w = d_model // 2
    nblk = rows // ATT_BLOCK
    b = pl.program_id(0)
    s = pl.program_id(1)
    tile = b * pl.num_programs(1) + s

    @pl.when((b == 0) & (s == 0))
    def _build_bias():
        bk = bucket_ref[...]
        col = lax.broadcasted_iota(jnp.int32, bk.shape, 1)
        accs = [jnp.full(bk.shape, MASK_VALUE, jnp.float32) for _ in range(N_Q_HEADS)]
        for bb in range(N_BUCKETS):
            hit = bk == bb
            for h in range(N_Q_HEADS):
                accs[h] = jnp.where(hit, relb_ref[bb, h], accs[h])
        for h in range(N_Q_HEADS):
            g, j = divmod(h, Q_PER_KV)
            bias_scr[0, g, pl.ds(j * ATT_BLOCK, ATT_BLOCK), :] = accs[h]
            bias_scr[1, g, pl.ds(j * ATT_BLOCK, ATT_BLOCK), :] = jnp.where(col < ATT_BLOCK, MASK_VALUE, accs[h])

    @pl.when(s == 0)
    def _reset_state():
        kd_scr[:, pl.ds(0, ATT_BLOCK), :] = jnp.zeros((N_KV_HEADS, ATT_BLOCK, LANES), kd_scr.dtype)
        vd_scr[:, pl.ds(0, ATT_BLOCK), :] = jnp.zeros((N_KV_HEADS, ATT_BLOCK, LANES), vd_scr.dtype)
        cu_scr[...] = jnp.zeros(cu_scr.shape, cu_scr.dtype)

    @pl.when(lax.rem(tile, tiles_per_chunk) == 0)
    def _reset_counts():
        cnt_scr[...] = jnp.zeros(cnt_scr.shape, cnt_scr.dtype)

    x = x_ref[0]
    h = _rms(x, gmix_ref[...]).astype(jnp.bfloat16)
    proj = jnp.dot(h, win_ref[...], preferred_element_type=jnp.float32) + bin_ref[...]
    b_gate = proj[:, 0:conv_w]
    c_gate = proj[:, conv_w:2 * conv_w]
    u = proj[:, 2 * conv_w:3 * conv_w]
    q = proj[:, 3 * conv_w:4 * conv_w]
    k = proj[:, 4 * conv_w:4 * conv_w + LANES]
    v = proj[:, 4 * conv_w + LANES:4 * conv_w + 2 * LANES]

    cu = c_gate * u
    row = lax.broadcasted_iota(jnp.int32, cu.shape, 0)
    prev2 = cu_scr[SUBLANES - 2:SUBLANES - 1, :]
    prev1 = cu_scr[SUBLANES - 1:SUBLANES, :]
    r1 = jnp.where(row == 0, prev1, pltpu.roll(cu, 1, 0))
    r2 = jnp.where(row == 0, prev2, jnp.where(row == 1, prev1, pltpu.roll(cu, 2, 0)))
    y_conv = b_gate * (convw_ref[2:3, :] * cu + convw_ref[1:2, :] * r1 + convw_ref[0:1, :] * r2)
    cu_scr[...] = cu[rows - SUBLANES:rows, :]

    lane = lax.broadcasted_iota(jnp.int32, (rows, LANES), 1)
    lo = lane < HEAD_DIM
    kn = _half_head_norm(k, gk_ref[...], lo)
    krot = pltpu.roll(kn, HEAD_DIM, 1)
    vrot = pltpu.roll(v, HEAD_DIM, 1)
    kd_scr[0, pl.ds(ATT_BLOCK, rows), :] = jnp.where(lo, kn, krot).astype(kd_scr.dtype)
    kd_scr[1, pl.ds(ATT_BLOCK, rows), :] = jnp.where(lo, krot, kn).astype(kd_scr.dtype)
    vd_scr[0, pl.ds(ATT_BLOCK, rows), :] = jnp.where(lo, v, vrot).astype(vd_scr.dtype)
    vd_scr[1, pl.ds(ATT_BLOCK, rows), :] = jnp.where(lo, vrot, v).astype(vd_scr.dtype)
    for c in range(conv_w // LANES):
        qn = _half_head_norm(q[:, c * LANES:(c + 1) * LANES], gq_ref[...], lo)
        qm_scr[2 * c] = jnp.where(lo, qn, 0.0).astype(qm_scr.dtype)
        qm_scr[2 * c + 1] = jnp.where(lo, 0.0, qn).astype(qm_scr.dtype)

    lo_b = lax.broadcasted_iota(jnp.int32, (ATT_BLOCK, LANES), 1) < HEAD_DIM
    ya_blocks = []
    for n in range(nblk):
        first = jnp.where(s == 0, 1, 0) if n == 0 else 0
        tiles = []
        for g in range(N_KV_HEADS):
            qs = jnp.concatenate([qm_scr[Q_PER_KV * g + j, pl.ds(n * ATT_BLOCK, ATT_BLOCK), :]
                                  for j in range(Q_PER_KV)], axis=0)
            kw = kd_scr[g, pl.ds(n * ATT_BLOCK, 2 * ATT_BLOCK), :]
            vw = vd_scr[g, pl.ds(n * ATT_BLOCK, 2 * ATT_BLOCK), :]
            sc = lax.dot_general(qs, kw, (((1,), (1,)), ((), ())), preferred_element_type=jnp.float32)
            logits = sc + bias_scr[first, g]
            ps, inv = [], []
            for j in range(Q_PER_KV):
                lj = logits[j * ATT_BLOCK:(j + 1) * ATT_BLOCK]
                sink = sinks_ref[Q_PER_KV * g + j]
                m = jnp.maximum(jnp.max(lj, axis=-1, keepdims=True), sink)
                p = jnp.exp(lj - m)
                den = jnp.sum(p, axis=-1, keepdims=True) + jnp.exp(sink - m)
                ps.append(p.astype(jnp.bfloat16))
                inv.append(1.0 / den)
            o = jnp.dot(jnp.concatenate(ps, axis=0), vw, preferred_element_type=jnp.float32)
            on = [o[j * ATT_BLOCK:(j + 1) * ATT_BLOCK] * inv[j] for j in range(Q_PER_KV)]
            tiles.append(jnp.where(lo_b, on[0], on[1]))
            tiles.append(jnp.where(lo_b, on[2], on[3]))
        ya_blocks.append(jnp.concatenate(tiles, axis=-1))
    y_attn = jnp.concatenate(ya_blocks, axis=0)

    for g in range(N_KV_HEADS):
        kd_scr[g, pl.ds(0, ATT_BLOCK), :] = kd_scr[g, pl.ds(rows, ATT_BLOCK), :]
        vd_scr[g, pl.ds(0, ATT_BLOCK), :] = vd_scr[g, pl.ds(rows, ATT_BLOCK), :]

    mix = jnp.concatenate([_rms(y_conv, gc_ref[...]), _rms(y_attn, ga_ref[...])], axis=-1)
    x1 = x + jnp.dot(mix.astype(jnp.bfloat16), wout_ref[...], preferred_element_type=jnp.float32) + bout_ref[...]
    _store_row_tiles(x1_ref, x1)
    h2 = _rms(x1, gffn_ref[...])
    _store_row_tiles(h2_ref, h2)
    h_hi = h2.astype(jnp.bfloat16)
    h_lo = (h2 - h_hi.astype(jnp.float32)).astype(jnp.bfloat16)
    wr = wrt_ref[...]
    w_hi = wr.astype(jnp.bfloat16)
    w_lo = (wr - w_hi.astype(jnp.float32)).astype(jnp.bfloat16)
    nt = (((1,), (1,)), ((), ()))
    lt = (lax.dot_general(w_hi, h_hi, nt, preferred_element_type=jnp.float32)
          + lax.dot_general(w_hi, h_lo, nt, preferred_element_type=jnp.float32)
          + lax.dot_general(w_lo, h_hi, nt, preferred_element_type=jnp.float32)) + br_ref[...]
    eid = lax.broadcasted_iota(jnp.int32, lt.shape, 0)
    vals, idxs, hits = [], [], []
    for _ in range(TOP_K):
        m = jnp.max(lt, axis=0, keepdims=True)
        idx = jnp.min(jnp.where(lt == m, eid, N_EXPERTS), axis=0, keepdims=True)
        hit = eid == idx
        vals.append(m)
        idxs.append(idx)
        hits.append(hit)
        lt = jnp.where(hit, -jnp.inf, lt)
    es = [jnp.exp(vv - vals[0]) for vv in vals]
    tot = es[0] + es[1] + es[2] + es[3]
    ids_ref[0] = jnp.concatenate(idxs, axis=0)
    gates_ref[0] = jnp.concatenate([e / tot for e in es], axis=0)

    member = jnp.where(hits[0] | hits[1] | hits[2] | hits[3], 1.0, 0.0)
    before = jnp.dot(member.astype(jnp.bfloat16), tri_ref[...], preferred_element_type=jnp.float32)
    running = cnt_scr[...][:, 0:1]
    rank_e = before + running
    rank_ref[0] = jnp.concatenate([jnp.sum(jnp.where(hh, rank_e, 0.0), axis=0, keepdims=True) for hh in hits],
                                  axis=0).astype(jnp.int32)
    new_counts = cnt_scr[...] + jnp.sum(member, axis=1, keepdims=True)
    cnt_scr[...] = new_counts
    counts_ref[0] = new_counts.astype(jnp.int32)


def _mixer(x, p, bucket, chunk):
    bsz, seq, d_model = x.shape
    rows = min(MIX_ROWS, seq)
    ns = seq // rows
    tiles_per_chunk = chunk // rows
    nchunks = bsz * seq // chunk
    conv_w = d_model // 2
    in_w = p["win"].shape[1]
    tri = jnp.asarray(np.triu(np.ones((rows, rows), np.float32), 1), jnp.bfloat16)
    full = lambda shape: pl.BlockSpec(shape, lambda b, s: (0,) * len(shape))
    smem = pl.BlockSpec(memory_space=pltpu.SMEM)
    tok_spec = pl.BlockSpec((1, rows, d_model), lambda b, s: (b, s, 0))
    tile_spec = pl.BlockSpec((rows * SUBLANES, LANES), lambda b, s: (b * ns + s, 0))
    sel_spec = pl.BlockSpec((1, TOP_K, rows), lambda b, s: (b * ns + s, 0, 0))
    cnt_spec = pl.BlockSpec((1, N_EXPERTS, LANES), lambda b, s: ((b * ns + s) // tiles_per_chunk, 0, 0))
    t_total = bsz * seq
    sel_shape = (bsz * ns, TOP_K, rows)
    return pl.pallas_call(
        functools.partial(_mixer_kernel, tiles_per_chunk),
        grid=(bsz, ns),
        in_specs=[smem, smem, tok_spec, full((1, d_model)), full((d_model, in_w)), full((1, in_w)),
                  full((CONV_K, conv_w)), full((1, LANES)), full((1, LANES)), full((ATT_BLOCK, 2 * ATT_BLOCK)),
                  full((1, conv_w)), full((1, conv_w)), full((d_model, d_model)), full((1, d_model)),
                  full((1, d_model)), full((N_EXPERTS, d_model)), full((N_EXPERTS, 1)), full((rows, rows))],
        out_specs=[tile_spec, tile_spec, sel_spec, sel_spec, sel_spec, cnt_spec],
        out_shape=[jax.ShapeDtypeStruct((t_total * SUBLANES, LANES), jnp.float32),
                   jax.ShapeDtypeStruct((t_total * SUBLANES, LANES), jnp.float32),
                   jax.ShapeDtypeStruct(sel_shape, jnp.int32),
                   jax.ShapeDtypeStruct(sel_shape, jnp.float32),
                   jax.ShapeDtypeStruct(sel_shape, jnp.int32),
                   jax.ShapeDtypeStruct((nchunks, N_EXPERTS, LANES), jnp.int32)],
        scratch_shapes=[pltpu.VMEM((N_KV_HEADS, ATT_BLOCK + rows, LANES), jnp.bfloat16),
                        pltpu.VMEM((N_KV_HEADS, ATT_BLOCK + rows, LANES), jnp.bfloat16),
                        pltpu.VMEM((SUBLANES, conv_w), jnp.float32),
                        pltpu.VMEM((2, N_KV_HEADS, Q_PER_KV * ATT_BLOCK, 2 * ATT_BLOCK), jnp.float32),
                        pltpu.VMEM((N_Q_HEADS, rows, LANES), jnp.bfloat16),
                        pltpu.VMEM((N_EXPERTS, LANES), jnp.float32)],
        compiler_params=pltpu.CompilerParams(dimension_semantics=("arbitrary", "arbitrary"),
                                             vmem_limit_bytes=VMEM_LIMIT_BYTES),
        name="mixer",
    )(p["sinks"], p["rel_bias"], x, p["gmix"], p["win"], p["bin"], p["convw"], p["gq2"], p["gk2"], bucket,
      p["gc"], p["ga"], p["wout"], p["bout"], p["gffn"], p["wrt"], p["br"], tri)


def _moe_kernel(nbc_ref, bexp_ref, bfirst_ref, bslot_ref, bnext_ref, firstexp_ref,
                x1_hbm, h_hbm, dest_hbm, gate_hbm, fill_hbm, w1_hbm, w2_hbm, b1_ref, b2_ref,
                out_hbm,
                acc, hbuf, w1buf, w2buf, gbuf0, gbuf1, rbuf0, rbuf1, obuf, dest_s, gate_s, list_s,
                io_sem, w_sem, l_sem, o_sem):
    chunk = hbuf.shape[0] // SUBLANES - 1
    chunk_rows = chunk * SUBLANES
    n_assign = TOP_K * chunk
    pad_block = bexp_ref.shape[1] - 1
    d_model = w1buf.shape[1]
    d_ff = w2buf.shape[1]
    nk = d_model // LANES
    c = pl.program_id(0)
    row0 = pl.multiple_of(c * chunk_rows, SUBLANES)

    def weight_copies(e, slot):
        return (pltpu.make_async_copy(w1_hbm.at[e], w1buf.at[slot], w_sem.at[0, slot]),
                pltpu.make_async_copy(w2_hbm.at[e], w2buf.at[slot], w_sem.at[1, slot]))

    def chunk_in_copies():
        return (pltpu.make_async_copy(x1_hbm.at[pl.ds(row0, chunk_rows)], acc.at[pl.ds(0, chunk_rows)], io_sem.at[0]),
                pltpu.make_async_copy(h_hbm.at[pl.ds(row0, chunk_rows)], hbuf.at[pl.ds(0, chunk_rows)], io_sem.at[1]))

    def list_copies():
        return (pltpu.make_async_copy(dest_hbm.at[c], dest_s, l_sem.at[0]),
                pltpu.make_async_copy(gate_hbm.at[c], gate_s.at[:, pl.ds(0, n_assign)], l_sem.at[1]),
                pltpu.make_async_copy(fill_hbm, list_s, l_sem.at[2]))

    for cp in list_copies() + chunk_in_copies():
        cp.start()

    @pl.when(c == 0)
    def _first_weights():
        for cp in weight_copies(firstexp_ref[0], 0):
            cp.start()

    gate_s[0, n_assign] = jnp.float32(0.0)
    for cp in list_copies():
        cp.wait()
    for kk in range(TOP_K):
        def invert_body(i, carry, kk=kk):
            for jj in range(INVERT_UNROLL):
                t = i * INVERT_UNROLL + jj
                list_s[0, dest_s[0, kk * chunk + t]] = t * TOP_K + kk
            return carry

        lax.fori_loop(0, chunk // INVERT_UNROLL, invert_body, 0)

    acc[pl.ds(chunk_rows, SUBLANES), :] = jnp.zeros((SUBLANES, LANES), jnp.float32)
    hbuf[pl.ds(chunk_rows, SUBLANES), :] = jnp.zeros((SUBLANES, LANES), jnp.float32)
    rbuf1[...] = jnp.zeros(rbuf1.shape, rbuf1.dtype)
    for cp in chunk_in_copies():
        cp.wait()

    def token_rows(a):
        return pl.ds(pl.multiple_of((a >> TOP_K_SHIFT) * SUBLANES, SUBLANES), SUBLANES)

    def weights_step(g):
        @pl.when(bfirst_ref[c, g] == 1)
        def _():
            slot = bslot_ref[c, g]
            for cp in weight_copies(bexp_ref[c, g], slot):
                cp.wait()
            nxt = bnext_ref[c, g]

            @pl.when(nxt >= 0)
            def _():
                for cp in weight_copies(nxt, 1 - slot):
                    cp.start()

    def gather_rows(g, gbuf, j_lo, j_hi):
        s0 = g * MOE_ROWS
        for j in range(j_lo, j_hi):
            gbuf[pl.ds(j * SUBLANES, SUBLANES), :] = hbuf[token_rows(list_s[0, s0 + j]), :]

    def scatter_rows(g, rbuf, j_lo, j_hi):
        s0 = g * MOE_ROWS
        for j0 in range(j_lo, j_hi, SCATTER_UNROLL):
            dsts, vals = [], []
            for j in range(j0, j0 + SCATTER_UNROLL):
                a = list_s[0, s0 + j]
                dst = token_rows(a)
                gate = gate_s[0, (a & (TOP_K - 1)) * chunk + (a >> TOP_K_SHIFT)]
                dsts.append(dst)
                vals.append(acc[dst, :] + gate * rbuf[pl.ds(j * SUBLANES, SUBLANES), :])
            for dst, val in zip(dsts, vals):
                acc[dst, :] = val

    n_hid = 2 * d_ff // MXU_COLS
    n_out = d_model // MXU_COLS
    rows_hid = MOE_ROWS // 2 // n_hid
    rows_out = MOE_ROWS // 2 // n_out

    def stage(g_gather, gbuf_in, g_compute, gbuf, rbuf, g_scatter, rbuf_out):
        e = bexp_ref[c, g_compute]
        slot = bslot_ref[c, g_compute]
        xs = [gbuf[pl.ds(kk, MOE_ROWS, stride=SUBLANES), :] for kk in range(nk)]
        xb = jnp.concatenate(xs, axis=-1).astype(jnp.bfloat16)
        acts = []
        for cg in range(n_hid):
            cols = pl.ds(cg * MXU_COLS, MXU_COLS)
            hid = jnp.dot(xb, w1buf[slot, :, cols], preferred_element_type=jnp.float32) + b1_ref[e, :, cols]
            glu = jnp.minimum(hid[:, :LANES], SWIGLU_LIMIT)
            lin = jnp.clip(hid[:, LANES:], -SWIGLU_LIMIT, SWIGLU_LIMIT)
            acts.append(glu * jax.nn.sigmoid(SWIGLU_ALPHA * glu) * (lin + 1.0))
            gather_rows(g_gather, gbuf_in, cg * rows_hid, (cg + 1) * rows_hid)
            scatter_rows(g_scatter, rbuf_out, cg * rows_hid, (cg + 1) * rows_hid)
        act = jnp.concatenate(acts, axis=-1).astype(jnp.bfloat16)
        half = MOE_ROWS // 2
        for og in range(n_out):
            cols = pl.ds(og * MXU_COLS, MXU_COLS)
            y = jnp.dot(act, w2buf[slot, :, cols], preferred_element_type=jnp.float32) + b2_ref[e, :, cols]
            for kk in range(MXU_COLS // LANES):
                rbuf[pl.ds(og * (MXU_COLS // LANES) + kk, MOE_ROWS, stride=SUBLANES), :] = (
                    y[:, kk * LANES:(kk + 1) * LANES])
            gather_rows(g_gather, gbuf_in, half + og * rows_out, half + (og + 1) * rows_out)
            scatter_rows(g_scatter, rbuf_out, half + og * rows_out, half + (og + 1) * rows_out)

    gather_rows(0, gbuf0, 0, MOE_ROWS)

    def pair_body(p, carry):
        g0 = 2 * p
        weights_step(g0)
        stage(g0 + 1, gbuf1, g0, gbuf0, rbuf0, jnp.where(p == 0, pad_block, g0 - 1), rbuf1)
        weights_step(g0 + 1)
        stage(g0 + 2, gbuf0, g0 + 1, gbuf1, rbuf1, g0, rbuf0)
        return carry

    npairs = (nbc_ref[c] + 1) // 2
    lax.fori_loop(0, npairs, pair_body, 0)
    scatter_rows(2 * npairs - 1, rbuf1, 0, MOE_ROWS)

    def out_copy(grp, buf):
        return pltpu.make_async_copy(obuf.at[buf], out_hbm.at[pl.ds(c * chunk + grp * OUT_ROWS, OUT_ROWS)],
                                     o_sem.at[buf])

    def out_body(gp, carry):
        for buf in range(2):
            grp = 2 * gp + buf

            @pl.when(gp > 0)
            def _():
                out_copy(grp - 2, buf).wait()

            first = grp * (OUT_ROWS * SUBLANES)
            obuf[buf] = jnp.concatenate(
                [acc[pl.ds(first + kk, OUT_ROWS, stride=SUBLANES), :] for kk in range(nk)], axis=-1)
            out_copy(grp, buf).start()
        return carry

    n_groups = chunk // OUT_ROWS
    lax.fori_loop(0, n_groups // 2, out_body, 0)
    for buf in range(2):
        out_copy(n_groups - 2 + buf, buf).wait()


def _block_tables(nblk, table_blocks):
    nchunks = nblk.shape[0]
    bend = jnp.cumsum(nblk, axis=1)
    nbc = bend[:, -1]
    g = jnp.arange(table_blocks, dtype=jnp.int32)
    nonempty = nblk > 0
    last_exp = jnp.max(jnp.where(nonempty, jnp.arange(N_EXPERTS, dtype=jnp.int32)[None, :], 0), axis=1)
    bexp = jnp.sum((g[None, :, None] >= bend[:, None, :]).astype(jnp.int32), axis=-1)
    bexp = jnp.where(g[None, :] >= nbc[:, None], last_exp[:, None], bexp)
    bstart = bend - nblk
    bfirst = ((g[None, :] == jnp.take_along_axis(bstart, bexp, axis=1)) & (g[None, :] < nbc[:, None])).astype(jnp.int32)
    flat = nonempty.reshape(-1)
    pair_slot = ((jnp.cumsum(flat.astype(jnp.int32)) - 1) % 2).reshape(nchunks, N_EXPERTS)
    n_pairs = flat.shape[0]
    idx = jnp.where(flat, jnp.arange(n_pairs, dtype=jnp.int32), n_pairs)
    later = jnp.concatenate([lax.cummin(idx[::-1])[::-1][1:], jnp.full((1,), n_pairs, jnp.int32)])
    pair_next = jnp.where(later < n_pairs, later % N_EXPERTS, -1).reshape(nchunks, N_EXPERTS)
    bslot = jnp.take_along_axis(pair_slot, bexp, axis=1)
    bnext = jnp.take_along_axis(pair_next, bexp, axis=1)
    first_exp = (jnp.min(idx) % N_EXPERTS).reshape(1)
    return nbc.astype(jnp.int32), bexp, bfirst, bslot.astype(jnp.int32), bnext.astype(jnp.int32), first_exp


def _moe(x1r, hr, dest, gates, tables, w1, w2, b1, b2, chunk, table_blocks):
    t_total = x1r.shape[0] // SUBLANES
    nchunks = t_total // chunk
    d_model = w1.shape[1]
    d_ff = w2.shape[1]
    n_assign = TOP_K * chunk
    n_slots = table_blocks * MOE_ROWS
    fill = jnp.full((1, n_slots), n_assign + TOP_K - 1, jnp.int32)
    anyspace = pl.BlockSpec(memory_space=pl.ANY)
    grid_spec = pltpu.PrefetchScalarGridSpec(
        num_scalar_prefetch=6,
        grid=(nchunks,),
        in_specs=[anyspace] * 7 + [pl.BlockSpec((N_EXPERTS, 1, 2 * d_ff), lambda c, *_: (0, 0, 0)),
                                   pl.BlockSpec((N_EXPERTS, 1, d_model), lambda c, *_: (0, 0, 0))],
        out_specs=anyspace,
        scratch_shapes=[pltpu.VMEM(((chunk + 1) * SUBLANES, LANES), jnp.float32),
                        pltpu.VMEM(((chunk + 1) * SUBLANES, LANES), jnp.float32),
                        pltpu.VMEM((2, d_model, 2 * d_ff), jnp.bfloat16),
                        pltpu.VMEM((2, d_ff, d_model), jnp.bfloat16),
                        pltpu.VMEM((MOE_ROWS * SUBLANES, LANES), jnp.float32),
                        pltpu.VMEM((MOE_ROWS * SUBLANES, LANES), jnp.float32),
                        pltpu.VMEM((MOE_ROWS * SUBLANES, LANES), jnp.float32),
                        pltpu.VMEM((MOE_ROWS * SUBLANES, LANES), jnp.float32),
                        pltpu.VMEM((2, OUT_ROWS, d_model), jnp.float32),
                        pltpu.SMEM((1, n_assign), jnp.int32),
                        pltpu.SMEM((1, n_assign + LANES), jnp.float32),
                        pltpu.SMEM((1, n_slots), jnp.int32),
                        pltpu.SemaphoreType.DMA((2,)),
                        pltpu.SemaphoreType.DMA((2, 2)),
                        pltpu.SemaphoreType.DMA((3,)),
                        pltpu.SemaphoreType.DMA((2,))],
    )
    return pl.pallas_call(
        _moe_kernel,
        grid_spec=grid_spec,
        out_shape=jax.ShapeDtypeStruct((t_total, d_model), jnp.float32),
        compiler_params=pltpu.CompilerParams(dimension_semantics=("arbitrary",),
                                             vmem_limit_bytes=VMEM_LIMIT_BYTES),
        name="moe",
    )(*tables, x1r, hr, dest, gates, fill, w1, w2, b1, b2)


def _t5_bucket_table():
    i = np.arange(ATT_BLOCK)[:, None]
    j = np.arange(2 * ATT_BLOCK)[None, :]
    rel = i + ATT_BLOCK - j
    max_exact = N_BUCKETS // 2
    nf = np.maximum(rel, 1).astype(np.float32)
    large = max_exact + (np.log(nf / max_exact) / np.float32(math.log(MAX_DISTANCE / max_exact))
                         * (N_BUCKETS - max_exact)).astype(np.int32)
    large = np.minimum(large, N_BUCKETS - 1)
    bucket = np.where(rel < max_exact, rel, large)
    return np.where((rel >= 0) & (rel < WINDOW), bucket, -1).astype(np.int32)


def _chunk_major(a, nchunks):
    tiles, _, rows = a.shape
    a = a.reshape(nchunks, tiles // nchunks, TOP_K, rows)
    return jnp.transpose(a, (0, 2, 1, 3)).reshape(nchunks, 1, -1)


def kernel(x, norm_mix, w_in, b_in, conv_w, q_norm, k_norm, sinks, rel_bias, out_norm_conv, out_norm_attn,
           w_out, b_out, norm_ffn, w_router, b_router, w1, b1, w2, b2):
    bsz, seq, d_model = x.shape
    depth = w_in.shape[0]
    bf16 = jnp.bfloat16
    bucket = jnp.asarray(_t5_bucket_table())
    t_total = bsz * seq
    chunk = min(MOE_CHUNK, t_total)
    nchunks = t_total // chunk
    table_blocks = chunk * TOP_K // MOE_ROWS + N_EXPERTS + 2
    for l in range(depth):
        d_ff = w2.shape[2]
        params = dict(
            sinks=sinks[l], rel_bias=rel_bias, gmix=norm_mix[l][None], win=w_in[l].astype(bf16), bin=b_in[l][None],
            convw=conv_w[l], gq2=jnp.tile(q_norm[l], 2)[None] * (HEAD_DIM ** -0.5), gk2=jnp.tile(k_norm[l], 2)[None],
            gc=out_norm_conv[l][None], ga=out_norm_attn[l][None], wout=w_out[l].astype(bf16), bout=b_out[l][None],
            gffn=norm_ffn[l][None], wrt=w_router[l].T, br=b_router[l][:, None])
        x1r, hr, ids_t, gates_t, rank_t, counts_t = _mixer(x, params, bucket, chunk)
        counts = counts_t[:, :, 0]
        padded = (counts + MOE_ROWS - 1) // MOE_ROWS * MOE_ROWS
        pad_starts = jnp.cumsum(padded, axis=1) - padded
        tables = _block_tables(padded // MOE_ROWS, table_blocks)
        ids_c = _chunk_major(ids_t, nchunks)
        dest = _chunk_major(rank_t, nchunks)
        for e in range(N_EXPERTS):
            dest = dest + jnp.where(ids_c == e, pad_starts[:, e:e + 1, None], 0)
        w1p, w2p = _wprep(w1[l], w2[l])
        b1p = jnp.transpose(b1[l].reshape(N_EXPERTS, 2 * d_ff // MXU_COLS, LANES, 2), (0, 1, 3, 2))
        b1p = b1p.reshape(N_EXPERTS, 1, 2 * d_ff)
        out = _moe(x1r, hr, dest, _chunk_major(gates_t, nchunks), tables, w1p, w2p, b1p, b2[l][:, None, :],
                   chunk, table_blocks)
        x = out.reshape(bsz, seq, d_model)
    return x
```

```python
import functools
import math

import jax
import jax.numpy as jnp
import numpy as np
from jax import lax
from jax.experimental import pallas as pl
from jax.experimental.pallas import tpu as pltpu

HEAD_DIM = 64
N_Q_HEADS = 8
N_KV_HEADS = 2
Q_PER_KV = N_Q_HEADS // N_KV_HEADS
CONV_K = 3
WINDOW = 128
ATT_BLOCK = 128
N_BUCKETS = 32
MAX_DISTANCE = 128
N_EXPERTS = 32
TOP_K = 4
TOP_K_SHIFT = 2
SWIGLU_LIMIT = 7.0
SWIGLU_ALPHA = 1.702
EPS = 1e-5
MASK_VALUE = -1e30

LANES = 128
SUBLANES = 8
MXU_COLS = 256
VMEM_LIMIT_BYTES = 56 * 1024 * 1024

MIX_ROWS = 512
MOE_CHUNK = 4096
MOE_ROWS = 128
OUT_ROWS = 128
PREP_ROWS = 256
INVERT_UNROLL = 16
SCATTER_UNROLL = 4


def _rms(x, g):
    return x * lax.rsqrt(jnp.mean(x * x, axis=-1, keepdims=True) + EPS) * g


def _half_head_norm(t, gain2, lo):
    t2 = t * t
    s_lo = jnp.sum(jnp.where(lo, t2, 0.0), axis=-1, keepdims=True)
    s_hi = jnp.sum(jnp.where(lo, 0.0, t2), axis=-1, keepdims=True)
    r = jnp.where(lo, lax.rsqrt(s_lo * (1.0 / HEAD_DIM) + EPS), lax.rsqrt(s_hi * (1.0 / HEAD_DIM) + EPS))
    return t * r * gain2


def _store_row_tiles(ref, val):
    rows = val.shape[0]
    for kk in range(val.shape[1] // LANES):
        ref[pl.ds(kk, rows, stride=SUBLANES), :] = val[:, kk * LANES:(kk + 1) * LANES]


def _wprep_kernel(w1_ref, w2_ref, perm_ref, w1o_ref, w2o_ref):
    perm = perm_ref[...]
    for c in range(w1_ref.shape[2] // MXU_COLS):
        cols = pl.ds(c * MXU_COLS, MXU_COLS)
        t = w1_ref[0, :, cols].astype(jnp.bfloat16)
        w1o_ref[0, :, cols] = jnp.dot(t, perm, preferred_element_type=jnp.float32).astype(jnp.bfloat16)
    w2o_ref[0] = w2_ref[0].astype(jnp.bfloat16)


def _wprep(w1, w2):
    n_e, d_model, two_ff = w1.shape
    d_ff, d_out = w2.shape[1], w2.shape[2]
    assert d_model == d_ff, "one row grid serves both weight tensors"
    j = np.arange(MXU_COLS)
    src = np.where(j < LANES, 2 * j, 2 * (j - LANES) + 1)
    perm = jnp.asarray((np.arange(MXU_COLS)[:, None] == src[None, :]).astype(np.float32), jnp.bfloat16)
    return pl.pallas_call(
        _wprep_kernel,
        grid=(n_e, d_model // PREP_ROWS),
        in_specs=[pl.BlockSpec((1, PREP_ROWS, two_ff), lambda e, r: (e, r, 0)),
                  pl.BlockSpec((1, PREP_ROWS, d_out), lambda e, r: (e, r, 0)),
                  pl.BlockSpec((MXU_COLS, MXU_COLS), lambda e, r: (0, 0))],
        out_specs=[pl.BlockSpec((1, PREP_ROWS, two_ff), lambda e, r: (e, r, 0)),
                   pl.BlockSpec((1, PREP_ROWS, d_out), lambda e, r: (e, r, 0))],
        out_shape=[jax.ShapeDtypeStruct(w1.shape, jnp.bfloat16), jax.ShapeDtypeStruct(w2.shape, jnp.bfloat16)],
        compiler_params=pltpu.CompilerParams(dimension_semantics=("arbitrary", "arbitrary")),
        name="wprep",
    )(w1, w2, perm)


def _mixer_kernel(tiles_per_chunk,
                  sinks_ref, relb_ref,
                  x_ref, gmix_ref, win_ref, bin_ref, convw_ref, gq_ref, gk_ref, bucket_ref,
                  gc_ref, ga_ref, wout_ref, bout_ref, gffn_ref, wrt_ref, br_ref, tri_ref,
                  x1_ref, h2_ref, ids_ref, gates_ref, rank_ref, counts_ref,
                  kd_scr, vd_scr, cu_scr, bias_scr, qm_scr, cnt_scr):
    rows = x_ref.shape[1]
    d_model = x_ref.shape[2]
    conv_w = d_model // 2
    nblk = rows // ATT_BLOCK
    b = pl.program_id(0)
    s = pl.program_id(1)
    tile = b * pl.num_programs(1) + s

    @pl.when((b == 0) & (s == 0))
    def _build_bias():
        bk = bucket_ref[...]
        col = lax.broadcasted_iota(jnp.int32, bk.shape, 1)
        accs = [jnp.full(bk.shape, MASK_VALUE, jnp.float32) for _ in range(N_Q_HEADS)]
        for bb in range(N_BUCKETS):
            hit = bk == bb
            for h in range(N_Q_HEADS):
                accs[h] = jnp.where(hit, relb_ref[bb, h], accs[h])
        for h in range(N_Q_HEADS):
            g, j = divmod(h, Q_PER_KV)
            bias_scr[0, g, pl.ds(j * ATT_BLOCK, ATT_BLOCK), :] = accs[h]
            bias_scr[1, g, pl.ds(j * ATT_BLOCK, ATT_BLOCK), :] = jnp.where(col < ATT_BLOCK, MASK_VALUE, accs[h])

    @pl.when(s == 0)
    def _reset_state():
        kd_scr[:, pl.ds(0, ATT_BLOCK), :] = jnp.zeros((N_KV_HEADS, ATT_BLOCK, LANES), kd_scr.dtype)
        vd_scr[:, pl.ds(0, ATT_BLOCK), :] = jnp.zeros((N_KV_HEADS, ATT_BLOCK, LANES), vd_scr.dtype)
        cu_scr[...] = jnp.zeros(cu_scr.shape, cu_scr.dtype)

    @pl.when(lax.rem(tile, tiles_per_chunk) == 0)
    def _reset_counts():
        cnt_scr[...] = jnp.zeros(cnt_scr.shape, cnt_scr.dtype)

    x = x_ref[0]
    h = _rms(x, gmix_ref[...]).astype(jnp.bfloat16)
    proj = jnp.dot(h, win_ref[...], preferred_element_type=jnp.float32) + bin_ref[...]
    b_gate = proj[:, 0:conv_w]
    c_gate = proj[:, conv_w:2 * conv_w]
    u = proj[:, 2 * conv_w:3 * conv_w]
    q = proj[:, 3 * conv_w:4 * conv_w]
    k = proj[:, 4 * conv_w:4 * conv_w + LANES]
    v = proj[:, 4 * conv_w + LANES:4 * conv_w + 2 * LANES]

    cu = c_gate * u
    row = lax.broadcasted_iota(jnp.int32, cu.shape, 0)
    prev2 = cu_scr[SUBLANES - 2:SUBLANES - 1, :]
    prev1 = cu_scr[SUBLANES - 1:SUBLANES, :]
    r1 = jnp.where(row == 0, prev1, pltpu.roll(cu, 1, 0))
    r2 = jnp.where(row == 0, prev2, jnp.where(row == 1, prev1, pltpu.roll(cu, 2, 0)))
    y_conv = b_gate * (convw_ref[2:3, :] * cu + convw_ref[1:2, :] * r1 + convw_ref[0:1, :] * r2)
    cu_scr[...] = cu[rows - SUBLANES:rows, :]

    lane = lax.broadcasted_iota(jnp.int32, (rows, LANES), 1)
    lo = lane < HEAD_DIM
    kn = _half_head_norm(k, gk_ref[...], lo)
    krot = pltpu.roll(kn, HEAD_DIM, 1)
    vrot = pltpu.roll(v, HEAD_DIM, 1)
    kd_scr[0, pl.ds(ATT_BLOCK, rows), :] = jnp.where(lo, kn, krot).astype(kd_scr.dtype)
    kd_scr[1, pl.ds(ATT_BLOCK, rows), :] = jnp.where(lo, krot, kn).astype(kd_scr.dtype)
    vd_scr[0, pl.ds(ATT_BLOCK, rows), :] = jnp.where(lo, v, vrot).astype(vd_scr.dtype)
    vd_scr[1, pl.ds(ATT_BLOCK, rows), :] = jnp.where(lo, vrot, v).astype(vd_scr.dtype)
    for c in range(conv_w // LANES):
        qn = _half_head_norm(q[:, c * LANES:(c + 1) * LANES], gq_ref[...], lo)
        qm_scr[2 * c] = jnp.where(lo, qn, 0.0).astype(qm_scr.dtype)
        qm_scr[2 * c + 1] = jnp.where(lo, 0.0, qn).astype(qm_scr.dtype)

    lo_b = lax.broadcasted_iota(jnp.int32, (ATT_BLOCK, LANES), 1) < HEAD_DIM
    ya_blocks = []
    for n in range(nblk):
        first = jnp.where(s == 0, 1, 0) if n == 0 else 0
        tiles = []
        for g in range(N_KV_HEADS):
            qs = jnp.concatenate([qm_scr[Q_PER_KV * g + j, pl.ds(n * ATT_BLOCK, ATT_BLOCK), :]
                                  for j in range(Q_PER_KV)], axis=0)
            kw = kd_scr[g, pl.ds(n * ATT_BLOCK, 2 * ATT_BLOCK), :]
            vw = vd_scr[g, pl.ds(n * ATT_BLOCK, 2 * ATT_BLOCK), :]
            sc = lax.dot_general(qs, kw, (((1,), (1,)), ((), ())), preferred_element_type=jnp.float32)
            logits = sc + bias_scr[first, g]
            ps, inv = [], []
            for j in range(Q_PER_KV):
                lj = logits[j * ATT_BLOCK:(j + 1) * ATT_BLOCK]
                sink = sinks_ref[Q_PER_KV * g + j]
                m = jnp.maximum(jnp.max(lj, axis=-1, keepdims=True), sink)
                p = jnp.exp(lj - m)
                den = jnp.sum(p, axis=-1, keepdims=True) + jnp.exp(sink - m)
                ps.append(p.astype(jnp.bfloat16))
                inv.append(1.0 / den)
            o = jnp.dot(jnp.concatenate(ps, axis=0), vw, preferred_element_type=jnp.float32)
            on = [o[j * ATT_BLOCK:(j + 1) * ATT_BLOCK] * inv[j] for j in range(Q_PER_KV)]
            tiles.append(jnp.where(lo_b, on[0], on[1]))
            tiles.append(jnp.where(lo_b, on[2], on[3]))
        ya_blocks.append(jnp.concatenate(tiles, axis=-1))
    y_attn = jnp.concatenate(ya_blocks, axis=0)

    for g in range(N_KV_HEADS):
        kd_scr[g, pl.ds(0, ATT_BLOCK), :] = kd_scr[g, pl.ds(rows, ATT_BLOCK), :]
        vd_scr[g, pl.ds(0, ATT_BLOCK), :] = vd_scr[g, pl.ds(rows, ATT_BLOCK), :]

    mix = jnp.concatenate([_rms(y_conv, gc_ref[...]), _rms(y_attn, ga_ref[...])], axis=-1)
    x1 = x + jnp.dot(mix.astype(jnp.bfloat16), wout_ref[...], preferred_element_type=jnp.float32) + bout_ref[...]
    _store_row_tiles(x1_ref, x1)
    h2 = _rms(x1, gffn_ref[...])
    _store_row_tiles(h2_ref, h2)
    h_hi = h2.astype(jnp.bfloat16)
    h_lo = (h2 - h_hi.astype(jnp.float32)).astype(jnp.bfloat16)
    wr = wrt_ref[...]
    w_hi = wr.astype(jnp.bfloat16)
    w_lo = (wr - w_hi.astype(jnp.float32)).astype(jnp.bfloat16)
    nt = (((1,), (1,)), ((), ()))
    lt = (lax.dot_general(w_hi, h_hi, nt, preferred_element_type=jnp.float32)
          + lax.dot_general(w_hi, h_lo, nt, preferred_element_type=jnp.float32)
          + lax.dot_general(w_lo, h_hi, nt, preferred_element_type=jnp.float32)) + br_ref[...]
    eid = lax.broadcasted_iota(jnp.int32, lt.shape, 0)
    vals, idxs, hits = [], [], []
    for _ in range(TOP_K):
        m = jnp.max(lt, axis=0, keepdims=True)
        idx = jnp.min(jnp.where(lt == m, eid, N_EXPERTS), axis=0, keepdims=True)
        hit = eid == idx
        vals.append(m)
        idxs.append(idx)
        hits.append(hit)
        lt = jnp.where(hit, -jnp.inf, lt)
    es = [jnp.exp(vv - vals[0]) for vv in vals]
    tot = es[0] + es[1] + es[2] + es[3]
    ids_ref[0] = jnp.concatenate(idxs, axis=0)
    gates_ref[0] = jnp.concatenate([e / tot for e in es], axis=0)

    member = jnp.where(hits[0] | hits[1] | hits[2] | hits[3], 1.0, 0.0)
    before = jnp.dot(member.astype(jnp.bfloat16), tri_ref[...], preferred_element_type=jnp.float32)
    running = cnt_scr[...][:, 0:1]
    rank_e = before + running
    rank_ref[0] = jnp.concatenate([jnp.sum(jnp.where(hh, rank_e, 0.0), axis=0, keepdims=True) for hh in hits],
                                  axis=0).astype(jnp.int32)
    new_counts = cnt_scr[...] + jnp.sum(member, axis=1, keepdims=True)
    cnt_scr[...] = new_counts
    counts_ref[0] = new_counts.astype(jnp.int32)


def _mixer(x, p, bucket, chunk):
    bsz, seq, d_model = x.shape
    rows = min(MIX_ROWS, seq)
    ns = seq // rows
    tiles_per_chunk = chunk // rows
    nchunks = bsz * seq // chunk
    conv_w = d_model // 2
    in_w = p["win"].shape[1]
    tri = jnp.asarray(np.triu(np.ones((rows, rows), np.float32), 1), jnp.bfloat16)
    full = lambda shape: pl.BlockSpec(shape, lambda b, s: (0,) * len(shape))
    smem = pl.BlockSpec(memory_space=pltpu.SMEM)
    tok_spec = pl.BlockSpec((1, rows, d_model), lambda b, s: (b, s, 0))
    tile_spec = pl.BlockSpec((rows * SUBLANES, LANES), lambda b, s: (b * ns + s, 0))
    sel_spec = pl.BlockSpec((1, TOP_K, rows), lambda b, s: (b * ns + s, 0, 0))
    cnt_spec = pl.BlockSpec((1, N_EXPERTS, LANES), lambda b, s: ((b * ns + s) // tiles_per_chunk, 0, 0))
    t_total = bsz * seq
    sel_shape = (bsz * ns, TOP_K, rows)
    return pl.pallas_call(
        functools.partial(_mixer_kernel, tiles_per_chunk),
        grid=(bsz, ns),
        in_specs=[smem, smem, tok_spec, full((1, d_model)), full((d_model, in_w)), full((1, in_w)),
                  full((CONV_K, conv_w)), full((1, LANES)), full((1, LANES)), full((ATT_BLOCK, 2 * ATT_BLOCK)),
                  full((1, conv_w)), full((1, conv_w)), full((d_model, d_model)), full((1, d_model)),
                  full((1, d_model)), full((N_EXPERTS, d_model)), full((N_EXPERTS, 1)), full((rows, rows))],
        out_specs=[tile_spec, tile_spec, sel_spec, sel_spec, sel_spec, cnt_spec],
        out_shape=[jax.ShapeDtypeStruct((t_total * SUBLANES, LANES), jnp.float32),
                   jax.ShapeDtypeStruct((t_total * SUBLANES, LANES), jnp.float32),
                   jax.ShapeDtypeStruct(sel_shape, jnp.int32),
                   jax.ShapeDtypeStruct(sel_shape, jnp.float32),
                   jax.ShapeDtypeStruct(sel_shape, jnp.int32),
                   jax.ShapeDtypeStruct((nchunks, N_EXPERTS, LANES), jnp.int32)],
        scratch_shapes=[pltpu.VMEM((N_KV_HEADS, ATT_BLOCK + rows, LANES), jnp.bfloat16),
                        pltpu.VMEM((N_KV_HEADS, ATT_BLOCK + rows, LANES), jnp.bfloat16),
                        pltpu.VMEM((SUBLANES, conv_w), jnp.float32),
                        pltpu.VMEM((2, N_KV_HEADS, Q_PER_KV * ATT_BLOCK, 2 * ATT_BLOCK), jnp.float32),
                        pltpu.VMEM((N_Q_HEADS, rows, LANES), jnp.bfloat16),
                        pltpu.VMEM((N_EXPERTS, LANES), jnp.float32)],
        compiler_params=pltpu.CompilerParams(dimension_semantics=("arbitrary", "arbitrary"),
                                             vmem_limit_bytes=VMEM_LIMIT_BYTES),
        name="mixer",
    )(p["sinks"], p["rel_bias"], x, p["gmix"], p["win"], p["bin"], p["convw"], p["gq2"], p["gk2"], bucket,
      p["gc"], p["ga"], p["wout"], p["bout"], p["gffn"], p["wrt"], p["br"], tri)


def _moe_kernel(nbc_ref, bexp_ref, bfirst_ref, bslot_ref, bnext_ref, firstexp_ref,
                x1_hbm, h_hbm, dest_hbm, gate_hbm, fill_hbm, w1_hbm, w2_hbm, b1_ref, b2_ref,
                out_hbm,
                acc, hbuf, w1buf, w2buf, gbuf0, gbuf1, rbuf0, rbuf1, obuf, dest_s, gate_s, list_s,
                io_sem, w_sem, l_sem, o_sem):
    chunk = hbuf.shape[0] // SUBLANES - 1
    chunk_rows = chunk * SUBLANES
    n_assign = TOP_K * chunk
    pad_block = bexp_ref.shape[1] - 1
    d_model = w1buf.shape[1]
    d_ff = w2buf.shape[1]
    nk = d_model // LANES
    c = pl.program_id(0)
    row0 = pl.multiple_of(c * chunk_rows, SUBLANES)

    def weight_copies(e, slot):
        return (pltpu.make_async_copy(w1_hbm.at[e], w1buf.at[slot], w_sem.at[0, slot]),
                pltpu.make_async_copy(w2_hbm.at[e], w2buf.at[slot], w_sem.at[1, slot]))

    def chunk_in_copies():
        return (pltpu.make_async_copy(x1_hbm.at[pl.ds(row0, chunk_rows)], acc.at[pl.ds(0, chunk_rows)], io_sem.at[0]),
                pltpu.make_async_copy(h_hbm.at[pl.ds(row0, chunk_rows)], hbuf.at[pl.ds(0, chunk_rows)], io_sem.at[1]))

    def list_copies():
        return (pltpu.make_async_copy(dest_hbm.at[c], dest_s, l_sem.at[0]),
                pltpu.make_async_copy(gate_hbm.at[c], gate_s.at[:, pl.ds(0, n_assign)], l_sem.at[1]),
                pltpu.make_async_copy(fill_hbm, list_s, l_sem.at[2]))

    for cp in list_copies() + chunk_in_copies():
        cp.start()

    @pl.when(c == 0)
    def _first_weights():
        for cp in weight_copies(firstexp_ref[0], 0):
            cp.start()

    gate_s[0, n_assign] = jnp.float32(0.0)
    for cp in list_copies():
        cp.wait()

    def invert_body(i, carry):
        for jj in range(INVERT_UNROLL):
            a = i * INVERT_UNROLL + jj
            list_s[0, dest_s[0, a]] = a
        return carry

    lax.fori_loop(0, n_assign // INVERT_UNROLL, invert_body, 0)

    acc[pl.ds(chunk_rows, SUBLANES), :] = jnp.zeros((SUBLANES, LANES), jnp.float32)
    hbuf[pl.ds(chunk_rows, SUBLANES), :] = jnp.zeros((SUBLANES, LANES), jnp.float32)
    rbuf1[...] = jnp.zeros(rbuf1.shape, rbuf1.dtype)
    for cp in chunk_in_copies():
        cp.wait()

    def token_rows(a):
        return pl.ds(pl.multiple_of((a >> TOP_K_SHIFT) * SUBLANES, SUBLANES), SUBLANES)

    def weights_step(g):
        @pl.when(bfirst_ref[c, g] == 1)
        def _():
            slot = bslot_ref[c, g]
            for cp in weight_copies(bexp_ref[c, g], slot):
                cp.wait()
            nxt = bnext_ref[c, g]

            @pl.when(nxt >= 0)
            def _():
                for cp in weight_copies(nxt, 1 - slot):
                    cp.start()

    def gather_rows(g, gbuf, j_lo, j_hi):
        s0 = g * MOE_ROWS
        for j in range(j_lo, j_hi):
            gbuf[pl.ds(j * SUBLANES, SUBLANES), :] = hbuf[token_rows(list_s[0, s0 + j]), :]

    def scatter_rows(g, rbuf, j_lo, j_hi):
        s0 = g * MOE_ROWS
        for j0 in range(j_lo, j_hi, SCATTER_UNROLL):
            dsts, vals = [], []
            for j in range(j0, j0 + SCATTER_UNROLL):
                a = list_s[0, s0 + j]
                dst = token_rows(a)
                dsts.append(dst)
                vals.append(acc[dst, :] + gate_s[0, a] * rbuf[pl.ds(j * SUBLANES, SUBLANES), :])
            for dst, val in zip(dsts, vals):
                acc[dst, :] = val

    n_hid = 2 * d_ff // MXU_COLS
    n_out = d_model // MXU_COLS
    rows_hid = MOE_ROWS // 2 // n_hid
    rows_out = MOE_ROWS // 2 // n_out

    def stage(g_gather, gbuf_in, g_compute, gbuf, rbuf, g_scatter, rbuf_out):
        e = bexp_ref[c, g_compute]
        slot = bslot_ref[c, g_compute]
        xs = [gbuf[pl.ds(kk, MOE_ROWS, stride=SUBLANES), :] for kk in range(nk)]
        xb = jnp.concatenate(xs, axis=-1).astype(jnp.bfloat16)
        acts = []
        for cg in range(n_hid):
            cols = pl.ds(cg * MXU_COLS, MXU_COLS)
            hid = jnp.dot(xb, w1buf[slot, :, cols], preferred_element_type=jnp.float32) + b1_ref[e, :, cols]
            glu = jnp.minimum(hid[:, :LANES], SWIGLU_LIMIT)
            lin = jnp.clip(hid[:, LANES:], -SWIGLU_LIMIT, SWIGLU_LIMIT)
            acts.append(glu * jax.nn.sigmoid(SWIGLU_ALPHA * glu) * (lin + 1.0))
            gather_rows(g_gather, gbuf_in, cg * rows_hid, (cg + 1) * rows_hid)
            scatter_rows(g_scatter, rbuf_out, cg * rows_hid, (cg + 1) * rows_hid)
        act = jnp.concatenate(acts, axis=-1).astype(jnp.bfloat16)
        half = MOE_ROWS // 2
        for og in range(n_out):
            cols = pl.ds(og * MXU_COLS, MXU_COLS)
            y = jnp.dot(act, w2buf[slot, :, cols], preferred_element_type=jnp.float32) + b2_ref[e, :, cols]
            for kk in range(MXU_COLS // LANES):
                rbuf[pl.ds(og * (MXU_COLS // LANES) + kk, MOE_ROWS, stride=SUBLANES), :] = (
                    y[:, kk * LANES:(kk + 1) * LANES])
            gather_rows(g_gather, gbuf_in, half + og * rows_out, half + (og + 1) * rows_out)
            scatter_rows(g_scatter, rbuf_out, half + og * rows_out, half + (og + 1) * rows_out)

    gather_rows(0, gbuf0, 0, MOE_ROWS)

    def pair_body(p, carry):
        g0 = 2 * p
        weights_step(g0)
        stage(g0 + 1, gbuf1, g0, gbuf0, rbuf0, jnp.where(p == 0, pad_block, g0 - 1), rbuf1)
        weights_step(g0 + 1)
        stage(g0 + 2, gbuf0, g0 + 1, gbuf1, rbuf1, g0, rbuf0)
        return carry

    npairs = (nbc_ref[c] + 1) // 2
    lax.fori_loop(0, npairs, pair_body, 0)
    scatter_rows(2 * npairs - 1, rbuf1, 0, MOE_ROWS)

    def out_copy(grp, buf):
        return pltpu.make_async_copy(obuf.at[buf], out_hbm.at[pl.ds(c * chunk + grp * OUT_ROWS, OUT_ROWS)],
                                     o_sem.at[buf])

    def out_body(gp, carry):
        for buf in range(2):
            grp = 2 * gp + buf

            @pl.when(gp > 0)
            def _():
                out_copy(grp - 2, buf).wait()

            first = grp * (OUT_ROWS * SUBLANES)
            obuf[buf] = jnp.concatenate(
                [acc[pl.ds(first + kk, OUT_ROWS, stride=SUBLANES), :] for kk in range(nk)], axis=-1)
            out_copy(grp, buf).start()
        return carry

    n_groups = chunk // OUT_ROWS
    lax.fori_loop(0, n_groups // 2, out_body, 0)
    for buf in range(2):
        out_copy(n_groups - 2 + buf, buf).wait()


def _block_tables(nblk, table_blocks):
    nchunks = nblk.shape[0]
    bend = jnp.cumsum(nblk, axis=1)
    nbc = bend[:, -1]
    g = jnp.arange(table_blocks, dtype=jnp.int32)
    nonempty = nblk > 0
    last_exp = jnp.max(jnp.where(nonempty, jnp.arange(N_EXPERTS, dtype=jnp.int32)[None, :], 0), axis=1)
    bexp = jnp.sum((g[None, :, None] >= bend[:, None, :]).astype(jnp.int32), axis=-1)
    bexp = jnp.where(g[None, :] >= nbc[:, None], last_exp[:, None], bexp)
    bstart = bend - nblk
    bfirst = ((g[None, :] == jnp.take_along_axis(bstart, bexp, axis=1)) & (g[None, :] < nbc[:, None])).astype(jnp.int32)
    flat = nonempty.reshape(-1)
    pair_slot = ((jnp.cumsum(flat.astype(jnp.int32)) - 1) % 2).reshape(nchunks, N_EXPERTS)
    n_pairs = flat.shape[0]
    idx = jnp.where(flat, jnp.arange(n_pairs, dtype=jnp.int32), n_pairs)
    later = jnp.concatenate([lax.cummin(idx[::-1])[::-1][1:], jnp.full((1,), n_pairs, jnp.int32)])
    pair_next = jnp.where(later < n_pairs, later % N_EXPERTS, -1).reshape(nchunks, N_EXPERTS)
    bslot = jnp.take_along_axis(pair_slot, bexp, axis=1)
    bnext = jnp.take_along_axis(pair_next, bexp, axis=1)
    first_exp = (jnp.min(idx) % N_EXPERTS).reshape(1)
    return nbc.astype(jnp.int32), bexp, bfirst, bslot.astype(jnp.int32), bnext.astype(jnp.int32), first_exp


def _moe(x1r, hr, dest, gates, tables, w1, w2, b1, b2, chunk, table_blocks):
    t_total = x1r.shape[0] // SUBLANES
    nchunks = t_total // chunk
    d_model = w1.shape[1]
    d_ff = w2.shape[1]
    n_assign = TOP_K * chunk
    n_slots = table_blocks * MOE_ROWS
    fill = jnp.full((1, n_slots), n_assign, jnp.int32)
    anyspace = pl.BlockSpec(memory_space=pl.ANY)
    grid_spec = pltpu.PrefetchScalarGridSpec(
        num_scalar_prefetch=6,
        grid=(nchunks,),
        in_specs=[anyspace] * 7 + [pl.BlockSpec((N_EXPERTS, 1, 2 * d_ff), lambda c, *_: (0, 0, 0)),
                                   pl.BlockSpec((N_EXPERTS, 1, d_model), lambda c, *_: (0, 0, 0))],
        out_specs=anyspace,
        scratch_shapes=[pltpu.VMEM(((chunk + 1) * SUBLANES, LANES), jnp.float32),
                        pltpu.VMEM(((chunk + 1) * SUBLANES, LANES), jnp.float32),
                        pltpu.VMEM((2, d_model, 2 * d_ff), jnp.bfloat16),
                        pltpu.VMEM((2, d_ff, d_model), jnp.bfloat16),
                        pltpu.VMEM((MOE_ROWS * SUBLANES, LANES), jnp.float32),
                        pltpu.VMEM((MOE_ROWS * SUBLANES, LANES), jnp.float32),
                        pltpu.VMEM((MOE_ROWS * SUBLANES, LANES), jnp.float32),
                        pltpu.VMEM((MOE_ROWS * SUBLANES, LANES), jnp.float32),
                        pltpu.VMEM((2, OUT_ROWS, d_model), jnp.float32),
                        pltpu.SMEM((1, n_assign), jnp.int32),
                        pltpu.SMEM((1, n_assign + LANES), jnp.float32),
                        pltpu.SMEM((1, n_slots), jnp.int32),
                        pltpu.SemaphoreType.DMA((2,)),
                        pltpu.SemaphoreType.DMA((2, 2)),
                        pltpu.SemaphoreType.DMA((3,)),
                        pltpu.SemaphoreType.DMA((2,))],
    )
    return pl.pallas_call(
        _moe_kernel,
        grid_spec=grid_spec,
        out_shape=jax.ShapeDtypeStruct((t_total, d_model), jnp.float32),
        compiler_params=pltpu.CompilerParams(dimension_semantics=("arbitrary",),
                                             vmem_limit_bytes=VMEM_LIMIT_BYTES),
        name="moe",
    )(*tables, x1r, hr, dest, gates, fill, w1, w2, b1, b2)


def _t5_bucket_table():
    i = np.arange(ATT_BLOCK)[:, None]
    j = np.arange(2 * ATT_BLOCK)[None, :]
    rel = i + ATT_BLOCK - j
    max_exact = N_BUCKETS // 2
    nf = np.maximum(rel, 1).astype(np.float32)
    large = max_exact + (np.log(nf / max_exact) / np.float32(math.log(MAX_DISTANCE / max_exact))
                         * (N_BUCKETS - max_exact)).astype(np.int32)
    large = np.minimum(large, N_BUCKETS - 1)
    bucket = np.where(rel < max_exact, rel, large)
    return np.where((rel >= 0) & (rel < WINDOW), bucket, -1).astype(np.int32)


def _token_major(a, nchunks):
    return jnp.transpose(a, (0, 2, 1)).reshape(nchunks, 1, -1)


def kernel(x, norm_mix, w_in, b_in, conv_w, q_norm, k_norm, sinks, rel_bias, out_norm_conv, out_norm_attn,
           w_out, b_out, norm_ffn, w_router, b_router, w1, b1, w2, b2):
    bsz, seq, d_model = x.shape
    depth = w_in.shape[0]
    bf16 = jnp.bfloat16
    bucket = jnp.asarray(_t5_bucket_table())
    t_total = bsz * seq
    chunk = min(MOE_CHUNK, t_total)
    nchunks = t_total // chunk
    table_blocks = chunk * TOP_K // MOE_ROWS + N_EXPERTS + 2
    for l in range(depth):
        d_ff = w2.shape[2]
        params = dict(
            sinks=sinks[l], rel_bias=rel_bias, gmix=norm_mix[l][None], win=w_in[l].astype(bf16), bin=b_in[l][None],
            convw=conv_w[l], gq2=jnp.tile(q_norm[l], 2)[None] * (HEAD_DIM ** -0.5), gk2=jnp.tile(k_norm[l], 2)[None],
            gc=out_norm_conv[l][None], ga=out_norm_attn[l][None], wout=w_out[l].astype(bf16), bout=b_out[l][None],
            gffn=norm_ffn[l][None], wrt=w_router[l].T, br=b_router[l][:, None])
        x1r, hr, ids_t, gates_t, rank_t, counts_t = _mixer(x, params, bucket, chunk)
        counts = counts_t[:, :, 0]
        padded = (counts + MOE_ROWS - 1) // MOE_ROWS * MOE_ROWS
        pad_starts = jnp.cumsum(padded, axis=1) - padded
        tables = _block_tables(padded // MOE_ROWS, table_blocks)
        tiles_per_chunk = ids_t.shape[0] // nchunks
        tile_starts = jnp.repeat(pad_starts, tiles_per_chunk, axis=0)
        dest_t = rank_t
        for e in range(N_EXPERTS):
            dest_t = dest_t + jnp.where(ids_t == e, tile_starts[:, e:e + 1, None], 0)
        w1p, w2p = _wprep(w1[l], w2[l])
        b1p = jnp.transpose(b1[l].reshape(N_EXPERTS, 2 * d_ff // MXU_COLS, LANES, 2), (0, 1, 3, 2))
        b1p = b1p.reshape(N_EXPERTS, 1, 2 * d_ff)
        out = _moe(x1r, hr, _token_major(dest_t, nchunks), _token_major(gates_t, nchunks), tables, w1p, w2p, b1p,
                   b2[l][:, None, :], chunk, table_blocks)
        x = out.reshape(bsz, seq, d_model)
    return x
```

```python
import functools
import math

import jax
import jax.numpy as jnp
import numpy as np
from jax import lax
from jax.experimental import pallas as pl
from jax.experimental.pallas import tpu as pltpu

HEAD_DIM = 64
N_Q_HEADS = 8
N_KV_HEADS = 2
Q_PER_KV = N_Q_HEADS // N_KV_HEADS
CONV_K = 3
WINDOW = 128
ATT_BLOCK = 128
N_BUCKETS = 32
MAX_DISTANCE = 128
N_EXPERTS = 32
TOP_K = 4
TOP_K_SHIFT = 2
SWIGLU_LIMIT = 7.0
SWIGLU_ALPHA = 1.702
EPS = 1e-5
MASK_VALUE = -1e30

LANES = 128
SUBLANES = 8
MXU_COLS = 256
VMEM_LIMIT_BYTES = 56 * 1024 * 1024

MIX_ROWS = 512
MOE_CHUNK = 4096
MOE_ROWS = 128
OUT_ROWS = 128
PREP_ROWS = 256
INVERT_UNROLL = 16
SCATTER_UNROLL = 4


def _rms(x, g):
    return x * lax.rsqrt(jnp.mean(x * x, axis=-1, keepdims=True) + EPS) * g


def _half_head_norm(t, gain2, lo):
    t2 = t * t
    s_lo = jnp.sum(jnp.where(lo, t2, 0.0), axis=-1, keepdims=True)
    s_hi = jnp.sum(jnp.where(lo, 0.0, t2), axis=-1, keepdims=True)
    r = jnp.where(lo, lax.rsqrt(s_lo * (1.0 / HEAD_DIM) + EPS), lax.rsqrt(s_hi * (1.0 / HEAD_DIM) + EPS))
    return t * r * gain2


def _store_row_tiles(ref, val):
    rows = val.shape[0]
    for kk in range(val.shape[1] // LANES):
        ref[pl.ds(kk, rows, stride=SUBLANES), :] = val[:, kk * LANES:(kk + 1) * LANES]


def _wprep_kernel(w1_ref, w2_ref, perm_ref, w1o_ref, w2o_ref):
    perm = perm_ref[...]
    for c in range(w1_ref.shape[2] // MXU_COLS):
        cols = pl.ds(c * MXU_COLS, MXU_COLS)
        t = w1_ref[0, :, cols].astype(jnp.bfloat16)
        w1o_ref[0, :, cols] = jnp.dot(t, perm, preferred_element_type=jnp.float32).astype(jnp.bfloat16)
    w2o_ref[0] = w2_ref[0].astype(jnp.bfloat16)


def _wprep(w1, w2):
    n_e, d_model, two_ff = w1.shape
    d_ff, d_out = w2.shape[1], w2.shape[2]
    assert d_model == d_ff, "one row grid serves both weight tensors"
    j = np.arange(MXU_COLS)
    src = np.where(j < LANES, 2 * j, 2 * (j - LANES) + 1)
    perm = jnp.asarray((np.arange(MXU_COLS)[:, None] == src[None, :]).astype(np.float32), jnp.bfloat16)
    return pl.pallas_call(
        _wprep_kernel,
        grid=(n_e, d_model // PREP_ROWS),
        in_specs=[pl.BlockSpec((1, PREP_ROWS, two_ff), lambda e, r: (e, r, 0)),
                  pl.BlockSpec((1, PREP_ROWS, d_out), lambda e, r: (e, r, 0)),
                  pl.BlockSpec((MXU_COLS, MXU_COLS), lambda e, r: (0, 0))],
        out_specs=[pl.BlockSpec((1, PREP_ROWS, two_ff), lambda e, r: (e, r, 0)),
                   pl.BlockSpec((1, PREP_ROWS, d_out), lambda e, r: (e, r, 0))],
        out_shape=[jax.ShapeDtypeStruct(w1.shape, jnp.bfloat16), jax.ShapeDtypeStruct(w2.shape, jnp.bfloat16)],
        compiler_params=pltpu.CompilerParams(dimension_semantics=("arbitrary", "arbitrary")),
        name="wprep",
    )(w1, w2, perm)


def _mixer_kernel(tiles_per_seq, tiles_per_chunk,
                  sinks_ref, relb_ref,
                  x_ref, xnext_ref, gmix_ref, win_ref, bin_ref, convw_ref, gq_ref, gk_ref, bucket_ref,
                  gc_ref, ga_ref, wout_ref, bout_ref, gffn_ref, wrt_ref, br_ref, tri_ref,
                  x1_ref, h2_ref, ids_ref, gates_ref, rank_ref, counts_ref,
                  kd_scr, vd_scr, cu_scr, bias_scr, qm_scr, cnt_scr, proj_scr):
    rows = x_ref.shape[1]
    d_model = x_ref.shape[2]
    conv_w = d_model // 2
    in_w = win_ref.shape[1]
    nblk = rows // ATT_BLOCK
    tile = pl.program_id(0)
    s = lax.rem(tile, tiles_per_seq)

    def project(h, j):
        cols = pl.ds(j * MXU_COLS, MXU_COLS)
        proj_scr[:, cols] = jnp.dot(h, win_ref[:, cols], preferred_element_type=jnp.float32) + bin_ref[:, cols]

    @pl.when(tile == 0)
    def _first_projection():
        h0 = _rms(x_ref[0], gmix_ref[...]).astype(jnp.bfloat16)
        for j in range(in_w // MXU_COLS):
            project(h0, j)

    @pl.when(tile == 0)
    def _build_bias():
        bk = bucket_ref[...]
        col = lax.broadcasted_iota(jnp.int32, bk.shape, 1)
        accs = [jnp.full(bk.shape, MASK_VALUE, jnp.float32) for _ in range(N_Q_HEADS)]
        for bb in range(N_BUCKETS):
            hit = bk == bb
            for h in range(N_Q_HEADS):
                accs[h] = jnp.where(hit, relb_ref[bb, h], accs[h])
        for h in range(N_Q_HEADS):
            g, j = divmod(h, Q_PER_KV)
            bias_scr[0, g, pl.ds(j * ATT_BLOCK, ATT_BLOCK), :] = accs[h]
            bias_scr[1, g, pl.ds(j * ATT_BLOCK, ATT_BLOCK), :] = jnp.where(col < ATT_BLOCK, MASK_VALUE, accs[h])

    @pl.when(s == 0)
    def _reset_state():
        kd_scr[:, pl.ds(0, ATT_BLOCK), :] = jnp.zeros((N_KV_HEADS, ATT_BLOCK, LANES), kd_scr.dtype)
        vd_scr[:, pl.ds(0, ATT_BLOCK), :] = jnp.zeros((N_KV_HEADS, ATT_BLOCK, LANES), vd_scr.dtype)
        cu_scr[...] = jnp.zeros(cu_scr.shape, cu_scr.dtype)

    @pl.when(lax.rem(tile, tiles_per_chunk) == 0)
    def _reset_counts():
        cnt_scr[...] = jnp.zeros(cnt_scr.shape, cnt_scr.dtype)

    x = x_ref[0]
    b_gate = proj_scr[:, 0:conv_w]
    c_gate = proj_scr[:, conv_w:2 * conv_w]
    u = proj_scr[:, 2 * conv_w:3 * conv_w]
    q = proj_scr[:, 3 * conv_w:4 * conv_w]
    k = proj_scr[:, 4 * conv_w:4 * conv_w + LANES]
    v = proj_scr[:, 4 * conv_w + LANES:4 * conv_w + 2 * LANES]

    cu = c_gate * u
    row = lax.broadcasted_iota(jnp.int32, cu.shape, 0)
    prev2 = cu_scr[SUBLANES - 2:SUBLANES - 1, :]
    prev1 = cu_scr[SUBLANES - 1:SUBLANES, :]
    r1 = jnp.where(row == 0, prev1, pltpu.roll(cu, 1, 0))
    r2 = jnp.where(row == 0, prev2, jnp.where(row == 1, prev1, pltpu.roll(cu, 2, 0)))
    y_conv = b_gate * (convw_ref[2:3, :] * cu + convw_ref[1:2, :] * r1 + convw_ref[0:1, :] * r2)
    cu_scr[...] = cu[rows - SUBLANES:rows, :]

    lane = lax.broadcasted_iota(jnp.int32, (rows, LANES), 1)
    lo = lane < HEAD_DIM
    kn = _half_head_norm(k, gk_ref[...], lo)
    krot = pltpu.roll(kn, HEAD_DIM, 1)
    vrot = pltpu.roll(v, HEAD_DIM, 1)
    kd_scr[0, pl.ds(ATT_BLOCK, rows), :] = jnp.where(lo, kn, krot).astype(kd_scr.dtype)
    kd_scr[1, pl.ds(ATT_BLOCK, rows), :] = jnp.where(lo, krot, kn).astype(kd_scr.dtype)
    vd_scr[0, pl.ds(ATT_BLOCK, rows), :] = jnp.where(lo, v, vrot).astype(vd_scr.dtype)
    vd_scr[1, pl.ds(ATT_BLOCK, rows), :] = jnp.where(lo, vrot, v).astype(vd_scr.dtype)
    for c in range(conv_w // LANES):
        qn = _half_head_norm(q[:, c * LANES:(c + 1) * LANES], gq_ref[...], lo)
        qm_scr[2 * c] = jnp.where(lo, qn, 0.0).astype(qm_scr.dtype)
        qm_scr[2 * c + 1] = jnp.where(lo, 0.0, qn).astype(qm_scr.dtype)

    lo_b = lax.broadcasted_iota(jnp.int32, (ATT_BLOCK, LANES), 1) < HEAD_DIM
    h_next = _rms(xnext_ref[0], gmix_ref[...]).astype(jnp.bfloat16)
    next_groups = iter(range(in_w // MXU_COLS))
    ya_blocks = []
    for n in range(nblk):
        first = jnp.where(s == 0, 1, 0) if n == 0 else 0
        tiles = []
        for g in range(N_KV_HEADS):
            qs = jnp.concatenate([qm_scr[Q_PER_KV * g + j, pl.ds(n * ATT_BLOCK, ATT_BLOCK), :]
                                  for j in range(Q_PER_KV)], axis=0)
            kw = kd_scr[g, pl.ds(n * ATT_BLOCK, 2 * ATT_BLOCK), :]
            vw = vd_scr[g, pl.ds(n * ATT_BLOCK, 2 * ATT_BLOCK), :]
            sc = lax.dot_general(qs, kw, (((1,), (1,)), ((), ())), preferred_element_type=jnp.float32)
            logits = sc + bias_scr[first, g]
            ps, inv = [], []
            for j in range(Q_PER_KV):
                lj = logits[j * ATT_BLOCK:(j + 1) * ATT_BLOCK]
                sink = sinks_ref[Q_PER_KV * g + j]
                m = jnp.maximum(jnp.max(lj, axis=-1, keepdims=True), sink)
                p = jnp.exp(lj - m)
                den = jnp.sum(p, axis=-1, keepdims=True) + jnp.exp(sink - m)
                ps.append(p.astype(jnp.bfloat16))
                inv.append(1.0 / den)
            o = jnp.dot(jnp.concatenate(ps, axis=0), vw, preferred_element_type=jnp.float32)
            on = [o[j * ATT_BLOCK:(j + 1) * ATT_BLOCK] * inv[j] for j in range(Q_PER_KV)]
            tiles.append(jnp.where(lo_b, on[0], on[1]))
            tiles.append(jnp.where(lo_b, on[2], on[3]))
            j = next(next_groups, None)
            if j is not None:
                project(h_next, j)
        ya_blocks.append(jnp.concatenate(tiles, axis=-1))
    y_attn = jnp.concatenate(ya_blocks, axis=0)
    for j in next_groups:
        project(h_next, j)

    for g in range(N_KV_HEADS):
        kd_scr[g, pl.ds(0, ATT_BLOCK), :] = kd_scr[g, pl.ds(rows, ATT_BLOCK), :]
        vd_scr[g, pl.ds(0, ATT_BLOCK), :] = vd_scr[g, pl.ds(rows, ATT_BLOCK), :]

    mix = jnp.concatenate([_rms(y_conv, gc_ref[...]), _rms(y_attn, ga_ref[...])], axis=-1)
    x1 = x + jnp.dot(mix.astype(jnp.bfloat16), wout_ref[...], preferred_element_type=jnp.float32) + bout_ref[...]
    _store_row_tiles(x1_ref, x1)
    h2 = _rms(x1, gffn_ref[...])
    _store_row_tiles(h2_ref, h2)
    h_hi = h2.astype(jnp.bfloat16)
    h_lo = (h2 - h_hi.astype(jnp.float32)).astype(jnp.bfloat16)
    wr = wrt_ref[...]
    w_hi = wr.astype(jnp.bfloat16)
    w_lo = (wr - w_hi.astype(jnp.float32)).astype(jnp.bfloat16)
    nt = (((1,), (1,)), ((), ()))
    lt = (lax.dot_general(w_hi, h_hi, nt, preferred_element_type=jnp.float32)
          + lax.dot_general(w_hi, h_lo, nt, preferred_element_type=jnp.float32)
          + lax.dot_general(w_lo, h_hi, nt, preferred_element_type=jnp.float32)) + br_ref[...]
    eid = lax.broadcasted_iota(jnp.int32, lt.shape, 0)
    vals, idxs, hits = [], [], []
    for _ in range(TOP_K):
        m = jnp.max(lt, axis=0, keepdims=True)
        idx = jnp.min(jnp.where(lt == m, eid, N_EXPERTS), axis=0, keepdims=True)
        hit = eid == idx
        vals.append(m)
        idxs.append(idx)
        hits.append(hit)
        lt = jnp.where(hit, -jnp.inf, lt)
    es = [jnp.exp(vv - vals[0]) for vv in vals]
    tot = es[0] + es[1] + es[2] + es[3]
    ids_ref[0] = jnp.concatenate(idxs, axis=0)
    gates_ref[0] = jnp.concatenate([e / tot for e in es], axis=0)

    member = jnp.where(hits[0] | hits[1] | hits[2] | hits[3], 1.0, 0.0)
    before = jnp.dot(member.astype(jnp.bfloat16), tri_ref[...], preferred_element_type=jnp.float32)
    running = cnt_scr[...][:, 0:1]
    rank_e = before + running
    rank_ref[0] = jnp.concatenate([jnp.sum(jnp.where(hh, rank_e, 0.0), axis=0, keepdims=True) for hh in hits],
                                  axis=0).astype(jnp.int32)
    new_counts = cnt_scr[...] + jnp.sum(member, axis=1, keepdims=True)
    cnt_scr[...] = new_counts
    counts_ref[0] = new_counts.astype(jnp.int32)


def _mixer(x, p, bucket, chunk):
    bsz, seq, d_model = x.shape
    rows = min(MIX_ROWS, seq)
    ns = seq // rows
    tiles_per_chunk = chunk // rows
    nchunks = bsz * seq // chunk
    conv_w = d_model // 2
    in_w = p["win"].shape[1]
    tri = jnp.asarray(np.triu(np.ones((rows, rows), np.float32), 1), jnp.bfloat16)
    n_tiles = bsz * ns
    full = lambda shape: pl.BlockSpec(shape, lambda i: (0,) * len(shape))
    smem = pl.BlockSpec(memory_space=pltpu.SMEM)
    tok_spec = pl.BlockSpec((1, rows, d_model), lambda i: (i // ns, i % ns, 0))
    next_spec = pl.BlockSpec((1, rows, d_model),
                             lambda i: (jnp.minimum(i + 1, n_tiles - 1) // ns, jnp.minimum(i + 1, n_tiles - 1) % ns, 0))
    tile_spec = pl.BlockSpec((rows * SUBLANES, LANES), lambda i: (i, 0))
    sel_spec = pl.BlockSpec((1, TOP_K, rows), lambda i: (i, 0, 0))
    cnt_spec = pl.BlockSpec((1, N_EXPERTS, LANES), lambda i: (i // tiles_per_chunk, 0, 0))
    t_total = bsz * seq
    sel_shape = (n_tiles, TOP_K, rows)
    return pl.pallas_call(
        functools.partial(_mixer_kernel, ns, tiles_per_chunk),
        grid=(n_tiles,),
        in_specs=[smem, smem, tok_spec, next_spec, full((1, d_model)), full((d_model, in_w)), full((1, in_w)),
                  full((CONV_K, conv_w)), full((1, LANES)), full((1, LANES)), full((ATT_BLOCK, 2 * ATT_BLOCK)),
                  full((1, conv_w)), full((1, conv_w)), full((d_model, d_model)), full((1, d_model)),
                  full((1, d_model)), full((N_EXPERTS, d_model)), full((N_EXPERTS, 1)), full((rows, rows))],
        out_specs=[tile_spec, tile_spec, sel_spec, sel_spec, sel_spec, cnt_spec],
        out_shape=[jax.ShapeDtypeStruct((t_total * SUBLANES, LANES), jnp.float32),
                   jax.ShapeDtypeStruct((t_total * SUBLANES, LANES), jnp.float32),
                   jax.ShapeDtypeStruct(sel_shape, jnp.int32),
                   jax.ShapeDtypeStruct(sel_shape, jnp.float32),
                   jax.ShapeDtypeStruct(sel_shape, jnp.int32),
                   jax.ShapeDtypeStruct((nchunks, N_EXPERTS, LANES), jnp.int32)],
        scratch_shapes=[pltpu.VMEM((N_KV_HEADS, ATT_BLOCK + rows, LANES), jnp.bfloat16),
                        pltpu.VMEM((N_KV_HEADS, ATT_BLOCK + rows, LANES), jnp.bfloat16),
                        pltpu.VMEM((SUBLANES, conv_w), jnp.float32),
                        pltpu.VMEM((2, N_KV_HEADS, Q_PER_KV * ATT_BLOCK, 2 * ATT_BLOCK), jnp.float32),
                        pltpu.VMEM((N_Q_HEADS, rows, LANES), jnp.bfloat16),
                        pltpu.VMEM((N_EXPERTS, LANES), jnp.float32),
                        pltpu.VMEM((rows, in_w), jnp.float32)],
        compiler_params=pltpu.CompilerParams(dimension_semantics=("arbitrary",),
                                             vmem_limit_bytes=VMEM_LIMIT_BYTES),
        name="mixer",
    )(p["sinks"], p["rel_bias"], x, x, p["gmix"], p["win"], p["bin"], p["convw"], p["gq2"], p["gk2"], bucket,
      p["gc"], p["ga"], p["wout"], p["bout"], p["gffn"], p["wrt"], p["br"], tri)


def _moe_kernel(nbc_ref, bexp_ref, bfirst_ref, bslot_ref, bnext_ref, firstexp_ref,
                x1_hbm, h_hbm, dest_hbm, gate_hbm, fill_hbm, w1_hbm, w2_hbm, b1_ref, b2_ref,
                out_hbm,
                acc, hbuf, w1buf, w2buf, gbuf0, gbuf1, rbuf0, rbuf1, obuf, dest_s, gate_s, list_s,
                io_sem, w_sem, l_sem, o_sem):
    chunk = hbuf.shape[0] // SUBLANES - 1
    chunk_rows = chunk * SUBLANES
    n_assign = TOP_K * chunk
    pad_block = bexp_ref.shape[1] - 1
    d_model = w1buf.shape[1]
    d_ff = w2buf.shape[1]
    nk = d_model // LANES
    c = pl.program_id(0)
    row0 = pl.multiple_of(c * chunk_rows, SUBLANES)

    def weight_copies(e, slot):
        return (pltpu.make_async_copy(w1_hbm.at[e], w1buf.at[slot], w_sem.at[0, slot]),
                pltpu.make_async_copy(w2_hbm.at[e], w2buf.at[slot], w_sem.at[1, slot]))

    def chunk_in_copies():
        return (pltpu.make_async_copy(x1_hbm.at[pl.ds(row0, chunk_rows)], acc.at[pl.ds(0, chunk_rows)], io_sem.at[0]),
                pltpu.make_async_copy(h_hbm.at[pl.ds(row0, chunk_rows)], hbuf.at[pl.ds(0, chunk_rows)], io_sem.at[1]))

    def list_copies():
        return (pltpu.make_async_copy(dest_hbm.at[c], dest_s, l_sem.at[0]),
                pltpu.make_async_copy(gate_hbm.at[c], gate_s.at[:, pl.ds(0, n_assign)], l_sem.at[1]),
                pltpu.make_async_copy(fill_hbm, list_s, l_sem.at[2]))

    for cp in list_copies() + chunk_in_copies():
        cp.start()

    @pl.when(c == 0)
    def _first_weights():
        for cp in weight_copies(firstexp_ref[0], 0):
            cp.start()

    gate_s[0, n_assign] = jnp.float32(0.0)
    for cp in list_copies():
        cp.wait()

    def invert_body(i, carry):
        for jj in range(INVERT_UNROLL):
            a = i * INVERT_UNROLL + jj
            list_s[0, dest_s[0, a]] = a
        return carry

    lax.fori_loop(0, n_assign // INVERT_UNROLL, invert_body, 0)

    acc[pl.ds(chunk_rows, SUBLANES), :] = jnp.zeros((SUBLANES, LANES), jnp.float32)
    hbuf[pl.ds(chunk_rows, SUBLANES), :] = jnp.zeros((SUBLANES, LANES), jnp.float32)
    rbuf1[...] = jnp.zeros(rbuf1.shape, rbuf1.dtype)
    for cp in chunk_in_copies():
        cp.wait()

    def token_rows(a):
        return pl.ds(pl.multiple_of((a >> TOP_K_SHIFT) * SUBLANES, SUBLANES), SUBLANES)

    def weights_step(g):
        @pl.when(bfirst_ref[c, g] == 1)
        def _():
            slot = bslot_ref[c, g]
            for cp in weight_copies(bexp_ref[c, g], slot):
                cp.wait()
            nxt = bnext_ref[c, g]

            @pl.when(nxt >= 0)
            def _():
                for cp in weight_copies(nxt, 1 - slot):
                    cp.start()

    def gather_rows(g, gbuf, j_lo, j_hi):
        s0 = g * MOE_ROWS
        for j in range(j_lo, j_hi):
            gbuf[pl.ds(j * SUBLANES, SUBLANES), :] = hbuf[token_rows(list_s[0, s0 + j]), :]

    def scatter_rows(g, rbuf, j_lo, j_hi):
        s0 = g * MOE_ROWS
        for j0 in range(j_lo, j_hi, SCATTER_UNROLL):
            dsts, vals = [], []
            for j in range(j0, j0 + SCATTER_UNROLL):
                a = list_s[0, s0 + j]
                dst = token_rows(a)
                dsts.append(dst)
                vals.append(acc[dst, :] + gate_s[0, a] * rbuf[pl.ds(j * SUBLANES, SUBLANES), :])
            for dst, val in zip(dsts, vals):
                acc[dst, :] = val

    n_hid = 2 * d_ff // MXU_COLS
    n_out = d_model // MXU_COLS
    rows_hid = MOE_ROWS // 2 // n_hid
    rows_out = MOE_ROWS // 2 // n_out

    def stage(g_gather, gbuf_in, g_compute, gbuf, rbuf, g_scatter, rbuf_out):
        e = bexp_ref[c, g_compute]
        slot = bslot_ref[c, g_compute]
        xs = [gbuf[pl.ds(kk, MOE_ROWS, stride=SUBLANES), :] for kk in range(nk)]
        xb = jnp.concatenate(xs, axis=-1).astype(jnp.bfloat16)
        acts = []
        for cg in range(n_hid):
            cols = pl.ds(cg * MXU_COLS, MXU_COLS)
            hid = jnp.dot(xb, w1buf[slot, :, cols], preferred_element_type=jnp.float32) + b1_ref[e, :, cols]
            glu = jnp.minimum(hid[:, :LANES], SWIGLU_LIMIT)
            lin = jnp.clip(hid[:, LANES:], -SWIGLU_LIMIT, SWIGLU_LIMIT)
            acts.append(glu * jax.nn.sigmoid(SWIGLU_ALPHA * glu) * (lin + 1.0))
            gather_rows(g_gather, gbuf_in, cg * rows_hid, (cg + 1) * rows_hid)
            scatter_rows(g_scatter, rbuf_out, cg * rows_hid, (cg + 1) * rows_hid)
        act = jnp.concatenate(acts, axis=-1).astype(jnp.bfloat16)
        half = MOE_ROWS // 2
        for og in range(n_out):
            cols = pl.ds(og * MXU_COLS, MXU_COLS)
            y = jnp.dot(act, w2buf[slot, :, cols], preferred_element_type=jnp.float32) + b2_ref[e, :, cols]
            for kk in range(MXU_COLS // LANES):
                rbuf[pl.ds(og * (MXU_COLS // LANES) + kk, MOE_ROWS, stride=SUBLANES), :] = (
                    y[:, kk * LANES:(kk + 1) * LANES])
            gather_rows(g_gather, gbuf_in, half + og * rows_out, half + (og + 1) * rows_out)
            scatter_rows(g_scatter, rbuf_out, half + og * rows_out, half + (og + 1) * rows_out)

    gather_rows(0, gbuf0, 0, MOE_ROWS)

    def pair_body(p, carry):
        g0 = 2 * p
        weights_step(g0)
        stage(g0 + 1, gbuf1, g0, gbuf0, rbuf0, jnp.where(p == 0, pad_block, g0 - 1), rbuf1)
        weights_step(g0 + 1)
        stage(g0 + 2, gbuf0, g0 + 1, gbuf1, rbuf1, g0, rbuf0)
        return carry

    npairs = (nbc_ref[c] + 1) // 2
    lax.fori_loop(0, npairs, pair_body, 0)
    scatter_rows(2 * npairs - 1, rbuf1, 0, MOE_ROWS)

    def out_copy(grp, buf):
        return pltpu.make_async_copy(obuf.at[buf], out_hbm.at[pl.ds(c * chunk + grp * OUT_ROWS, OUT_ROWS)],
                                     o_sem.at[buf])

    def out_body(gp, carry):
        for buf in range(2):
            grp = 2 * gp + buf

            @pl.when(gp > 0)
            def _():
                out_copy(grp - 2, buf).wait()

            first = grp * (OUT_ROWS * SUBLANES)
            obuf[buf] = jnp.concatenate(
                [acc[pl.ds(first + kk, OUT_ROWS, stride=SUBLANES), :] for kk in range(nk)], axis=-1)
            out_copy(grp, buf).start()
        return carry

    n_groups = chunk // OUT_ROWS
    lax.fori_loop(0, n_groups // 2, out_body, 0)
    for buf in range(2):
        out_copy(n_groups - 2 + buf, buf).wait()


def _block_tables(nblk, table_blocks):
    nchunks = nblk.shape[0]
    bend = jnp.cumsum(nblk, axis=1)
    nbc = bend[:, -1]
    g = jnp.arange(table_blocks, dtype=jnp.int32)
    nonempty = nblk > 0
    last_exp = jnp.max(jnp.where(nonempty, jnp.arange(N_EXPERTS, dtype=jnp.int32)[None, :], 0), axis=1)
    bexp = jnp.sum((g[None, :, None] >= bend[:, None, :]).astype(jnp.int32), axis=-1)
    bexp = jnp.where(g[None, :] >= nbc[:, None], last_exp[:, None], bexp)
    bstart = bend - nblk
    is_exp = bexp[:, :, None] == jnp.arange(N_EXPERTS, dtype=jnp.int32)[None, None, :]
    per_block = lambda table: jnp.sum(jnp.where(is_exp, table[:, None, :], 0), axis=-1)
    bfirst = ((g[None, :] == per_block(bstart)) & (g[None, :] < nbc[:, None])).astype(jnp.int32)
    flat = nonempty.reshape(-1)
    pair_slot = ((jnp.cumsum(flat.astype(jnp.int32)) - 1) % 2).reshape(nchunks, N_EXPERTS)
    n_pairs = flat.shape[0]
    idx = jnp.where(flat, jnp.arange(n_pairs, dtype=jnp.int32), n_pairs)
    later = jnp.concatenate([lax.cummin(idx[::-1])[::-1][1:], jnp.full((1,), n_pairs, jnp.int32)])
    pair_next = jnp.where(later < n_pairs, later % N_EXPERTS, -1).reshape(nchunks, N_EXPERTS)
    bslot = per_block(pair_slot)
    bnext = per_block(pair_next)
    first_exp = (jnp.min(idx) % N_EXPERTS).reshape(1)
    return nbc.astype(jnp.int32), bexp, bfirst, bslot.astype(jnp.int32), bnext.astype(jnp.int32), first_exp


def _moe(x1r, hr, dest, gates, tables, w1, w2, b1, b2, chunk, table_blocks):
    t_total = x1r.shape[0] // SUBLANES
    nchunks = t_total // chunk
    d_model = w1.shape[1]
    d_ff = w2.shape[1]
    n_assign = TOP_K * chunk
    n_slots = table_blocks * MOE_ROWS
    fill = jnp.full((1, n_slots), n_assign, jnp.int32)
    anyspace = pl.BlockSpec(memory_space=pl.ANY)
    grid_spec = pltpu.PrefetchScalarGridSpec(
        num_scalar_prefetch=6,
        grid=(nchunks,),
        in_specs=[anyspace] * 7 + [pl.BlockSpec((N_EXPERTS, 1, 2 * d_ff), lambda c, *_: (0, 0, 0)),
                                   pl.BlockSpec((N_EXPERTS, 1, d_model), lambda c, *_: (0, 0, 0))],
        out_specs=anyspace,
        scratch_shapes=[pltpu.VMEM(((chunk + 1) * SUBLANES, LANES), jnp.float32),
                        pltpu.VMEM(((chunk + 1) * SUBLANES, LANES), jnp.float32),
                        pltpu.VMEM((2, d_model, 2 * d_ff), jnp.bfloat16),
                        pltpu.VMEM((2, d_ff, d_model), jnp.bfloat16),
                        pltpu.VMEM((MOE_ROWS * SUBLANES, LANES), jnp.float32),
                        pltpu.VMEM((MOE_ROWS * SUBLANES, LANES), jnp.float32),
                        pltpu.VMEM((MOE_ROWS * SUBLANES, LANES), jnp.float32),
                        pltpu.VMEM((MOE_ROWS * SUBLANES, LANES), jnp.float32),
                        pltpu.VMEM((2, OUT_ROWS, d_model), jnp.float32),
                        pltpu.SMEM((1, n_assign), jnp.int32),
                        pltpu.SMEM((1, n_assign + LANES), jnp.float32),
                        pltpu.SMEM((1, n_slots), jnp.int32),
                        pltpu.SemaphoreType.DMA((2,)),
                        pltpu.SemaphoreType.DMA((2, 2)),
                        pltpu.SemaphoreType.DMA((3,)),
                        pltpu.SemaphoreType.DMA((2,))],
    )
    return pl.pallas_call(
        _moe_kernel,
        grid_spec=grid_spec,
        out_shape=jax.ShapeDtypeStruct((t_total, d_model), jnp.float32),
        compiler_params=pltpu.CompilerParams(dimension_semantics=("arbitrary",),
                                             vmem_limit_bytes=VMEM_LIMIT_BYTES),
        name="moe",
    )(*tables, x1r, hr, dest, gates, fill, w1, w2, b1, b2)


def _t5_bucket_table():
    i = np.arange(ATT_BLOCK)[:, None]
    j = np.arange(2 * ATT_BLOCK)[None, :]
    rel = i + ATT_BLOCK - j
    max_exact = N_BUCKETS // 2
    nf = np.maximum(rel, 1).astype(np.float32)
    large = max_exact + (np.log(nf / max_exact) / np.float32(math.log(MAX_DISTANCE / max_exact))
                         * (N_BUCKETS - max_exact)).astype(np.int32)
    large = np.minimum(large, N_BUCKETS - 1)
    bucket = np.where(rel < max_exact, rel, large)
    return np.where((rel >= 0) & (rel < WINDOW), bucket, -1).astype(np.int32)


def _token_major(a, nchunks):
    return jnp.transpose(a, (0, 2, 1)).reshape(nchunks, 1, -1)


def kernel(x, norm_mix, w_in, b_in, conv_w, q_norm, k_norm, sinks, rel_bias, out_norm_conv, out_norm_attn,
           w_out, b_out, norm_ffn, w_router, b_router, w1, b1, w2, b2):
    bsz, seq, d_model = x.shape
    depth = w_in.shape[0]
    bf16 = jnp.bfloat16
    bucket = jnp.asarray(_t5_bucket_table())
    t_total = bsz * seq
    chunk = min(MOE_CHUNK, t_total)
    nchunks = t_total // chunk
    table_blocks = chunk * TOP_K // MOE_ROWS + N_EXPERTS + 2
    for l in range(depth):
        d_ff = w2.shape[2]
        params = dict(
            sinks=sinks[l], rel_bias=rel_bias, gmix=norm_mix[l][None], win=w_in[l].astype(bf16), bin=b_in[l][None],
            convw=conv_w[l], gq2=jnp.tile(q_norm[l], 2)[None] * (HEAD_DIM ** -0.5), gk2=jnp.tile(k_norm[l], 2)[None],
            gc=out_norm_conv[l][None], ga=out_norm_attn[l][None], wout=w_out[l].astype(bf16), bout=b_out[l][None],
            gffn=norm_ffn[l][None], wrt=w_router[l].T, br=b_router[l][:, None])
        x1r, hr, ids_t, gates_t, rank_t, counts_t = _mixer(x, params, bucket, chunk)
        counts = counts_t[:, :, 0]
        padded = (counts + MOE_ROWS - 1) // MOE_ROWS * MOE_ROWS
        pad_starts = jnp.cumsum(padded, axis=1) - padded
        tables = _block_tables(padded // MOE_ROWS, table_blocks)
        tiles_per_chunk = ids_t.shape[0] // nchunks
        tile_starts = jnp.repeat(pad_starts, tiles_per_chunk, axis=0)
        dest_t = rank_t
        for e in range(N_EXPERTS):
            dest_t = dest_t + jnp.where(ids_t == e, tile_starts[:, e:e + 1, None], 0)
        w1p, w2p = _wprep(w1[l], w2[l])
        b1p = jnp.transpose(b1[l].reshape(N_EXPERTS, 2 * d_ff // MXU_COLS, LANES, 2), (0, 1, 3, 2))
        b1p = b1p.reshape(N_EXPERTS, 1, 2 * d_ff)
        out = _moe(x1r, hr, _token_major(dest_t, nchunks), _token_major(gates_t, nchunks), tables, w1p, w2p, b1p,
                   b2[l][:, None, :], chunk, table_blocks)
        x = out.reshape(bsz, seq, d_model)
    return x
```

```python
import functools
import math

import jax
import jax.numpy as jnp
import numpy as np
from jax import lax
from jax.experimental import pallas as pl
from jax.experimental.pallas import tpu as pltpu

HEAD_DIM = 64
N_Q_HEADS = 8
N_KV_HEADS = 2
Q_PER_KV = N_Q_HEADS // N_KV_HEADS
CONV_K = 3
WINDOW = 128
ATT_BLOCK = 128
N_BUCKETS = 32
MAX_DISTANCE = 128
N_EXPERTS = 32
TOP_K = 4
TOP_K_SHIFT = 2
SWIGLU_LIMIT = 7.0
SWIGLU_ALPHA = 1.702
EPS = 1e-5
MASK_VALUE = -1e30

LANES = 128
SUBLANES = 8
MXU_COLS = 256
VMEM_LIMIT_BYTES = 56 * 1024 * 1024

MIX_ROWS = 512
MOE_CHUNK = 4096
MOE_ROWS = 128
WEIGHT_SLOTS = 3
MOE_COLS = 256
OUT_ROWS = 128
PREP_ROWS = 256
INVERT_UNROLL = 16
SCATTER_UNROLL = 4


def _rms(x, g):
    return x * lax.rsqrt(jnp.mean(x * x, axis=-1, keepdims=True) + EPS) * g


def _half_head_norm(t, gain2, lo):
    t2 = t * t
    s_lo = jnp.sum(jnp.where(lo, t2, 0.0), axis=-1, keepdims=True)
    s_hi = jnp.sum(jnp.where(lo, 0.0, t2), axis=-1, keepdims=True)
    r = jnp.where(lo, lax.rsqrt(s_lo * (1.0 / HEAD_DIM) + EPS), lax.rsqrt(s_hi * (1.0 / HEAD_DIM) + EPS))
    return t * r * gain2


def _store_row_tiles(ref, val):
    rows = val.shape[0]
    for kk in range(val.shape[1] // LANES):
        ref[pl.ds(kk, rows, stride=SUBLANES), :] = val[:, kk * LANES:(kk + 1) * LANES]


def _wprep_kernel(w1_ref, w2_ref, perm_ref, w1o_ref, w2o_ref):
    perm = perm_ref[...]
    for c in range(w1_ref.shape[2] // MXU_COLS):
        cols = pl.ds(c * MXU_COLS, MXU_COLS)
        t = w1_ref[0, :, cols].astype(jnp.bfloat16)
        w1o_ref[0, :, cols] = jnp.dot(t, perm, preferred_element_type=jnp.float32).astype(jnp.bfloat16)
    w2o_ref[0] = w2_ref[0].astype(jnp.bfloat16)


def _wprep(w1, w2):
    n_e, d_model, two_ff = w1.shape
    d_ff, d_out = w2.shape[1], w2.shape[2]
    assert d_model == d_ff, "one row grid serves both weight tensors"
    j = np.arange(MXU_COLS)
    src = np.where(j < LANES, 2 * j, 2 * (j - LANES) + 1)
    perm = jnp.asarray((np.arange(MXU_COLS)[:, None] == src[None, :]).astype(np.float32), jnp.bfloat16)
    return pl.pallas_call(
        _wprep_kernel,
        grid=(n_e, d_model // PREP_ROWS),
        in_specs=[pl.BlockSpec((1, PREP_ROWS, two_ff), lambda e, r: (e, r, 0)),
                  pl.BlockSpec((1, PREP_ROWS, d_out), lambda e, r: (e, r, 0)),
                  pl.BlockSpec((MXU_COLS, MXU_COLS), lambda e, r: (0, 0))],
        out_specs=[pl.BlockSpec((1, PREP_ROWS, two_ff), lambda e, r: (e, r, 0)),
                   pl.BlockSpec((1, PREP_ROWS, d_out), lambda e, r: (e, r, 0))],
        out_shape=[jax.ShapeDtypeStruct(w1.shape, jnp.bfloat16), jax.ShapeDtypeStruct(w2.shape, jnp.bfloat16)],
        compiler_params=pltpu.CompilerParams(dimension_semantics=("arbitrary", "arbitrary")),
        name="wprep",
    )(w1, w2, perm)


def _mixer_kernel(tiles_per_seq, tiles_per_chunk,
                  sinks_ref, relb_ref,
                  x_ref, xnext_ref, gmix_ref, win_ref, bin_ref, convw_ref, gq_ref, gk_ref, bucket_ref,
                  gc_ref, ga_ref, wout_ref, bout_ref, gffn_ref, wrt_ref, br_ref, tri_ref,
                  x1_ref, h2_ref, ids_ref, gates_ref, rank_ref, counts_ref,
                  kd_scr, vd_scr, cu_scr, bias_scr, qm_scr, cnt_scr, proj_scr):
    rows = x_ref.shape[1]
    d_model = x_ref.shape[2]
    conv_w = d_model // 2
    in_w = win_ref.shape[1]
    nblk = rows // ATT_BLOCK
    tile = pl.program_id(0)
    s = lax.rem(tile, tiles_per_seq)

    def project(h, j):
        cols = pl.ds(j * MXU_COLS, MXU_COLS)
        proj_scr[:, cols] = jnp.dot(h, win_ref[:, cols], preferred_element_type=jnp.float32) + bin_ref[:, cols]

    @pl.when(tile == 0)
    def _first_projection():
        h0 = _rms(x_ref[0], gmix_ref[...]).astype(jnp.bfloat16)
        for j in range(in_w // MXU_COLS):
            project(h0, j)

    @pl.when(tile == 0)
    def _build_bias():
        bk = bucket_ref[...]
        col = lax.broadcasted_iota(jnp.int32, bk.shape, 1)
        accs = [jnp.full(bk.shape, MASK_VALUE, jnp.float32) for _ in range(N_Q_HEADS)]
        for bb in range(N_BUCKETS):
            hit = bk == bb
            for h in range(N_Q_HEADS):
                accs[h] = jnp.where(hit, relb_ref[bb, h], accs[h])
        for h in range(N_Q_HEADS):
            g, j = divmod(h, Q_PER_KV)
            bias_scr[0, g, pl.ds(j * ATT_BLOCK, ATT_BLOCK), :] = accs[h]
            bias_scr[1, g, pl.ds(j * ATT_BLOCK, ATT_BLOCK), :] = jnp.where(col < ATT_BLOCK, MASK_VALUE, accs[h])

    @pl.when(s == 0)
    def _reset_state():
        kd_scr[:, pl.ds(0, ATT_BLOCK), :] = jnp.zeros((N_KV_HEADS, ATT_BLOCK, LANES), kd_scr.dtype)
        vd_scr[:, pl.ds(0, ATT_BLOCK), :] = jnp.zeros((N_KV_HEADS, ATT_BLOCK, LANES), vd_scr.dtype)
        cu_scr[...] = jnp.zeros(cu_scr.shape, cu_scr.dtype)

    @pl.when(lax.rem(tile, tiles_per_chunk) == 0)
    def _reset_counts():
        cnt_scr[...] = jnp.zeros(cnt_scr.shape, cnt_scr.dtype)

    b_gate = proj_scr[:, 0:conv_w]
    c_gate = proj_scr[:, conv_w:2 * conv_w]
    u = proj_scr[:, 2 * conv_w:3 * conv_w]
    q = proj_scr[:, 3 * conv_w:4 * conv_w]
    k = proj_scr[:, 4 * conv_w:4 * conv_w + LANES]
    v = proj_scr[:, 4 * conv_w + LANES:4 * conv_w + 2 * LANES]

    cu = c_gate * u
    row = lax.broadcasted_iota(jnp.int32, cu.shape, 0)
    prev2 = cu_scr[SUBLANES - 2:SUBLANES - 1, :]
    prev1 = cu_scr[SUBLANES - 1:SUBLANES, :]
    r1 = jnp.where(row == 0, prev1, pltpu.roll(cu, 1, 0))
    r2 = jnp.where(row == 0, prev2, jnp.where(row == 1, prev1, pltpu.roll(cu, 2, 0)))
    y_conv = b_gate * (convw_ref[2:3, :] * cu + convw_ref[1:2, :] * r1 + convw_ref[0:1, :] * r2)
    cu_scr[...] = cu[rows - SUBLANES:rows, :]

    lane = lax.broadcasted_iota(jnp.int32, (rows, LANES), 1)
    lo = lane < HEAD_DIM
    kn = _half_head_norm(k, gk_ref[...], lo)
    krot = pltpu.roll(kn, HEAD_DIM, 1)
    vrot = pltpu.roll(v, HEAD_DIM, 1)
    kd_scr[0, pl.ds(ATT_BLOCK, rows), :] = jnp.where(lo, kn, krot).astype(kd_scr.dtype)
    kd_scr[1, pl.ds(ATT_BLOCK, rows), :] = jnp.where(lo, krot, kn).astype(kd_scr.dtype)
    vd_scr[0, pl.ds(ATT_BLOCK, rows), :] = jnp.where(lo, v, vrot).astype(vd_scr.dtype)
    vd_scr[1, pl.ds(ATT_BLOCK, rows), :] = jnp.where(lo, vrot, v).astype(vd_scr.dtype)
    for c in range(conv_w // LANES):
        qn = _half_head_norm(q[:, c * LANES:(c + 1) * LANES], gq_ref[...], lo)
        qm_scr[2 * c] = jnp.where(lo, qn, 0.0).astype(qm_scr.dtype)
        qm_scr[2 * c + 1] = jnp.where(lo, 0.0, qn).astype(qm_scr.dtype)

    lo_b = lax.broadcasted_iota(jnp.int32, (ATT_BLOCK, LANES), 1) < HEAD_DIM
    h_next = _rms(xnext_ref[0], gmix_ref[...]).astype(jnp.bfloat16)
    next_groups = iter(range(in_w // MXU_COLS))
    ya_blocks = []
    for n in range(nblk):
        first = jnp.where(s == 0, 1, 0) if n == 0 else 0
        tiles = []
        for g in range(N_KV_HEADS):
            qs = jnp.concatenate([qm_scr[Q_PER_KV * g + j, pl.ds(n * ATT_BLOCK, ATT_BLOCK), :]
                                  for j in range(Q_PER_KV)], axis=0)
            kw = kd_scr[g, pl.ds(n * ATT_BLOCK, 2 * ATT_BLOCK), :]
            vw = vd_scr[g, pl.ds(n * ATT_BLOCK, 2 * ATT_BLOCK), :]
            sc = lax.dot_general(qs, kw, (((1,), (1,)), ((), ())), preferred_element_type=jnp.float32)
            logits = sc + bias_scr[first, g]
            ps, inv = [], []
            for j in range(Q_PER_KV):
                lj = logits[j * ATT_BLOCK:(j + 1) * ATT_BLOCK]
                sink = sinks_ref[Q_PER_KV * g + j]
                m = jnp.maximum(jnp.max(lj, axis=-1, keepdims=True), sink)
                p = jnp.exp(lj - m)
                den = jnp.sum(p, axis=-1, keepdims=True) + jnp.exp(sink - m)
                ps.append(p.astype(jnp.bfloat16))
                inv.append(1.0 / den)
            o = jnp.dot(jnp.concatenate(ps, axis=0), vw, preferred_element_type=jnp.float32)
            on = [o[j * ATT_BLOCK:(j + 1) * ATT_BLOCK] * inv[j] for j in range(Q_PER_KV)]
            tiles.append(jnp.where(lo_b, on[0], on[1]))
            tiles.append(jnp.where(lo_b, on[2], on[3]))
            j = next(next_groups, None)
            if j is not None:
                project(h_next, j)
        ya_blocks.append(jnp.concatenate(tiles, axis=-1))
    y_attn = jnp.concatenate(ya_blocks, axis=0)
    for j in next_groups:
        project(h_next, j)

    for g in range(N_KV_HEADS):
        kd_scr[g, pl.ds(0, ATT_BLOCK), :] = kd_scr[g, pl.ds(rows, ATT_BLOCK), :]
        vd_scr[g, pl.ds(0, ATT_BLOCK), :] = vd_scr[g, pl.ds(rows, ATT_BLOCK), :]

    mix = jnp.concatenate([_rms(y_conv, gc_ref[...]), _rms(y_attn, ga_ref[...])], axis=-1)
    x1 = (x_ref[0] + jnp.dot(mix.astype(jnp.bfloat16), wout_ref[...], preferred_element_type=jnp.float32)
          + bout_ref[...])
    _store_row_tiles(x1_ref, x1)
    h2 = _rms(x1, gffn_ref[...])
    _store_row_tiles(h2_ref, h2)
    h_hi = h2.astype(jnp.bfloat16)
    h_lo = (h2 - h_hi.astype(jnp.float32)).astype(jnp.bfloat16)
    wr = wrt_ref[...]
    w_hi = wr.astype(jnp.bfloat16)
    w_lo = (wr - w_hi.astype(jnp.float32)).astype(jnp.bfloat16)
    nt = (((1,), (1,)), ((), ()))
    lt = (lax.dot_general(w_hi, h_hi, nt, preferred_element_type=jnp.float32)
          + lax.dot_general(w_hi, h_lo, nt, preferred_element_type=jnp.float32)
          + lax.dot_general(w_lo, h_hi, nt, preferred_element_type=jnp.float32)) + br_ref[...]
    eid = lax.broadcasted_iota(jnp.int32, lt.shape, 0)
    vals, idxs, hits = [], [], []
    for _ in range(TOP_K):
        m = jnp.max(lt, axis=0, keepdims=True)
        idx = jnp.min(jnp.where(lt == m, eid, N_EXPERTS), axis=0, keepdims=True)
        hit = eid == idx
        vals.append(m)
        idxs.append(idx)
        hits.append(hit)
        lt = jnp.where(hit, -jnp.inf, lt)
    es = [jnp.exp(vv - vals[0]) for vv in vals]
    tot = es[0] + es[1] + es[2] + es[3]
    ids_ref[0] = jnp.concatenate(idxs, axis=0)
    gates_ref[0] = jnp.concatenate([e / tot for e in es], axis=0)

    member = jnp.where(hits[0] | hits[1] | hits[2] | hits[3], 1.0, 0.0)
    before = jnp.dot(member.astype(jnp.bfloat16), tri_ref[...], preferred_element_type=jnp.float32)
    running = cnt_scr[...][:, 0:1]
    rank_e = before + running
    rank_ref[0] = jnp.concatenate([jnp.sum(jnp.where(hh, rank_e, 0.0), axis=0, keepdims=True) for hh in hits],
                                  axis=0).astype(jnp.int32)
    new_counts = cnt_scr[...] + jnp.sum(member, axis=1, keepdims=True)
    cnt_scr[...] = new_counts
    counts_ref[0] = new_counts.astype(jnp.int32)


def _mixer(x, p, bucket, chunk):
    bsz, seq, d_model = x.shape
    rows = min(MIX_ROWS, seq)
    ns = seq // rows
    tiles_per_chunk = chunk // rows
    nchunks = bsz * seq // chunk
    conv_w = d_model // 2
    in_w = p["win"].shape[1]
    tri = jnp.asarray(np.triu(np.ones((rows, rows), np.float32), 1), jnp.bfloat16)
    n_tiles = bsz * ns
    full = lambda shape: pl.BlockSpec(shape, lambda i: (0,) * len(shape))
    smem = pl.BlockSpec(memory_space=pltpu.SMEM)
    tok_spec = pl.BlockSpec((1, rows, d_model), lambda i: (i // ns, i % ns, 0))
    next_spec = pl.BlockSpec((1, rows, d_model),
                             lambda i: (jnp.minimum(i + 1, n_tiles - 1) // ns, jnp.minimum(i + 1, n_tiles - 1) % ns, 0))
    tile_spec = pl.BlockSpec((rows * SUBLANES, LANES), lambda i: (i, 0))
    sel_spec = pl.BlockSpec((1, TOP_K, rows), lambda i: (i, 0, 0))
    cnt_spec = pl.BlockSpec((1, N_EXPERTS, LANES), lambda i: (i // tiles_per_chunk, 0, 0))
    t_total = bsz * seq
    sel_shape = (n_tiles, TOP_K, rows)
    return pl.pallas_call(
        functools.partial(_mixer_kernel, ns, tiles_per_chunk),
        grid=(n_tiles,),
        in_specs=[smem, smem, tok_spec, next_spec, full((1, d_model)), full((d_model, in_w)), full((1, in_w)),
                  full((CONV_K, conv_w)), full((1, LANES)), full((1, LANES)), full((ATT_BLOCK, 2 * ATT_BLOCK)),
                  full((1, conv_w)), full((1, conv_w)), full((d_model, d_model)), full((1, d_model)),
                  full((1, d_model)), full((N_EXPERTS, d_model)), full((N_EXPERTS, 1)), full((rows, rows))],
        out_specs=[tile_spec, tile_spec, sel_spec, sel_spec, sel_spec, cnt_spec],
        out_shape=[jax.ShapeDtypeStruct((t_total * SUBLANES, LANES), jnp.float32),
                   jax.ShapeDtypeStruct((t_total * SUBLANES, LANES), jnp.float32),
                   jax.ShapeDtypeStruct(sel_shape, jnp.int32),
                   jax.ShapeDtypeStruct(sel_shape, jnp.float32),
                   jax.ShapeDtypeStruct(sel_shape, jnp.int32),
                   jax.ShapeDtypeStruct((nchunks, N_EXPERTS, LANES), jnp.int32)],
        scratch_shapes=[pltpu.VMEM((N_KV_HEADS, ATT_BLOCK + rows, LANES), jnp.bfloat16),
                        pltpu.VMEM((N_KV_HEADS, ATT_BLOCK + rows, LANES), jnp.bfloat16),
                        pltpu.VMEM((SUBLANES, conv_w), jnp.float32),
                        pltpu.VMEM((2, N_KV_HEADS, Q_PER_KV * ATT_BLOCK, 2 * ATT_BLOCK), jnp.float32),
                        pltpu.VMEM((N_Q_HEADS, rows, LANES), jnp.bfloat16),
                        pltpu.VMEM((N_EXPERTS, LANES), jnp.float32),
                        pltpu.VMEM((rows, in_w), jnp.float32)],
        compiler_params=pltpu.CompilerParams(dimension_semantics=("arbitrary",),
                                             vmem_limit_bytes=VMEM_LIMIT_BYTES),
        name="mixer",
    )(p["sinks"], p["rel_bias"], x, x, p["gmix"], p["win"], p["bin"], p["convw"], p["gq2"], p["gk2"], bucket,
      p["gc"], p["ga"], p["wout"], p["bout"], p["gffn"], p["wrt"], p["br"], tri)


def _moe_kernel(nbc_ref, bexp_ref, bfirst_ref, bslot_ref, bnext_ref, firstexp_ref,
                x1_hbm, h_hbm, dest_hbm, gate_hbm, fill_hbm, w1_hbm, w2_hbm, b1_ref, b2_ref,
                out_hbm,
                acc, hbuf, w1buf, w2buf, gbuf0, gbuf1, rbuf0, rbuf1, obuf, dest_s, gate_s, list_s,
                io_sem, w_sem, l_sem, o_sem):
    chunk = hbuf.shape[0] // SUBLANES - 1
    chunk_rows = chunk * SUBLANES
    n_assign = TOP_K * chunk
    pad_block = bexp_ref.shape[1] - 1
    d_model = w1buf.shape[1]
    d_ff = w2buf.shape[1]
    nk = d_model // LANES
    c = pl.program_id(0)
    row0 = pl.multiple_of(c * chunk_rows, SUBLANES)

    def weight_copies(e, slot):
        return (pltpu.make_async_copy(w1_hbm.at[e], w1buf.at[slot], w_sem.at[0, slot]),
                pltpu.make_async_copy(w2_hbm.at[e], w2buf.at[slot], w_sem.at[1, slot]))

    def chunk_in_copies():
        return (pltpu.make_async_copy(x1_hbm.at[pl.ds(row0, chunk_rows)], acc.at[pl.ds(0, chunk_rows)], io_sem.at[0]),
                pltpu.make_async_copy(h_hbm.at[pl.ds(row0, chunk_rows)], hbuf.at[pl.ds(0, chunk_rows)], io_sem.at[1]))

    def list_copies():
        return (pltpu.make_async_copy(dest_hbm.at[c], dest_s, l_sem.at[0]),
                pltpu.make_async_copy(gate_hbm.at[c], gate_s.at[:, pl.ds(0, n_assign)], l_sem.at[1]),
                pltpu.make_async_copy(fill_hbm, list_s, l_sem.at[2]))

    for cp in list_copies() + chunk_in_copies():
        cp.start()

    @pl.when(c == 0)
    def _first_weights():
        for cp in weight_copies(firstexp_ref[0], 0):
            cp.start()

    gate_s[0, n_assign] = jnp.float32(0.0)
    for cp in list_copies():
        cp.wait()

    def invert_body(i, carry):
        for jj in range(INVERT_UNROLL):
            a = i * INVERT_UNROLL + jj
            list_s[0, dest_s[0, a]] = a
        return carry

    lax.fori_loop(0, n_assign // INVERT_UNROLL, invert_body, 0)

    acc[pl.ds(chunk_rows, SUBLANES), :] = jnp.zeros((SUBLANES, LANES), jnp.float32)
    hbuf[pl.ds(chunk_rows, SUBLANES), :] = jnp.zeros((SUBLANES, LANES), jnp.float32)
    rbuf1[...] = jnp.zeros(rbuf1.shape, rbuf1.dtype)
    for cp in chunk_in_copies():
        cp.wait()

    def token_rows(a):
        return pl.ds(pl.multiple_of((a >> TOP_K_SHIFT) * SUBLANES, SUBLANES), SUBLANES)

    def weights_step(g):
        @pl.when(bfirst_ref[c, g] == 1)
        def _():
            slot = bslot_ref[c, g]
            for cp in weight_copies(bexp_ref[c, g], slot):
                cp.wait()
            nxt = bnext_ref[c, g]

            @pl.when(nxt >= 0)
            def _():
                for cp in weight_copies(nxt, jnp.where(slot == WEIGHT_SLOTS - 1, 0, slot + 1)):
                    cp.start()

    def gather_rows(g, gbuf, j_lo, j_hi):
        s0 = g * MOE_ROWS
        for j in range(j_lo, j_hi):
            gbuf[pl.ds(j * SUBLANES, SUBLANES), :] = hbuf[token_rows(list_s[0, s0 + j]), :]

    def scatter_rows(g, rbuf, j_lo, j_hi):
        s0 = g * MOE_ROWS
        for j0 in range(j_lo, j_hi, SCATTER_UNROLL):
            dsts, vals = [], []
            for j in range(j0, j0 + SCATTER_UNROLL):
                a = list_s[0, s0 + j]
                dst = token_rows(a)
                dsts.append(dst)
                vals.append(acc[dst, :] + gate_s[0, a] * rbuf[pl.ds(j * SUBLANES, SUBLANES), :])
            for dst, val in zip(dsts, vals):
                acc[dst, :] = val

    n_hid = 2 * d_ff // MOE_COLS
    n_out = d_model // MOE_COLS
    rows_hid = MOE_ROWS // 2 // n_hid
    rows_out = MOE_ROWS // 2 // n_out

    def stage(g_gather, gbuf_in, g_compute, gbuf, rbuf, g_scatter, rbuf_out):
        e = bexp_ref[c, g_compute]
        slot = bslot_ref[c, g_compute]
        xs = [gbuf[pl.ds(kk, MOE_ROWS, stride=SUBLANES), :] for kk in range(nk)]
        xb = jnp.concatenate(xs, axis=-1).astype(jnp.bfloat16)
        acts = []
        for cg in range(n_hid):
            cols = pl.ds(cg * MOE_COLS, MOE_COLS)
            hid = jnp.dot(xb, w1buf[slot, :, cols], preferred_element_type=jnp.float32) + b1_ref[e, :, cols]
            for pg in range(MOE_COLS // MXU_COLS):
                glu = jnp.minimum(hid[:, pg * MXU_COLS:pg * MXU_COLS + LANES], SWIGLU_LIMIT)
                lin = jnp.clip(hid[:, pg * MXU_COLS + LANES:(pg + 1) * MXU_COLS], -SWIGLU_LIMIT, SWIGLU_LIMIT)
                acts.append(glu * jax.nn.sigmoid(SWIGLU_ALPHA * glu) * (lin + 1.0))
            gather_rows(g_gather, gbuf_in, cg * rows_hid, (cg + 1) * rows_hid)
            scatter_rows(g_scatter, rbuf_out, cg * rows_hid, (cg + 1) * rows_hid)
        act = jnp.concatenate(acts, axis=-1).astype(jnp.bfloat16)
        half = MOE_ROWS // 2
        for og in range(n_out):
            cols = pl.ds(og * MOE_COLS, MOE_COLS)
            y = jnp.dot(act, w2buf[slot, :, cols], preferred_element_type=jnp.float32) + b2_ref[e, :, cols]
            for kk in range(MOE_COLS // LANES):
                rbuf[pl.ds(og * (MOE_COLS // LANES) + kk, MOE_ROWS, stride=SUBLANES), :] = (
                    y[:, kk * LANES:(kk + 1) * LANES])
            gather_rows(g_gather, gbuf_in, half + og * rows_out, half + (og + 1) * rows_out)
            scatter_rows(g_scatter, rbuf_out, half + og * rows_out, half + (og + 1) * rows_out)

    gather_rows(0, gbuf0, 0, MOE_ROWS)

    def pair_body(p, carry):
        g0 = 2 * p
        weights_step(g0)
        weights_step(g0 + 1)
        stage(g0 + 1, gbuf1, g0, gbuf0, rbuf0, jnp.where(p == 0, pad_block, g0 - 1), rbuf1)
        stage(g0 + 2, gbuf0, g0 + 1, gbuf1, rbuf1, g0, rbuf0)
        return carry

    npairs = (nbc_ref[c] + 1) // 2
    lax.fori_loop(0, npairs, pair_body, 0)
    scatter_rows(2 * npairs - 1, rbuf1, 0, MOE_ROWS)

    def out_copy(grp, buf):
        return pltpu.make_async_copy(obuf.at[buf], out_hbm.at[pl.ds(c * chunk + grp * OUT_ROWS, OUT_ROWS)],
                                     o_sem.at[buf])

    def out_body(gp, carry):
        for buf in range(2):
            grp = 2 * gp + buf

            @pl.when(gp > 0)
            def _():
                out_copy(grp - 2, buf).wait()

            first = grp * (OUT_ROWS * SUBLANES)
            obuf[buf] = jnp.concatenate(
                [acc[pl.ds(first + kk, OUT_ROWS, stride=SUBLANES), :] for kk in range(nk)], axis=-1)
            out_copy(grp, buf).start()
        return carry

    n_groups = chunk // OUT_ROWS
    lax.fori_loop(0, n_groups // 2, out_body, 0)
    for buf in range(2):
        out_copy(n_groups - 2 + buf, buf).wait()


def _block_tables(nblk, table_blocks):
    nchunks = nblk.shape[0]
    bend = jnp.cumsum(nblk, axis=1)
    nbc = bend[:, -1]
    g = jnp.arange(table_blocks, dtype=jnp.int32)
    nonempty = nblk > 0
    last_exp = jnp.max(jnp.where(nonempty, jnp.arange(N_EXPERTS, dtype=jnp.int32)[None, :], 0), axis=1)
    bexp = jnp.sum((g[None, :, None] >= bend[:, None, :]).astype(jnp.int32), axis=-1)
    bexp = jnp.where(g[None, :] >= nbc[:, None], last_exp[:, None], bexp)
    bstart = bend - nblk
    is_exp = bexp[:, :, None] == jnp.arange(N_EXPERTS, dtype=jnp.int32)[None, None, :]
    per_block = lambda table: jnp.sum(jnp.where(is_exp, table[:, None, :], 0), axis=-1)
    bfirst = ((g[None, :] == per_block(bstart)) & (g[None, :] < nbc[:, None])).astype(jnp.int32)
    flat = nonempty.reshape(-1)
    pair_slot = ((jnp.cumsum(flat.astype(jnp.int32)) - 1) % WEIGHT_SLOTS).reshape(nchunks, N_EXPERTS)
    n_pairs = flat.shape[0]
    idx = jnp.where(flat, jnp.arange(n_pairs, dtype=jnp.int32), n_pairs)
    later = jnp.concatenate([lax.cummin(idx[::-1])[::-1][1:], jnp.full((1,), n_pairs, jnp.int32)])
    pair_next = jnp.where(later < n_pairs, later % N_EXPERTS, -1).reshape(nchunks, N_EXPERTS)
    bslot = per_block(pair_slot)
    bnext = per_block(pair_next)
    first_exp = (jnp.min(idx) % N_EXPERTS).reshape(1)
    return nbc.astype(jnp.int32), bexp, bfirst, bslot.astype(jnp.int32), bnext.astype(jnp.int32), first_exp


def _moe(x1r, hr, dest, gates, tables, w1, w2, b1, b2, chunk, table_blocks):
    t_total = x1r.shape[0] // SUBLANES
    nchunks = t_total // chunk
    d_model = w1.shape[1]
    d_ff = w2.shape[1]
    n_assign = TOP_K * chunk
    n_slots = table_blocks * MOE_ROWS
    fill = jnp.full((1, n_slots), n_assign, jnp.int32)
    anyspace = pl.BlockSpec(memory_space=pl.ANY)
    grid_spec = pltpu.PrefetchScalarGridSpec(
        num_scalar_prefetch=6,
        grid=(nchunks,),
        in_specs=[anyspace] * 7 + [pl.BlockSpec((N_EXPERTS, 1, 2 * d_ff), lambda c, *_: (0, 0, 0)),
                                   pl.BlockSpec((N_EXPERTS, 1, d_model), lambda c, *_: (0, 0, 0))],
        out_specs=anyspace,
        scratch_shapes=[pltpu.VMEM(((chunk + 1) * SUBLANES, LANES), jnp.float32),
                        pltpu.VMEM(((chunk + 1) * SUBLANES, LANES), jnp.float32),
                        pltpu.VMEM((WEIGHT_SLOTS, d_model, 2 * d_ff), jnp.bfloat16),
                        pltpu.VMEM((WEIGHT_SLOTS, d_ff, d_model), jnp.bfloat16),
                        pltpu.VMEM((MOE_ROWS * SUBLANES, LANES), jnp.float32),
                        pltpu.VMEM((MOE_ROWS * SUBLANES, LANES), jnp.float32),
                        pltpu.VMEM((MOE_ROWS * SUBLANES, LANES), jnp.float32),
                        pltpu.VMEM((MOE_ROWS * SUBLANES, LANES), jnp.float32),
                        pltpu.VMEM((2, OUT_ROWS, d_model), jnp.float32),
                        pltpu.SMEM((1, n_assign), jnp.int32),
                        pltpu.SMEM((1, n_assign + LANES), jnp.float32),
                        pltpu.SMEM((1, n_slots), jnp.int32),
                        pltpu.SemaphoreType.DMA((2,)),
                        pltpu.SemaphoreType.DMA((2, WEIGHT_SLOTS)),
                        pltpu.SemaphoreType.DMA((3,)),
                        pltpu.SemaphoreType.DMA((2,))],
    )
    return pl.pallas_call(
        _moe_kernel,
        grid_spec=grid_spec,
        out_shape=jax.ShapeDtypeStruct((t_total, d_model), jnp.float32),
        compiler_params=pltpu.CompilerParams(dimension_semantics=("arbitrary",),
                                             vmem_limit_bytes=VMEM_LIMIT_BYTES),
        name="moe",
    )(*tables, x1r, hr, dest, gates, fill, w1, w2, b1, b2)


def _t5_bucket_table():
    i = np.arange(ATT_BLOCK)[:, None]
    j = np.arange(2 * ATT_BLOCK)[None, :]
    rel = i + ATT_BLOCK - j
    max_exact = N_BUCKETS // 2
    nf = np.maximum(rel, 1).astype(np.float32)
    large = max_exact + (np.log(nf / max_exact) / np.float32(math.log(MAX_DISTANCE / max_exact))
                         * (N_BUCKETS - max_exact)).astype(np.int32)
    large = np.minimum(large, N_BUCKETS - 1)
    bucket = np.where(rel < max_exact, rel, large)
    return np.where((rel >= 0) & (rel < WINDOW), bucket, -1).astype(np.int32)


def _token_major(a, nchunks):
    return jnp.transpose(a, (0, 2, 1)).reshape(nchunks, 1, -1)


def kernel(x, norm_mix, w_in, b_in, conv_w, q_norm, k_norm, sinks, rel_bias, out_norm_conv, out_norm_attn,
           w_out, b_out, norm_ffn, w_router, b_router, w1, b1, w2, b2):
    bsz, seq, d_model = x.shape
    depth = w_in.shape[0]
    bf16 = jnp.bfloat16
    bucket = jnp.asarray(_t5_bucket_table())
    t_total = bsz * seq
    chunk = min(MOE_CHUNK, t_total)
    nchunks = t_total // chunk
    table_blocks = chunk * TOP_K // MOE_ROWS + N_EXPERTS + 2
    for l in range(depth):
        d_ff = w2.shape[2]
        params = dict(
            sinks=sinks[l], rel_bias=rel_bias, gmix=norm_mix[l][None], win=w_in[l].astype(bf16), bin=b_in[l][None],
            convw=conv_w[l], gq2=jnp.tile(q_norm[l], 2)[None] * (HEAD_DIM ** -0.5), gk2=jnp.tile(k_norm[l], 2)[None],
            gc=out_norm_conv[l][None], ga=out_norm_attn[l][None], wout=w_out[l].astype(bf16), bout=b_out[l][None],
            gffn=norm_ffn[l][None], wrt=w_router[l].T, br=b_router[l][:, None])
        x1r, hr, ids_t, gates_t, rank_t, counts_t = _mixer(x, params, bucket, chunk)
        counts = counts_t[:, :, 0]
        padded = (counts + MOE_ROWS - 1) // MOE_ROWS * MOE_ROWS
        pad_starts = jnp.cumsum(padded, axis=1) - padded
        tables = _block_tables(padded // MOE_ROWS, table_blocks)
        tiles_per_chunk = ids_t.shape[0] // nchunks
        tile_starts = jnp.repeat(pad_starts, tiles_per_chunk, axis=0)
        dest_t = rank_t
        for e in range(N_EXPERTS):
            dest_t = dest_t + jnp.where(ids_t == e, tile_starts[:, e:e + 1, None], 0)
        w1p, w2p = _wprep(w1[l], w2[l])
        b1p = jnp.transpose(b1[l].reshape(N_EXPERTS, 2 * d_ff // MXU_COLS, LANES, 2), (0, 1, 3, 2))
        b1p = b1p.reshape(N_EXPERTS, 1, 2 * d_ff)
        out = _moe(x1r, hr, _token_major(dest_t, nchunks), _token_major(gates_t, nchunks), tables, w1p, w2p, b1p,
                   b2[l][:, None, :], chunk, table_blocks)
        x = out.reshape(bsz, seq, d_model)
    return x
```

```python
import functools
import math

import jax
import jax.numpy as jnp
import numpy as np
from jax import lax
from jax.experimental import pallas as pl
from jax.experimental.pallas import tpu as pltpu

HEAD_DIM = 64
N_Q_HEADS = 8
N_KV_HEADS = 2
Q_PER_KV = N_Q_HEADS // N_KV_HEADS
CONV_K = 3
WINDOW = 128
ATT_BLOCK = 128
N_BUCKETS = 32
MAX_DISTANCE = 128
N_EXPERTS = 32
TOP_K = 4
TOP_K_SHIFT = 2
SWIGLU_LIMIT = 7.0
SWIGLU_ALPHA = 1.702
EPS = 1e-5
MASK_VALUE = -1e30

LANES = 128
SUBLANES = 8
MXU_COLS = 256
VMEM_LIMIT_BYTES = 56 * 1024 * 1024

MIX_ROWS = 512
MOE_CHUNK = 4096
MOE_ROWS = 128
BLOCKS_PER_STEP = 2
MOVES_IN_FIRST_MATMUL = 64
WEIGHT_SLOTS = 3
MOE_COLS = 256
OUT_ROWS = 128
PREP_ROWS = 256
INVERT_UNROLL = 16
SCATTER_UNROLL = 4


def _rms(x, g):
    return x * lax.rsqrt(jnp.mean(x * x, axis=-1, keepdims=True) + EPS) * g


def _half_head_norm(t, gain2, lo):
    t2 = t * t
    s_lo = jnp.sum(jnp.where(lo, t2, 0.0), axis=-1, keepdims=True)
    s_hi = jnp.sum(jnp.where(lo, 0.0, t2), axis=-1, keepdims=True)
    r = jnp.where(lo, lax.rsqrt(s_lo * (1.0 / HEAD_DIM) + EPS), lax.rsqrt(s_hi * (1.0 / HEAD_DIM) + EPS))
    return t * r * gain2


def _store_row_tiles(ref, val):
    rows = val.shape[0]
    for kk in range(val.shape[1] // LANES):
        ref[pl.ds(kk, rows, stride=SUBLANES), :] = val[:, kk * LANES:(kk + 1) * LANES]


def _wprep_kernel(w1_ref, w2_ref, perm_ref, w1o_ref, w2o_ref):
    perm = perm_ref[...]
    for c in range(w1_ref.shape[1] // MXU_COLS):
        cols = pl.ds(c * MXU_COLS, MXU_COLS)
        t = w1_ref[:, cols].astype(jnp.bfloat16)
        w1o_ref[:, cols] = jnp.dot(t, perm, preferred_element_type=jnp.float32).astype(jnp.bfloat16)
    w2o_ref[...] = w2_ref[...].astype(jnp.bfloat16)


def _wprep_perm():
    j = np.arange(MXU_COLS)
    src = np.where(j < LANES, 2 * j, 2 * (j - LANES) + 1)
    return jnp.asarray((np.arange(MXU_COLS)[:, None] == src[None, :]).astype(np.float32), jnp.bfloat16)


def _wprep(w1, w2):
    n_e, d_model, two_ff = w1.shape
    d_ff, d_out = w2.shape[1], w2.shape[2]
    assert d_model == d_ff, "one row grid serves both weight tensors"
    w1o, w2o = pl.pallas_call(
        _wprep_kernel,
        grid=(n_e * d_model // PREP_ROWS,),
        in_specs=[pl.BlockSpec((PREP_ROWS, two_ff), lambda r: (r, 0)),
                  pl.BlockSpec((PREP_ROWS, d_out), lambda r: (r, 0)),
                  pl.BlockSpec((MXU_COLS, MXU_COLS), lambda r: (0, 0))],
        out_specs=[pl.BlockSpec((PREP_ROWS, two_ff), lambda r: (r, 0)),
                   pl.BlockSpec((PREP_ROWS, d_out), lambda r: (r, 0))],
        out_shape=[jax.ShapeDtypeStruct((n_e * d_model, two_ff), jnp.bfloat16),
                   jax.ShapeDtypeStruct((n_e * d_ff, d_out), jnp.bfloat16)],
        compiler_params=pltpu.CompilerParams(dimension_semantics=("arbitrary",)),
        name="wprep",
    )(w1.reshape(n_e * d_model, two_ff), w2.reshape(n_e * d_ff, d_out), _wprep_perm())
    return w1o.reshape(w1.shape), w2o.reshape(w2.shape)


def _mixer_kernel(tiles_per_seq, tiles_per_chunk,
                  sinks_ref, relb_ref,
                  x_ref, xnext_ref, gmix_ref, win_ref, bin_ref, convw_ref, gq_ref, gk_ref, bucket_ref,
                  gc_ref, ga_ref, wout_ref, bout_ref, gffn_ref, wrt_ref, br_ref, tri_ref,
                  w1_ref, w2_ref, perm_ref,
                  x1_ref, h2_ref, ids_ref, gates_ref, rank_ref, counts_ref, w1o_ref, w2o_ref,
                  kd_scr, vd_scr, cu_scr, bias_scr, qm_scr, cnt_scr, proj_scr):
    _wprep_kernel(w1_ref, w2_ref, perm_ref, w1o_ref, w2o_ref)
    rows = x_ref.shape[1]
    d_model = x_ref.shape[2]
    conv_w = d_model // 2
    in_w = win_ref.shape[1]
    nblk = rows // ATT_BLOCK
    tile = pl.program_id(0)
    s = lax.rem(tile, tiles_per_seq)

    def project(h, j):
        cols = pl.ds(j * MXU_COLS, MXU_COLS)
        proj_scr[:, cols] = jnp.dot(h, win_ref[:, cols], preferred_element_type=jnp.float32) + bin_ref[:, cols]

    @pl.when(tile == 0)
    def _first_projection():
        h0 = _rms(x_ref[0], gmix_ref[...]).astype(jnp.bfloat16)
        for j in range(in_w // MXU_COLS):
            project(h0, j)

    @pl.when(tile == 0)
    def _build_bias():
        bk = bucket_ref[...]
        col = lax.broadcasted_iota(jnp.int32, bk.shape, 1)
        accs = [jnp.full(bk.shape, MASK_VALUE, jnp.float32) for _ in range(N_Q_HEADS)]
        for bb in range(N_BUCKETS):
            hit = bk == bb
            for h in range(N_Q_HEADS):
                accs[h] = jnp.where(hit, relb_ref[bb, h], accs[h])
        for h in range(N_Q_HEADS):
            g, j = divmod(h, Q_PER_KV)
            bias_scr[0, g, pl.ds(j * ATT_BLOCK, ATT_BLOCK), :] = accs[h]
            bias_scr[1, g, pl.ds(j * ATT_BLOCK, ATT_BLOCK), :] = jnp.where(col < ATT_BLOCK, MASK_VALUE, accs[h])

    @pl.when(s == 0)
    def _reset_state():
        kd_scr[:, pl.ds(0, ATT_BLOCK), :] = jnp.zeros((N_KV_HEADS, ATT_BLOCK, LANES), kd_scr.dtype)
        vd_scr[:, pl.ds(0, ATT_BLOCK), :] = jnp.zeros((N_KV_HEADS, ATT_BLOCK, LANES), vd_scr.dtype)
        cu_scr[...] = jnp.zeros(cu_scr.shape, cu_scr.dtype)

    @pl.when(lax.rem(tile, tiles_per_chunk) == 0)
    def _reset_counts():
        cnt_scr[...] = jnp.zeros(cnt_scr.shape, cnt_scr.dtype)

    b_gate = proj_scr[:, 0:conv_w]
    c_gate = proj_scr[:, conv_w:2 * conv_w]
    u = proj_scr[:, 2 * conv_w:3 * conv_w]
    q = proj_scr[:, 3 * conv_w:4 * conv_w]
    k = proj_scr[:, 4 * conv_w:4 * conv_w + LANES]
    v = proj_scr[:, 4 * conv_w + LANES:4 * conv_w + 2 * LANES]

    cu = c_gate * u
    row = lax.broadcasted_iota(jnp.int32, cu.shape, 0)
    prev2 = cu_scr[SUBLANES - 2:SUBLANES - 1, :]
    prev1 = cu_scr[SUBLANES - 1:SUBLANES, :]
    r1 = jnp.where(row == 0, prev1, pltpu.roll(cu, 1, 0))
    r2 = jnp.where(row == 0, prev2, jnp.where(row == 1, prev1, pltpu.roll(cu, 2, 0)))
    y_conv = b_gate * (convw_ref[2:3, :] * cu + convw_ref[1:2, :] * r1 + convw_ref[0:1, :] * r2)
    cu_scr[...] = cu[rows - SUBLANES:rows, :]

    lane = lax.broadcasted_iota(jnp.int32, (rows, LANES), 1)
    lo = lane < HEAD_DIM
    kn = _half_head_norm(k, gk_ref[...], lo)
    krot = pltpu.roll(kn, HEAD_DIM, 1)
    vrot = pltpu.roll(v, HEAD_DIM, 1)
    kd_scr[0, pl.ds(ATT_BLOCK, rows), :] = jnp.where(lo, kn, krot).astype(kd_scr.dtype)
    kd_scr[1, pl.ds(ATT_BLOCK, rows), :] = jnp.where(lo, krot, kn).astype(kd_scr.dtype)
    vd_scr[0, pl.ds(ATT_BLOCK, rows), :] = jnp.where(lo, v, vrot).astype(vd_scr.dtype)
    vd_scr[1, pl.ds(ATT_BLOCK, rows), :] = jnp.where(lo, vrot, v).astype(vd_scr.dtype)
    for c in range(conv_w // LANES):
        qn = _half_head_norm(q[:, c * LANES:(c + 1) * LANES], gq_ref[...], lo)
        qm_scr[2 * c] = jnp.where(lo, qn, 0.0).astype(qm_scr.dtype)
        qm_scr[2 * c + 1] = jnp.where(lo, 0.0, qn).astype(qm_scr.dtype)

    lo_b = lax.broadcasted_iota(jnp.int32, (ATT_BLOCK, LANES), 1) < HEAD_DIM
    h_next = _rms(xnext_ref[0], gmix_ref[...]).astype(jnp.bfloat16)
    next_groups = iter(range(in_w // MXU_COLS))
    ya_blocks = []
    for n in range(nblk):
        first = jnp.where(s == 0, 1, 0) if n == 0 else 0
        tiles = []
        for g in range(N_KV_HEADS):
            qs = jnp.concatenate([qm_scr[Q_PER_KV * g + j, pl.ds(n * ATT_BLOCK, ATT_BLOCK), :]
                                  for j in range(Q_PER_KV)], axis=0)
            kw = kd_scr[g, pl.ds(n * ATT_BLOCK, 2 * ATT_BLOCK), :]
            vw = vd_scr[g, pl.ds(n * ATT_BLOCK, 2 * ATT_BLOCK), :]
            sc = lax.dot_general(qs, kw, (((1,), (1,)), ((), ())), preferred_element_type=jnp.float32)
            logits = sc + bias_scr[first, g]
            ps, inv = [], []
            for j in range(Q_PER_KV):
                lj = logits[j * ATT_BLOCK:(j + 1) * ATT_BLOCK]
                sink = sinks_ref[Q_PER_KV * g + j]
                m = jnp.maximum(jnp.max(lj, axis=-1, keepdims=True), sink)
                p = jnp.exp(lj - m)
                den = jnp.sum(p, axis=-1, keepdims=True) + jnp.exp(sink - m)
                ps.append(p.astype(jnp.bfloat16))
                inv.append(1.0 / den)
            o = jnp.dot(jnp.concatenate(ps, axis=0), vw, preferred_element_type=jnp.float32)
            on = [o[j * ATT_BLOCK:(j + 1) * ATT_BLOCK] * inv[j] for j in range(Q_PER_KV)]
            tiles.append(jnp.where(lo_b, on[0], on[1]))
            tiles.append(jnp.where(lo_b, on[2], on[3]))
            j = next(next_groups, None)
            if j is not None:
                project(h_next, j)
        ya_blocks.append(jnp.concatenate(tiles, axis=-1))
    y_attn = jnp.concatenate(ya_blocks, axis=0)
    for j in next_groups:
        project(h_next, j)

    for g in range(N_KV_HEADS):
        kd_scr[g, pl.ds(0, ATT_BLOCK), :] = kd_scr[g, pl.ds(rows, ATT_BLOCK), :]
        vd_scr[g, pl.ds(0, ATT_BLOCK), :] = vd_scr[g, pl.ds(rows, ATT_BLOCK), :]

    mix = jnp.concatenate([_rms(y_conv, gc_ref[...]), _rms(y_attn, ga_ref[...])], axis=-1)
    x1 = (x_ref[0] + jnp.dot(mix.astype(jnp.bfloat16), wout_ref[...], preferred_element_type=jnp.float32)
          + bout_ref[...])
    _store_row_tiles(x1_ref, x1)
    h2 = _rms(x1, gffn_ref[...])
    _store_row_tiles(h2_ref, h2)
    h_hi = h2.astype(jnp.bfloat16)
    h_lo = (h2 - h_hi.astype(jnp.float32)).astype(jnp.bfloat16)
    wr = wrt_ref[...]
    w_hi = wr.astype(jnp.bfloat16)
    w_lo = (wr - w_hi.astype(jnp.float32)).astype(jnp.bfloat16)
    nt = (((1,), (1,)), ((), ()))
    lt = (lax.dot_general(w_hi, h_hi, nt, preferred_element_type=jnp.float32)
          + lax.dot_general(w_hi, h_lo, nt, preferred_element_type=jnp.float32)
          + lax.dot_general(w_lo, h_hi, nt, preferred_element_type=jnp.float32)) + br_ref[...]
    eid = lax.broadcasted_iota(jnp.int32, lt.shape, 0)
    vals, idxs, hits = [], [], []
    for _ in range(TOP_K):
        m = jnp.max(lt, axis=0, keepdims=True)
        idx = jnp.min(jnp.where(lt == m, eid, N_EXPERTS), axis=0, keepdims=True)
        hit = eid == idx
        vals.append(m)
        idxs.append(idx)
        hits.append(hit)
        lt = jnp.where(hit, -jnp.inf, lt)
    es = [jnp.exp(vv - vals[0]) for vv in vals]
    tot = es[0] + es[1] + es[2] + es[3]
    ids_ref[0] = jnp.concatenate(idxs, axis=0)
    gates_ref[0] = jnp.concatenate([e / tot for e in es], axis=0)

    member = jnp.where(hits[0] | hits[1] | hits[2] | hits[3], 1.0, 0.0)
    before = jnp.dot(member.astype(jnp.bfloat16), tri_ref[...], preferred_element_type=jnp.float32)
    running = cnt_scr[...][:, 0:1]
    rank_e = before + running
    rank_ref[0] = jnp.concatenate([jnp.sum(jnp.where(hh, rank_e, 0.0), axis=0, keepdims=True) for hh in hits],
                                  axis=0).astype(jnp.int32)
    new_counts = cnt_scr[...] + jnp.sum(member, axis=1, keepdims=True)
    cnt_scr[...] = new_counts
    counts_ref[0] = new_counts.astype(jnp.int32)


def _mixer(x, p, bucket, chunk, w1, w2):
    n_e, _, two_ff = w1.shape
    d_ff, d_out = w2.shape[1], w2.shape[2]
    n_wtiles = n_e * d_ff // PREP_ROWS
    w1f = w1.reshape(n_e * w1.shape[1], two_ff)
    w2f = w2.reshape(n_e * d_ff, d_out)
    bsz, seq, d_model = x.shape
    rows = min(MIX_ROWS, seq)
    ns = seq // rows
    tiles_per_chunk = chunk // rows
    nchunks = bsz * seq // chunk
    conv_w = d_model // 2
    in_w = p["win"].shape[1]
    tri = jnp.asarray(np.triu(np.ones((rows, rows), np.float32), 1), jnp.bfloat16)
    n_tiles = bsz * ns
    full = lambda shape: pl.BlockSpec(shape, lambda i: (0,) * len(shape))
    smem = pl.BlockSpec(memory_space=pltpu.SMEM)
    tok_spec = pl.BlockSpec((1, rows, d_model), lambda i: (i // ns, i % ns, 0))
    next_spec = pl.BlockSpec((1, rows, d_model),
                             lambda i: (jnp.minimum(i + 1, n_tiles - 1) // ns, jnp.minimum(i + 1, n_tiles - 1) % ns, 0))
    tile_spec = pl.BlockSpec((rows * SUBLANES, LANES), lambda i: (i, 0))
    sel_spec = pl.BlockSpec((1, TOP_K, rows), lambda i: (i, 0, 0))
    cnt_spec = pl.BlockSpec((1, N_EXPERTS, LANES), lambda i: (i // tiles_per_chunk, 0, 0))
    t_total = bsz * seq
    sel_shape = (n_tiles, TOP_K, rows)
    wtile = lambda i: (jnp.minimum(i, n_wtiles - 1), 0)
    w1_spec = pl.BlockSpec((PREP_ROWS, two_ff), wtile)
    w2_spec = pl.BlockSpec((PREP_ROWS, d_out), wtile)
    outs = pl.pallas_call(
        functools.partial(_mixer_kernel, ns, tiles_per_chunk),
        grid=(n_tiles,),
        in_specs=[smem, smem, tok_spec, next_spec, full((1, d_model)), full((d_model, in_w)), full((1, in_w)),
                  full((CONV_K, conv_w)), full((1, LANES)), full((1, LANES)), full((ATT_BLOCK, 2 * ATT_BLOCK)),
                  full((1, conv_w)), full((1, conv_w)), full((d_model, d_model)), full((1, d_model)),
                  full((1, d_model)), full((N_EXPERTS, d_model)), full((N_EXPERTS, 1)), full((rows, rows)),
                  w1_spec, w2_spec, full((MXU_COLS, MXU_COLS))],
        out_specs=[tile_spec, tile_spec, sel_spec, sel_spec, sel_spec, cnt_spec, w1_spec, w2_spec],
        out_shape=[jax.ShapeDtypeStruct((t_total * SUBLANES, LANES), jnp.float32),
                   jax.ShapeDtypeStruct((t_total * SUBLANES, LANES), jnp.float32),
                   jax.ShapeDtypeStruct(sel_shape, jnp.int32),
                   jax.ShapeDtypeStruct(sel_shape, jnp.float32),
                   jax.ShapeDtypeStruct(sel_shape, jnp.int32),
                   jax.ShapeDtypeStruct((nchunks, N_EXPERTS, LANES), jnp.int32),
                   jax.ShapeDtypeStruct(w1f.shape, jnp.bfloat16),
                   jax.ShapeDtypeStruct(w2f.shape, jnp.bfloat16)],
        scratch_shapes=[pltpu.VMEM((N_KV_HEADS, ATT_BLOCK + rows, LANES), jnp.bfloat16),
                        pltpu.VMEM((N_KV_HEADS, ATT_BLOCK + rows, LANES), jnp.bfloat16),
                        pltpu.VMEM((SUBLANES, conv_w), jnp.float32),
                        pltpu.VMEM((2, N_KV_HEADS, Q_PER_KV * ATT_BLOCK, 2 * ATT_BLOCK), jnp.float32),
                        pltpu.VMEM((N_Q_HEADS, rows, LANES), jnp.bfloat16),
                        pltpu.VMEM((N_EXPERTS, LANES), jnp.float32),
                        pltpu.VMEM((rows, in_w), jnp.float32)],
        compiler_params=pltpu.CompilerParams(dimension_semantics=("arbitrary",),
                                             vmem_limit_bytes=VMEM_LIMIT_BYTES),
        name="mixer",
    )(p["sinks"], p["rel_bias"], x, x, p["gmix"], p["win"], p["bin"], p["convw"], p["gq2"], p["gk2"], bucket,
      p["gc"], p["ga"], p["wout"], p["bout"], p["gffn"], p["wrt"], p["br"], tri, w1f, w2f, _wprep_perm())
    if n_tiles >= n_wtiles:
        prepared = (outs[6].reshape(w1.shape), outs[7].reshape(w2.shape))
    else:
        prepared = _wprep(w1, w2)
    return tuple(outs[:6]) + prepared


def _moe_kernel(nbc_ref, bexp_ref, bfirst_ref, bslot_ref, bnext_ref, firstexp_ref,
                x1_hbm, h_hbm, dest_hbm, gate_hbm, fill_hbm, w1_hbm, w2_hbm, b1_ref, b2_ref,
                out_hbm,
                acc, hbuf, w1buf, w2buf, gbuf0, gbuf1, rbuf0, rbuf1, obuf, dest_s, gate_s, list_s,
                io_sem, w_sem, l_sem, o_sem):
    chunk = hbuf.shape[0] // SUBLANES - 1
    chunk_rows = chunk * SUBLANES
    n_assign = TOP_K * chunk
    pad_block = bexp_ref.shape[1] - 1
    d_model = w1buf.shape[1]
    d_ff = w2buf.shape[1]
    nk = d_model // LANES
    c = pl.program_id(0)
    row0 = pl.multiple_of(c * chunk_rows, SUBLANES)

    def weight_copies(e, slot):
        return (pltpu.make_async_copy(w1_hbm.at[e], w1buf.at[slot], w_sem.at[0, slot]),
                pltpu.make_async_copy(w2_hbm.at[e], w2buf.at[slot], w_sem.at[1, slot]))

    def chunk_in_copies():
        return (pltpu.make_async_copy(x1_hbm.at[pl.ds(row0, chunk_rows)], acc.at[pl.ds(0, chunk_rows)], io_sem.at[0]),
                pltpu.make_async_copy(h_hbm.at[pl.ds(row0, chunk_rows)], hbuf.at[pl.ds(0, chunk_rows)], io_sem.at[1]))

    def list_copies():
        return (pltpu.make_async_copy(dest_hbm.at[c], dest_s, l_sem.at[0]),
                pltpu.make_async_copy(gate_hbm.at[c], gate_s.at[:, pl.ds(0, n_assign)], l_sem.at[1]),
                pltpu.make_async_copy(fill_hbm, list_s, l_sem.at[2]))

    for cp in list_copies() + chunk_in_copies():
        cp.start()

    @pl.when(c == 0)
    def _first_weights():
        for cp in weight_copies(firstexp_ref[0], 0):
            cp.start()

    gate_s[0, n_assign] = jnp.float32(0.0)
    for cp in list_copies():
        cp.wait()

    def invert_body(i, carry):
        for jj in range(INVERT_UNROLL):
            a = i * INVERT_UNROLL + jj
            list_s[0, dest_s[0, a]] = a
        return carry

    lax.fori_loop(0, n_assign // INVERT_UNROLL, invert_body, 0)

    acc[pl.ds(chunk_rows, SUBLANES), :] = jnp.zeros((SUBLANES, LANES), jnp.float32)
    hbuf[pl.ds(chunk_rows, SUBLANES), :] = jnp.zeros((SUBLANES, LANES), jnp.float32)
    rbuf1[...] = jnp.zeros(rbuf1.shape, rbuf1.dtype)
    for cp in chunk_in_copies():
        cp.wait()

    def token_rows(a):
        return pl.ds(pl.multiple_of((a >> TOP_K_SHIFT) * SUBLANES, SUBLANES), SUBLANES)

    def weights_step(g):
        @pl.when(bfirst_ref[c, g] == 1)
        def _():
            slot = bslot_ref[c, g]
            for cp in weight_copies(bexp_ref[c, g], slot):
                cp.wait()
            nxt = bnext_ref[c, g]

            @pl.when(nxt >= 0)
            def _():
                for cp in weight_copies(nxt, jnp.where(slot == WEIGHT_SLOTS - 1, 0, slot + 1)):
                    cp.start()

    def gather_rows(g, gbuf, j_lo, j_hi):
        s0 = g * MOE_ROWS
        for j in range(j_lo, j_hi):
            gbuf[pl.ds(j * SUBLANES, SUBLANES), :] = hbuf[token_rows(list_s[0, s0 + j]), :]

    def scatter_rows(g, rbuf, j_lo, j_hi):
        s0 = g * MOE_ROWS
        for j0 in range(j_lo, j_hi, SCATTER_UNROLL):
            dsts, vals = [], []
            for j in range(j0, j0 + SCATTER_UNROLL):
                a = list_s[0, s0 + j]
                dst = token_rows(a)
                dsts.append(dst)
                vals.append(acc[dst, :] + gate_s[0, a] * rbuf[pl.ds(j * SUBLANES, SUBLANES), :])
            for dst, val in zip(dsts, vals):
                acc[dst, :] = val

    n_hid = 2 * d_ff // MOE_COLS
    n_out = d_model // MOE_COLS
    half = MOVES_IN_FIRST_MATMUL
    rows_hid = half // n_hid
    rows_out = (MOE_ROWS - half) // n_out

    def stage(g_gather, gbuf_in, g_compute, gbuf, rbuf, g_scatter, rbuf_out):
        e = bexp_ref[c, g_compute]
        slot = bslot_ref[c, g_compute]
        xs = [gbuf[pl.ds(kk, MOE_ROWS, stride=SUBLANES), :] for kk in range(nk)]
        xb = jnp.concatenate(xs, axis=-1).astype(jnp.bfloat16)
        acts = []
        for cg in range(n_hid):
            cols = pl.ds(cg * MOE_COLS, MOE_COLS)
            hid = jnp.dot(xb, w1buf[slot, :, cols], preferred_element_type=jnp.float32) + b1_ref[e, :, cols]
            for pg in range(MOE_COLS // MXU_COLS):
                glu = jnp.minimum(hid[:, pg * MXU_COLS:pg * MXU_COLS + LANES], SWIGLU_LIMIT)
                lin = jnp.clip(hid[:, pg * MXU_COLS + LANES:(pg + 1) * MXU_COLS], -SWIGLU_LIMIT, SWIGLU_LIMIT)
                acts.append(glu * jax.nn.sigmoid(SWIGLU_ALPHA * glu) * (lin + 1.0))
            gather_rows(g_gather, gbuf_in, cg * rows_hid, (cg + 1) * rows_hid)
            scatter_rows(g_scatter, rbuf_out, cg * rows_hid, (cg + 1) * rows_hid)
        act = jnp.concatenate(acts, axis=-1).astype(jnp.bfloat16)
        for og in range(n_out):
            cols = pl.ds(og * MOE_COLS, MOE_COLS)
            y = jnp.dot(act, w2buf[slot, :, cols], preferred_element_type=jnp.float32) + b2_ref[e, :, cols]
            for kk in range(MOE_COLS // LANES):
                rbuf[pl.ds(og * (MOE_COLS // LANES) + kk, MOE_ROWS, stride=SUBLANES), :] = (
                    y[:, kk * LANES:(kk + 1) * LANES])
            gather_rows(g_gather, gbuf_in, half + og * rows_out, half + (og + 1) * rows_out)
            scatter_rows(g_scatter, rbuf_out, half + og * rows_out, half + (og + 1) * rows_out)

    gather_rows(0, gbuf0, 0, MOE_ROWS)

    def pair(g0, g_scatter):
        @pl.when(bfirst_ref[c, g0] + bfirst_ref[c, g0 + 1] > 0)
        def _():
            weights_step(g0)
            weights_step(g0 + 1)

        stage(g0 + 1, gbuf1, g0, gbuf0, rbuf0, g_scatter, rbuf1)
        stage(g0 + 2, gbuf0, g0 + 1, gbuf1, rbuf1, g0, rbuf0)

    def step_body(i, carry):
        g0 = BLOCKS_PER_STEP * i
        pair(g0, jnp.where(i == 0, pad_block, g0 - 1))
        for k2 in range(2, BLOCKS_PER_STEP, 2):
            pair(g0 + k2, g0 + k2 - 1)
        return carry

    nsteps = (nbc_ref[c] + BLOCKS_PER_STEP - 1) // BLOCKS_PER_STEP
    lax.fori_loop(0, nsteps, step_body, 0)
    scatter_rows(BLOCKS_PER_STEP * nsteps - 1, rbuf1, 0, MOE_ROWS)

    def out_copy(grp, buf):
        return pltpu.make_async_copy(obuf.at[buf], out_hbm.at[pl.ds(c * chunk + grp * OUT_ROWS, OUT_ROWS)],
                                     o_sem.at[buf])

    def out_body(gp, carry):
        for buf in range(2):
            grp = 2 * gp + buf

            @pl.when(gp > 0)
            def _():
                out_copy(grp - 2, buf).wait()

            first = grp * (OUT_ROWS * SUBLANES)
            obuf[buf] = jnp.concatenate(
                [acc[pl.ds(first + kk, OUT_ROWS, stride=SUBLANES), :] for kk in range(nk)], axis=-1)
            out_copy(grp, buf).start()
        return carry

    n_groups = chunk // OUT_ROWS
    lax.fori_loop(0, n_groups // 2, out_body, 0)
    for buf in range(2):
        out_copy(n_groups - 2 + buf, buf).wait()


def _block_tables(nblk, table_blocks):
    nchunks = nblk.shape[0]
    bend = jnp.cumsum(nblk, axis=1)
    nbc = bend[:, -1]
    g = jnp.arange(table_blocks, dtype=jnp.int32)
    nonempty = nblk > 0
    last_exp = jnp.max(jnp.where(nonempty, jnp.arange(N_EXPERTS, dtype=jnp.int32)[None, :], 0), axis=1)
    bexp = jnp.sum((g[None, :, None] >= bend[:, None, :]).astype(jnp.int32), axis=-1)
    bexp = jnp.where(g[None, :] >= nbc[:, None], last_exp[:, None], bexp)
    bstart = bend - nblk
    is_exp = bexp[:, :, None] == jnp.arange(N_EXPERTS, dtype=jnp.int32)[None, None, :]
    per_block = lambda table: jnp.sum(jnp.where(is_exp, table[:, None, :], 0), axis=-1)
    bfirst = ((g[None, :] == per_block(bstart)) & (g[None, :] < nbc[:, None])).astype(jnp.int32)
    flat = nonempty.reshape(-1)
    pair_slot = ((jnp.cumsum(flat.astype(jnp.int32)) - 1) % WEIGHT_SLOTS).reshape(nchunks, N_EXPERTS)
    n_pairs = flat.shape[0]
    idx = jnp.where(flat, jnp.arange(n_pairs, dtype=jnp.int32), n_pairs)
    later = jnp.concatenate([lax.cummin(idx[::-1])[::-1][1:], jnp.full((1,), n_pairs, jnp.int32)])
    pair_next = jnp.where(later < n_pairs, later % N_EXPERTS, -1).reshape(nchunks, N_EXPERTS)
    bslot = per_block(pair_slot)
    bnext = per_block(pair_next)
    first_exp = (jnp.min(idx) % N_EXPERTS).reshape(1)
    return nbc.astype(jnp.int32), bexp, bfirst, bslot.astype(jnp.int32), bnext.astype(jnp.int32), first_exp


def _moe(x1r, hr, dest, gates, tables, w1, w2, b1, b2, chunk, table_blocks):
    t_total = x1r.shape[0] // SUBLANES
    nchunks = t_total // chunk
    d_model = w1.shape[1]
    d_ff = w2.shape[1]
    n_assign = TOP_K * chunk
    n_slots = table_blocks * MOE_ROWS
    fill = jnp.full((1, n_slots), n_assign, jnp.int32)
    anyspace = pl.BlockSpec(memory_space=pl.ANY)
    grid_spec = pltpu.PrefetchScalarGridSpec(
        num_scalar_prefetch=6,
        grid=(nchunks,),
        in_specs=[anyspace] * 7 + [pl.BlockSpec((N_EXPERTS, 1, 2 * d_ff), lambda c, *_: (0, 0, 0)),
                                   pl.BlockSpec((N_EXPERTS, 1, d_model), lambda c, *_: (0, 0, 0))],
        out_specs=anyspace,
        scratch_shapes=[pltpu.VMEM(((chunk + 1) * SUBLANES, LANES), jnp.float32),
                        pltpu.VMEM(((chunk + 1) * SUBLANES, LANES), jnp.float32),
                        pltpu.VMEM((WEIGHT_SLOTS, d_model, 2 * d_ff), jnp.bfloat16),
                        pltpu.VMEM((WEIGHT_SLOTS, d_ff, d_model), jnp.bfloat16),
                        pltpu.VMEM((MOE_ROWS * SUBLANES, LANES), jnp.float32),
                        pltpu.VMEM((MOE_ROWS * SUBLANES, LANES), jnp.float32),
                        pltpu.VMEM((MOE_ROWS * SUBLANES, LANES), jnp.float32),
                        pltpu.VMEM((MOE_ROWS * SUBLANES, LANES), jnp.float32),
                        pltpu.VMEM((2, OUT_ROWS, d_model), jnp.float32),
                        pltpu.SMEM((1, n_assign), jnp.int32),
                        pltpu.SMEM((1, n_assign + LANES), jnp.float32),
                        pltpu.SMEM((1, n_slots), jnp.int32),
                        pltpu.SemaphoreType.DMA((2,)),
                        pltpu.SemaphoreType.DMA((2, WEIGHT_SLOTS)),
                        pltpu.SemaphoreType.DMA((3,)),
                        pltpu.SemaphoreType.DMA((2,))],
    )
    return pl.pallas_call(
        _moe_kernel,
        grid_spec=grid_spec,
        out_shape=jax.ShapeDtypeStruct((t_total, d_model), jnp.float32),
        compiler_params=pltpu.CompilerParams(dimension_semantics=("arbitrary",),
                                             vmem_limit_bytes=VMEM_LIMIT_BYTES),
        name="moe",
    )(*tables, x1r, hr, dest, gates, fill, w1, w2, b1, b2)


def _t5_bucket_table():
    i = np.arange(ATT_BLOCK)[:, None]
    j = np.arange(2 * ATT_BLOCK)[None, :]
    rel = i + ATT_BLOCK - j
    max_exact = N_BUCKETS // 2
    nf = np.maximum(rel, 1).astype(np.float32)
    large = max_exact + (np.log(nf / max_exact) / np.float32(math.log(MAX_DISTANCE / max_exact))
                         * (N_BUCKETS - max_exact)).astype(np.int32)
    large = np.minimum(large, N_BUCKETS - 1)
    bucket = np.where(rel < max_exact, rel, large)
    return np.where((rel >= 0) & (rel < WINDOW), bucket, -1).astype(np.int32)


def _token_major(a, nchunks):
    return jnp.transpose(a, (0, 2, 1)).reshape(nchunks, 1, -1)


def kernel(x, norm_mix, w_in, b_in, conv_w, q_norm, k_norm, sinks, rel_bias, out_norm_conv, out_norm_attn,
           w_out, b_out, norm_ffn, w_router, b_router, w1, b1, w2, b2):
    bsz, seq, d_model = x.shape
    depth = w_in.shape[0]
    bf16 = jnp.bfloat16
    bucket = jnp.asarray(_t5_bucket_table())
    t_total = bsz * seq
    chunk = min(MOE_CHUNK, t_total)
    nchunks = t_total // chunk
    table_blocks = chunk * TOP_K // MOE_ROWS + N_EXPERTS + BLOCKS_PER_STEP
    for l in range(depth):
        d_ff = w2.shape[2]
        params = dict(
            sinks=sinks[l], rel_bias=rel_bias, gmix=norm_mix[l][None], win=w_in[l].astype(bf16), bin=b_in[l][None],
            convw=conv_w[l], gq2=jnp.tile(q_norm[l], 2)[None] * (HEAD_DIM ** -0.5), gk2=jnp.tile(k_norm[l], 2)[None],
            gc=out_norm_conv[l][None], ga=out_norm_attn[l][None], wout=w_out[l].astype(bf16), bout=b_out[l][None],
            gffn=norm_ffn[l][None], wrt=w_router[l].T, br=b_router[l][:, None])
        x1r, hr, ids_t, gates_t, rank_t, counts_t, w1p, w2p = _mixer(x, params, bucket, chunk, w1[l], w2[l])
        counts = counts_t[:, :, 0]
        padded = (counts + MOE_ROWS - 1) // MOE_ROWS * MOE_ROWS
        pad_starts = jnp.cumsum(padded, axis=1) - padded
        tables = _block_tables(padded // MOE_ROWS, table_blocks)
        tiles_per_chunk = ids_t.shape[0] // nchunks
        tile_starts = jnp.repeat(pad_starts, tiles_per_chunk, axis=0)
        dest_t = rank_t
        for e in range(N_EXPERTS):
            dest_t = dest_t + jnp.where(ids_t == e, tile_starts[:, e:e + 1, None], 0)
        b1p = jnp.transpose(b1[l].reshape(N_EXPERTS, 2 * d_ff // MXU_COLS, LANES, 2), (0, 1, 3, 2))
        b1p = b1p.reshape(N_EXPERTS, 1, 2 * d_ff)
        out = _moe(x1r, hr, _token_major(dest_t, nchunks), _token_major(gates_t, nchunks), tables, w1p, w2p, b1p,
                   b2[l][:, None, :], chunk, table_blocks)
        x = out.reshape(bsz, seq, d_model)
    return x
```

```python
import functools
import math

import jax
import jax.numpy as jnp
import numpy as np
from jax import lax
from jax.experimental import pallas as pl
from jax.experimental.pallas import tpu as pltpu

HEAD_DIM = 64
N_Q_HEADS = 8
N_KV_HEADS = 2
Q_PER_KV = N_Q_HEADS // N_KV_HEADS
CONV_K = 3
WINDOW = 128
ATT_BLOCK = 128
N_BUCKETS = 32
MAX_DISTANCE = 128
N_EXPERTS = 32
TOP_K = 4
TOP_K_SHIFT = 2
SWIGLU_LIMIT = 7.0
SWIGLU_ALPHA = 1.702
EPS = 1e-5
MASK_VALUE = -1e30

LANES = 128
SUBLANES = 8
MXU_COLS = 256
VMEM_LIMIT_BYTES = 56 * 1024 * 1024

MIX_ROWS = 512
MOE_CHUNK = 4096
MOE_ROWS = 128
WEIGHT_SLOTS = 3
OUT_ROWS = 128
PREP_ROWS = 256
INVERT_UNROLL = 16
SCATTER_UNROLL = 4


def _rms(x, g):
    return x * lax.rsqrt(jnp.mean(x * x, axis=-1, keepdims=True) + EPS) * g


def _half_head_norm(t, gain2, lo):
    t2 = t * t
    s_lo = jnp.sum(jnp.where(lo, t2, 0.0), axis=-1, keepdims=True)
    s_hi = jnp.sum(jnp.where(lo, 0.0, t2), axis=-1, keepdims=True)
    r = jnp.where(lo, lax.rsqrt(s_lo * (1.0 / HEAD_DIM) + EPS), lax.rsqrt(s_hi * (1.0 / HEAD_DIM) + EPS))
    return t * r * gain2


def _store_row_tiles(ref, val):
    rows = val.shape[0]
    for kk in range(val.shape[1] // LANES):
        ref[pl.ds(kk, rows, stride=SUBLANES), :] = val[:, kk * LANES:(kk + 1) * LANES]


def _wprep_kernel(w1_ref, w2_ref, perm_ref, w1o_ref, w2o_ref):
    perm = perm_ref[...]
    for c in range(w1_ref.shape[1] // MXU_COLS):
        cols = pl.ds(c * MXU_COLS, MXU_COLS)
        t = w1_ref[:, cols].astype(jnp.bfloat16)
        w1o_ref[:, cols] = jnp.dot(t, perm, preferred_element_type=jnp.float32).astype(jnp.bfloat16)
    w2o_ref[...] = w2_ref[...].astype(jnp.bfloat16)


def _wprep_perm():
    j = np.arange(MXU_COLS)
    src = np.where(j < LANES, 2 * j, 2 * (j - LANES) + 1)
    return jnp.asarray((np.arange(MXU_COLS)[:, None] == src[None, :]).astype(np.float32), jnp.bfloat16)


def _wprep(w1, w2):
    n_e, d_model, two_ff = w1.shape
    d_ff, d_out = w2.shape[1], w2.shape[2]
    assert d_model == d_ff, "one row grid serves both weight tensors"
    w1o, w2o = pl.pallas_call(
        _wprep_kernel,
        grid=(n_e * d_model // PREP_ROWS,),
        in_specs=[pl.BlockSpec((PREP_ROWS, two_ff), lambda r: (r, 0)),
                  pl.BlockSpec((PREP_ROWS, d_out), lambda r: (r, 0)),
                  pl.BlockSpec((MXU_COLS, MXU_COLS), lambda r: (0, 0))],
        out_specs=[pl.BlockSpec((PREP_ROWS, two_ff), lambda r: (r, 0)),
                   pl.BlockSpec((PREP_ROWS, d_out), lambda r: (r, 0))],
        out_shape=[jax.ShapeDtypeStruct((n_e * d_model, two_ff), jnp.bfloat16),
                   jax.ShapeDtypeStruct((n_e * d_ff, d_out), jnp.bfloat16)],
        compiler_params=pltpu.CompilerParams(dimension_semantics=("arbitrary",)),
        name="wprep",
    )(w1.reshape(n_e * d_model, two_ff), w2.reshape(n_e * d_ff, d_out), _wprep_perm())
    return w1o.reshape(w1.shape), w2o.reshape(w2.shape)


def _mixer_kernel(tiles_per_seq, tiles_per_chunk,
                  sinks_ref, relb_ref,
                  x_ref, xnext_ref, gmix_ref, win_ref, bin_ref, convw_ref, gq_ref, gk_ref, bucket_ref,
                  gc_ref, ga_ref, wout_ref, bout_ref, gffn_ref, wrt_ref, br_ref, tri_ref,
                  w1_ref, w2_ref, perm_ref,
                  x1_ref, h2_ref, ids_ref, gates_ref, rank_ref, counts_ref, w1o_ref, w2o_ref,
                  kd_scr, vd_scr, cu_scr, bias_scr, qm_scr, cnt_scr, proj_scr):
    rows = x_ref.shape[1]
    d_model = x_ref.shape[2]
    conv_w = d_model // 2
    in_w = win_ref.shape[1]
    nblk = rows // ATT_BLOCK
    tile = pl.program_id(0)
    s = lax.rem(tile, tiles_per_seq)

    def project(h, j):
        cols = pl.ds(j * MXU_COLS, MXU_COLS)
        proj_scr[:, cols] = jnp.dot(h, win_ref[:, cols], preferred_element_type=jnp.float32) + bin_ref[:, cols]

    @pl.when(tile == 0)
    def _first_projection():
        h0 = _rms(x_ref[0], gmix_ref[...]).astype(jnp.bfloat16)
        for j in range(in_w // MXU_COLS):
            project(h0, j)

    @pl.when(tile == 0)
    def _build_bias():
        bk = bucket_ref[...]
        col = lax.broadcasted_iota(jnp.int32, bk.shape, 1)
        accs = [jnp.full(bk.shape, MASK_VALUE, jnp.float32) for _ in range(N_Q_HEADS)]
        for bb in range(N_BUCKETS):
            hit = bk == bb
            for h in range(N_Q_HEADS):
                accs[h] = jnp.where(hit, relb_ref[bb, h], accs[h])
        for h in range(N_Q_HEADS):
            g, j = divmod(h, Q_PER_KV)
            bias_scr[0, g, pl.ds(j * ATT_BLOCK, ATT_BLOCK), :] = accs[h]
            bias_scr[1, g, pl.ds(j * ATT_BLOCK, ATT_BLOCK), :] = jnp.where(col < ATT_BLOCK, MASK_VALUE, accs[h])

    @pl.when(s == 0)
    def _reset_state():
        kd_scr[:, pl.ds(0, ATT_BLOCK), :] = jnp.zeros((N_KV_HEADS, ATT_BLOCK, LANES), kd_scr.dtype)
        vd_scr[:, pl.ds(0, ATT_BLOCK), :] = jnp.zeros((N_KV_HEADS, ATT_BLOCK, LANES), vd_scr.dtype)
        cu_scr[...] = jnp.zeros(cu_scr.shape, cu_scr.dtype)

    @pl.when(lax.rem(tile, tiles_per_chunk) == 0)
    def _reset_counts():
        cnt_scr[...] = jnp.zeros(cnt_scr.shape, cnt_scr.dtype)

    b_gate = proj_scr[:, 0:conv_w]
    c_gate = proj_scr[:, conv_w:2 * conv_w]
    u = proj_scr[:, 2 * conv_w:3 * conv_w]
    q = proj_scr[:, 3 * conv_w:4 * conv_w]
    k = proj_scr[:, 4 * conv_w:4 * conv_w + LANES]
    v = proj_scr[:, 4 * conv_w + LANES:4 * conv_w + 2 * LANES]

    cu = c_gate * u
    row = lax.broadcasted_iota(jnp.int32, cu.shape, 0)
    prev2 = cu_scr[SUBLANES - 2:SUBLANES - 1, :]
    prev1 = cu_scr[SUBLANES - 1:SUBLANES, :]
    r1 = jnp.where(row == 0, prev1, pltpu.roll(cu, 1, 0))
    r2 = jnp.where(row == 0, prev2, jnp.where(row == 1, prev1, pltpu.roll(cu, 2, 0)))
    y_conv = b_gate * (convw_ref[2:3, :] * cu + convw_ref[1:2, :] * r1 + convw_ref[0:1, :] * r2)
    cu_scr[...] = cu[rows - SUBLANES:rows, :]

    lane = lax.broadcasted_iota(jnp.int32, (rows, LANES), 1)
    lo = lane < HEAD_DIM
    kn = _half_head_norm(k, gk_ref[...], lo)
    krot = pltpu.roll(kn, HEAD_DIM, 1)
    vrot = pltpu.roll(v, HEAD_DIM, 1)
    kd_scr[0, pl.ds(ATT_BLOCK, rows), :] = jnp.where(lo, kn, krot).astype(kd_scr.dtype)
    kd_scr[1, pl.ds(ATT_BLOCK, rows), :] = jnp.where(lo, krot, kn).astype(kd_scr.dtype)
    vd_scr[0, pl.ds(ATT_BLOCK, rows), :] = jnp.where(lo, v, vrot).astype(vd_scr.dtype)
    vd_scr[1, pl.ds(ATT_BLOCK, rows), :] = jnp.where(lo, vrot, v).astype(vd_scr.dtype)
    for c in range(conv_w // LANES):
        qn = _half_head_norm(q[:, c * LANES:(c + 1) * LANES], gq_ref[...], lo)
        qm_scr[2 * c] = jnp.where(lo, qn, 0.0).astype(qm_scr.dtype)
        qm_scr[2 * c + 1] = jnp.where(lo, 0.0, qn).astype(qm_scr.dtype)

    lo_b = lax.broadcasted_iota(jnp.int32, (ATT_BLOCK, LANES), 1) < HEAD_DIM
    h_next = _rms(xnext_ref[0], gmix_ref[...]).astype(jnp.bfloat16)
    next_groups = iter(range(in_w // MXU_COLS))
    ya_blocks = []
    for n in range(nblk):
        first = jnp.where(s == 0, 1, 0) if n == 0 else 0
        tiles = []
        for g in range(N_KV_HEADS):
            qs = jnp.concatenate([qm_scr[Q_PER_KV * g + j, pl.ds(n * ATT_BLOCK, ATT_BLOCK), :]
                                  for j in range(Q_PER_KV)], axis=0)
            kw = kd_scr[g, pl.ds(n * ATT_BLOCK, 2 * ATT_BLOCK), :]
            vw = vd_scr[g, pl.ds(n * ATT_BLOCK, 2 * ATT_BLOCK), :]
            sc = lax.dot_general(qs, kw, (((1,), (1,)), ((), ())), preferred_element_type=jnp.float32)
            logits = sc + bias_scr[first, g]
            ps, inv = [], []
            for j in range(Q_PER_KV):
                lj = logits[j * ATT_BLOCK:(j + 1) * ATT_BLOCK]
                sink = sinks_ref[Q_PER_KV * g + j]
                m = jnp.maximum(jnp.max(lj, axis=-1, keepdims=True), sink)
                p = jnp.exp(lj - m)
                den = jnp.sum(p, axis=-1, keepdims=True) + jnp.exp(sink - m)
                ps.append(p.astype(jnp.bfloat16))
                inv.append(1.0 / den)
            o = jnp.dot(jnp.concatenate(ps, axis=0), vw, preferred_element_type=jnp.float32)
            on = [o[j * ATT_BLOCK:(j + 1) * ATT_BLOCK] * inv[j] for j in range(Q_PER_KV)]
            tiles.append(jnp.where(lo_b, on[0], on[1]))
            tiles.append(jnp.where(lo_b, on[2], on[3]))
            j = next(next_groups, None)
            if j is not None:
                project(h_next, j)
        ya_blocks.append(jnp.concatenate(tiles, axis=-1))
    y_attn = jnp.concatenate(ya_blocks, axis=0)
    for j in next_groups:
        project(h_next, j)

    for g in range(N_KV_HEADS):
        kd_scr[g, pl.ds(0, ATT_BLOCK), :] = kd_scr[g, pl.ds(rows, ATT_BLOCK), :]
        vd_scr[g, pl.ds(0, ATT_BLOCK), :] = vd_scr[g, pl.ds(rows, ATT_BLOCK), :]

    mix = jnp.concatenate([_rms(y_conv, gc_ref[...]), _rms(y_attn, ga_ref[...])], axis=-1)
    x1 = (x_ref[0] + jnp.dot(mix.astype(jnp.bfloat16), wout_ref[...], preferred_element_type=jnp.float32)
          + bout_ref[...])
    _store_row_tiles(x1_ref, x1)
    h2 = _rms(x1, gffn_ref[...])
    _store_row_tiles(h2_ref, h2)
    h_hi = h2.astype(jnp.bfloat16)
    h_lo = (h2 - h_hi.astype(jnp.float32)).astype(jnp.bfloat16)
    wr = wrt_ref[...]
    w_hi = wr.astype(jnp.bfloat16)
    w_lo = (wr - w_hi.astype(jnp.float32)).astype(jnp.bfloat16)
    nt = (((1,), (1,)), ((), ()))
    lt = (lax.dot_general(w_hi, h_hi, nt, preferred_element_type=jnp.float32)
          + lax.dot_general(w_hi, h_lo, nt, preferred_element_type=jnp.float32)
          + lax.dot_general(w_lo, h_hi, nt, preferred_element_type=jnp.float32)) + br_ref[...]
    eid = lax.broadcasted_iota(jnp.int32, lt.shape, 0)
    vals, idxs, hits = [], [], []
    for _ in range(TOP_K):
        m = jnp.max(lt, axis=0, keepdims=True)
        idx = jnp.min(jnp.where(lt == m, eid, N_EXPERTS), axis=0, keepdims=True)
        hit = eid == idx
        vals.append(m)
        idxs.append(idx)
        hits.append(hit)
        lt = jnp.where(hit, -jnp.inf, lt)
    es = [jnp.exp(vv - vals[0]) for vv in vals]
    tot = es[0] + es[1] + es[2] + es[3]
    ids_ref[0] = jnp.concatenate(idxs, axis=0)
    gates_ref[0] = jnp.concatenate([e / tot for e in es], axis=0)

    _wprep_kernel(w1_ref, w2_ref, perm_ref, w1o_ref, w2o_ref)

    member = jnp.where(hits[0] | hits[1] | hits[2] | hits[3], 1.0, 0.0)
    before = jnp.dot(member.astype(jnp.bfloat16), tri_ref[...], preferred_element_type=jnp.float32)
    running = cnt_scr[...][:, 0:1]
    rank_e = before + running
    rank_ref[0] = jnp.concatenate([jnp.sum(jnp.where(hh, rank_e, 0.0), axis=0, keepdims=True) for hh in hits],
                                  axis=0).astype(jnp.int32)
    new_counts = cnt_scr[...] + jnp.sum(member, axis=1, keepdims=True)
    cnt_scr[...] = new_counts
    counts_ref[0] = new_counts.astype(jnp.int32)


def _mixer(x, p, bucket, chunk, w1, w2):
    n_e, _, two_ff = w1.shape
    d_ff, d_out = w2.shape[1], w2.shape[2]
    n_wtiles = n_e * d_ff // PREP_ROWS
    w1f = w1.reshape(n_e * w1.shape[1], two_ff)
    w2f = w2.reshape(n_e * d_ff, d_out)
    bsz, seq, d_model = x.shape
    rows = min(MIX_ROWS, seq)
    ns = seq // rows
    tiles_per_chunk = chunk // rows
    nchunks = bsz * seq // chunk
    conv_w = d_model // 2
    in_w = p["win"].shape[1]
    tri = jnp.asarray(np.triu(np.ones((rows, rows), np.float32), 1), jnp.bfloat16)
    n_tiles = bsz * ns
    full = lambda shape: pl.BlockSpec(shape, lambda i: (0,) * len(shape))
    smem = pl.BlockSpec(memory_space=pltpu.SMEM)
    tok_spec = pl.BlockSpec((1, rows, d_model), lambda i: (i // ns, i % ns, 0))
    next_spec = pl.BlockSpec((1, rows, d_model),
                             lambda i: (jnp.minimum(i + 1, n_tiles - 1) // ns, jnp.minimum(i + 1, n_tiles - 1) % ns, 0))
    tile_spec = pl.BlockSpec((rows * SUBLANES, LANES), lambda i: (i, 0))
    sel_spec = pl.BlockSpec((1, TOP_K, rows), lambda i: (i, 0, 0))
    cnt_spec = pl.BlockSpec((1, N_EXPERTS, LANES), lambda i: (i // tiles_per_chunk, 0, 0))
    t_total = bsz * seq
    sel_shape = (n_tiles, TOP_K, rows)
    wtile = lambda i: (jnp.minimum(i, n_wtiles - 1), 0)
    w1_spec = pl.BlockSpec((PREP_ROWS, two_ff), wtile)
    w2_spec = pl.BlockSpec((PREP_ROWS, d_out), wtile)
    outs = pl.pallas_call(
        functools.partial(_mixer_kernel, ns, tiles_per_chunk),
        grid=(n_tiles,),
        in_specs=[smem, smem, tok_spec, next_spec, full((1, d_model)), full((d_model, in_w)), full((1, in_w)),
                  full((CONV_K, conv_w)), full((1, LANES)), full((1, LANES)), full((ATT_BLOCK, 2 * ATT_BLOCK)),
                  full((1, conv_w)), full((1, conv_w)), full((d_model, d_model)), full((1, d_model)),
                  full((1, d_model)), full((N_EXPERTS, d_model)), full((N_EXPERTS, 1)), full((rows, rows)),
                  w1_spec, w2_spec, full((MXU_COLS, MXU_COLS))],
        out_specs=[tile_spec, tile_spec, sel_spec, sel_spec, sel_spec, cnt_spec, w1_spec, w2_spec],
        out_shape=[jax.ShapeDtypeStruct((t_total * SUBLANES, LANES), jnp.float32),
                   jax.ShapeDtypeStruct((t_total * SUBLANES, LANES), jnp.float32),
                   jax.ShapeDtypeStruct(sel_shape, jnp.int32),
                   jax.ShapeDtypeStruct(sel_shape, jnp.float32),
                   jax.ShapeDtypeStruct(sel_shape, jnp.int32),
                   jax.ShapeDtypeStruct((nchunks, N_EXPERTS, LANES), jnp.int32),
                   jax.ShapeDtypeStruct(w1f.shape, jnp.bfloat16),
                   jax.ShapeDtypeStruct(w2f.shape, jnp.bfloat16)],
        scratch_shapes=[pltpu.VMEM((N_KV_HEADS, ATT_BLOCK + rows, LANES), jnp.bfloat16),
                        pltpu.VMEM((N_KV_HEADS, ATT_BLOCK + rows, LANES), jnp.bfloat16),
                        pltpu.VMEM((SUBLANES, conv_w), jnp.float32),
                        pltpu.VMEM((2, N_KV_HEADS, Q_PER_KV * ATT_BLOCK, 2 * ATT_BLOCK), jnp.float32),
                        pltpu.VMEM((N_Q_HEADS, rows, LANES), jnp.bfloat16),
                        pltpu.VMEM((N_EXPERTS, LANES), jnp.float32),
                        pltpu.VMEM((rows, in_w), jnp.float32)],
        compiler_params=pltpu.CompilerParams(dimension_semantics=("arbitrary",),
                                             vmem_limit_bytes=VMEM_LIMIT_BYTES),
        name="mixer",
    )(p["sinks"], p["rel_bias"], x, x, p["gmix"], p["win"], p["bin"], p["convw"], p["gq2"], p["gk2"], bucket,
      p["gc"], p["ga"], p["wout"], p["bout"], p["gffn"], p["wrt"], p["br"], tri, w1f, w2f, _wprep_perm())
    if n_tiles >= n_wtiles:
        prepared = (outs[6].reshape(w1.shape), outs[7].reshape(w2.shape))
    else:
        prepared = _wprep(w1, w2)
    return tuple(outs[:6]) + prepared


def _moe_kernel(nbc_ref, bexp_ref, bfirst_ref, bslot_ref, bnext_ref, firstexp_ref,
                x1_hbm, h_hbm, dest_hbm, gate_hbm, fill_hbm, w1_hbm, w2_hbm, b1_ref, b2_ref,
                out_hbm,
                acc, hbuf, w1buf, w2buf, gbuf0, gbuf1, rbuf0, rbuf1, obuf, dest_s, gate_s, list_s,
                io_sem, w_sem, l_sem, o_sem):
    chunk = hbuf.shape[0] // SUBLANES - 1
    chunk_rows = chunk * SUBLANES
    n_assign = TOP_K * chunk
    pad_block = bexp_ref.shape[1] - 1
    d_model = w1buf.shape[1]
    d_ff = w2buf.shape[1]
    nk = d_model // LANES
    c = pl.program_id(0)
    row0 = pl.multiple_of(c * chunk_rows, SUBLANES)

    def weight_copies(e, slot):
        return (pltpu.make_async_copy(w1_hbm.at[e], w1buf.at[slot], w_sem.at[0, slot]),
                pltpu.make_async_copy(w2_hbm.at[e], w2buf.at[slot], w_sem.at[1, slot]))

    def chunk_in_copies():
        return (pltpu.make_async_copy(x1_hbm.at[pl.ds(row0, chunk_rows)], acc.at[pl.ds(0, chunk_rows)], io_sem.at[0]),
                pltpu.make_async_copy(h_hbm.at[pl.ds(row0, chunk_rows)], hbuf.at[pl.ds(0, chunk_rows)], io_sem.at[1]))

    def list_copies():
        return (pltpu.make_async_copy(dest_hbm.at[c], dest_s, l_sem.at[0]),
                pltpu.make_async_copy(gate_hbm.at[c], gate_s.at[:, pl.ds(0, n_assign)], l_sem.at[1]),
                pltpu.make_async_copy(fill_hbm, list_s, l_sem.at[2]))

    for cp in list_copies() + chunk_in_copies():
        cp.start()

    @pl.when(c == 0)
    def _first_weights():
        for cp in weight_copies(firstexp_ref[0], 0):
            cp.start()

    gate_s[0, n_assign] = jnp.float32(0.0)
    for cp in list_copies():
        cp.wait()

    def invert_body(i, carry):
        for jj in range(INVERT_UNROLL):
            a = i * INVERT_UNROLL + jj
            list_s[0, dest_s[0, a]] = a
        return carry

    lax.fori_loop(0, n_assign // INVERT_UNROLL, invert_body, 0)

    acc[pl.ds(chunk_rows, SUBLANES), :] = jnp.zeros((SUBLANES, LANES), jnp.float32)
    hbuf[pl.ds(chunk_rows, SUBLANES), :] = jnp.zeros((SUBLANES, LANES), jnp.float32)
    rbuf1[...] = jnp.zeros(rbuf1.shape, rbuf1.dtype)
    for cp in chunk_in_copies():
        cp.wait()

    def token_rows(a):
        return pl.ds(pl.multiple_of((a >> TOP_K_SHIFT) * SUBLANES, SUBLANES), SUBLANES)

    def weights_step(g):
        @pl.when(bfirst_ref[c, g] == 1)
        def _():
            slot = bslot_ref[c, g]
            for cp in weight_copies(bexp_ref[c, g], slot):
                cp.wait()
            nxt = bnext_ref[c, g]

            @pl.when(nxt >= 0)
            def _():
                for cp in weight_copies(nxt, jnp.where(slot == WEIGHT_SLOTS - 1, 0, slot + 1)):
                    cp.start()

    def gather_rows(g, gbuf, j_lo, j_hi):
        s0 = g * MOE_ROWS
        for j in range(j_lo, j_hi):
            gbuf[pl.ds(j * SUBLANES, SUBLANES), :] = hbuf[token_rows(list_s[0, s0 + j]), :]

    def scatter_rows(g, rbuf, j_lo, j_hi):
        s0 = g * MOE_ROWS
        for j0 in range(j_lo, j_hi, SCATTER_UNROLL):
            dsts, vals = [], []
            for j in range(j0, j0 + SCATTER_UNROLL):
                a = list_s[0, s0 + j]
                dst = token_rows(a)
                dsts.append(dst)
                vals.append(acc[dst, :] + gate_s[0, a] * rbuf[pl.ds(j * SUBLANES, SUBLANES), :])
            for dst, val in zip(dsts, vals):
                acc[dst, :] = val

    n_hid = 2 * d_ff // MXU_COLS
    n_out = d_model // MXU_COLS
    half = MOE_ROWS // 2
    rows_hid = half // n_hid
    rows_out = half // n_out

    def stage(g_gather, gbuf_in, g_compute, gbuf, rbuf, g_scatter, rbuf_out):
        e = bexp_ref[c, g_compute]
        slot = bslot_ref[c, g_compute]
        xs = [gbuf[pl.ds(kk, MOE_ROWS, stride=SUBLANES), :] for kk in range(nk)]
        xb = jnp.concatenate(xs, axis=-1).astype(jnp.bfloat16)
        acts = []
        for cg in range(n_hid):
            cols = pl.ds(cg * MXU_COLS, MXU_COLS)
            hid = jnp.dot(xb, w1buf[slot, :, cols], preferred_element_type=jnp.float32) + b1_ref[e, :, cols]
            glu = jnp.minimum(hid[:, :LANES], SWIGLU_LIMIT)
            lin = jnp.clip(hid[:, LANES:], -SWIGLU_LIMIT, SWIGLU_LIMIT)
            acts.append(glu * jax.nn.sigmoid(SWIGLU_ALPHA * glu) * (lin + 1.0))
            gather_rows(g_gather, gbuf_in, cg * rows_hid, (cg + 1) * rows_hid)
            scatter_rows(g_scatter, rbuf_out, cg * rows_hid, (cg + 1) * rows_hid)
        act = jnp.concatenate(acts, axis=-1).astype(jnp.bfloat16)
        for og in range(n_out):
            cols = pl.ds(og * MXU_COLS, MXU_COLS)
            y = jnp.dot(act, w2buf[slot, :, cols], preferred_element_type=jnp.float32) + b2_ref[e, :, cols]
            for kk in range(MXU_COLS // LANES):
                rbuf[pl.ds(og * (MXU_COLS // LANES) + kk, MOE_ROWS, stride=SUBLANES), :] = (
                    y[:, kk * LANES:(kk + 1) * LANES])
            gather_rows(g_gather, gbuf_in, half + og * rows_out, half + (og + 1) * rows_out)
            scatter_rows(g_scatter, rbuf_out, half + og * rows_out, half + (og + 1) * rows_out)

    gather_rows(0, gbuf0, 0, MOE_ROWS)

    def pair_body(p, carry):
        g0 = 2 * p
        @pl.when(bfirst_ref[c, g0] + bfirst_ref[c, g0 + 1] > 0)
        def _():
            weights_step(g0)
            weights_step(g0 + 1)

        stage(g0 + 1, gbuf1, g0, gbuf0, rbuf0, jnp.where(p == 0, pad_block, g0 - 1), rbuf1)
        stage(g0 + 2, gbuf0, g0 + 1, gbuf1, rbuf1, g0, rbuf0)
        return carry

    npairs = (nbc_ref[c] + 1) // 2
    lax.fori_loop(0, npairs, pair_body, 0)
    scatter_rows(2 * npairs - 1, rbuf1, 0, MOE_ROWS)

    def out_copy(grp, buf):
        return pltpu.make_async_copy(obuf.at[buf], out_hbm.at[pl.ds(c * chunk + grp * OUT_ROWS, OUT_ROWS)],
                                     o_sem.at[buf])

    def out_body(gp, carry):
        for buf in range(2):
            grp = 2 * gp + buf

            @pl.when(gp > 0)
            def _():
                out_copy(grp - 2, buf).wait()

            first = grp * (OUT_ROWS * SUBLANES)
            obuf[buf] = jnp.concatenate(
                [acc[pl.ds(first + kk, OUT_ROWS, stride=SUBLANES), :] for kk in range(nk)], axis=-1)
            out_copy(grp, buf).start()
        return carry

    n_groups = chunk // OUT_ROWS
    lax.fori_loop(0, n_groups // 2, out_body, 0)
    for buf in range(2):
        out_copy(n_groups - 2 + buf, buf).wait()


def _block_tables(nblk, table_blocks):
    nchunks = nblk.shape[0]
    bend = jnp.cumsum(nblk, axis=1)
    nbc = bend[:, -1]
    g = jnp.arange(table_blocks, dtype=jnp.int32)
    nonempty = nblk > 0
    last_exp = jnp.max(jnp.where(nonempty, jnp.arange(N_EXPERTS, dtype=jnp.int32)[None, :], 0), axis=1)
    bexp = jnp.sum((g[None, :, None] >= bend[:, None, :]).astype(jnp.int32), axis=-1)
    bexp = jnp.where(g[None, :] >= nbc[:, None], last_exp[:, None], bexp)
    bstart = bend - nblk
    is_exp = bexp[:, :, None] == jnp.arange(N_EXPERTS, dtype=jnp.int32)[None, None, :]
    per_block = lambda table: jnp.sum(jnp.where(is_exp, table[:, None, :], 0), axis=-1)
    bfirst = ((g[None, :] == per_block(bstart)) & (g[None, :] < nbc[:, None])).astype(jnp.int32)
    flat = nonempty.reshape(-1)
    pair_slot = ((jnp.cumsum(flat.astype(jnp.int32)) - 1) % WEIGHT_SLOTS).reshape(nchunks, N_EXPERTS)
    n_pairs = flat.shape[0]
    idx = jnp.where(flat, jnp.arange(n_pairs, dtype=jnp.int32), n_pairs)
    later = jnp.concatenate([lax.cummin(idx[::-1])[::-1][1:], jnp.full((1,), n_pairs, jnp.int32)])
    pair_next = jnp.where(later < n_pairs, later % N_EXPERTS, -1).reshape(nchunks, N_EXPERTS)
    bslot = per_block(pair_slot)
    bnext = per_block(pair_next)
    first_exp = (jnp.min(idx) % N_EXPERTS).reshape(1)
    return nbc.astype(jnp.int32), bexp, bfirst, bslot.astype(jnp.int32), bnext.astype(jnp.int32), first_exp


def _moe(x1r, hr, dest, gates, tables, w1, w2, b1, b2, chunk, table_blocks):
    t_total = x1r.shape[0] // SUBLANES
    nchunks = t_total // chunk
    d_model = w1.shape[1]
    d_ff = w2.shape[1]
    n_assign = TOP_K * chunk
    n_slots = table_blocks * MOE_ROWS
    fill = jnp.full((1, n_slots), n_assign, jnp.int32)
    anyspace = pl.BlockSpec(memory_space=pl.ANY)
    grid_spec = pltpu.PrefetchScalarGridSpec(
        num_scalar_prefetch=6,
        grid=(nchunks,),
        in_specs=[anyspace] * 7 + [pl.BlockSpec((N_EXPERTS, 1, 2 * d_ff), lambda c, *_: (0, 0, 0)),
                                   pl.BlockSpec((N_EXPERTS, 1, d_model), lambda c, *_: (0, 0, 0))],
        out_specs=anyspace,
        scratch_shapes=[pltpu.VMEM(((chunk + 1) * SUBLANES, LANES), jnp.float32),
                        pltpu.VMEM(((chunk + 1) * SUBLANES, LANES), jnp.float32),
                        pltpu.VMEM((WEIGHT_SLOTS, d_model, 2 * d_ff), jnp.bfloat16),
                        pltpu.VMEM((WEIGHT_SLOTS, d_ff, d_model), jnp.bfloat16),
                        pltpu.VMEM((MOE_ROWS * SUBLANES, LANES), jnp.float32),
                        pltpu.VMEM((MOE_ROWS * SUBLANES, LANES), jnp.float32),
                        pltpu.VMEM((MOE_ROWS * SUBLANES, LANES), jnp.float32),
                        pltpu.VMEM((MOE_ROWS * SUBLANES, LANES), jnp.float32),
                        pltpu.VMEM((2, OUT_ROWS, d_model), jnp.float32),
                        pltpu.SMEM((1, n_assign), jnp.int32),
                        pltpu.SMEM((1, n_assign + LANES), jnp.float32),
                        pltpu.SMEM((1, n_slots), jnp.int32),
                        pltpu.SemaphoreType.DMA((2,)),
                        pltpu.SemaphoreType.DMA((2, WEIGHT_SLOTS)),
                        pltpu.SemaphoreType.DMA((3,)),
                        pltpu.SemaphoreType.DMA((2,))],
    )
    return pl.pallas_call(
        _moe_kernel,
        grid_spec=grid_spec,
        out_shape=jax.ShapeDtypeStruct((t_total, d_model), jnp.float32),
        compiler_params=pltpu.CompilerParams(dimension_semantics=("arbitrary",),
                                             vmem_limit_bytes=VMEM_LIMIT_BYTES),
        name="moe",
    )(*tables, x1r, hr, dest, gates, fill, w1, w2, b1, b2)


def _t5_bucket_table():
    i = np.arange(ATT_BLOCK)[:, None]
    j = np.arange(2 * ATT_BLOCK)[None, :]
    rel = i + ATT_BLOCK - j
    max_exact = N_BUCKETS // 2
    nf = np.maximum(rel, 1).astype(np.float32)
    large = max_exact + (np.log(nf / max_exact) / np.float32(math.log(MAX_DISTANCE / max_exact))
                         * (N_BUCKETS - max_exact)).astype(np.int32)
    large = np.minimum(large, N_BUCKETS - 1)
    bucket = np.where(rel < max_exact, rel, large)
    return np.where((rel >= 0) & (rel < WINDOW), bucket, -1).astype(np.int32)


def _token_major(a, nchunks):
    return jnp.transpose(a, (0, 2, 1)).reshape(nchunks, 1, -1)


def kernel(x, norm_mix, w_in, b_in, conv_w, q_norm, k_norm, sinks, rel_bias, out_norm_conv, out_norm_attn,
           w_out, b_out, norm_ffn, w_router, b_router, w1, b1, w2, b2):
    bsz, seq, d_model = x.shape
    depth = w_in.shape[0]
    bf16 = jnp.bfloat16
    bucket = jnp.asarray(_t5_bucket_table())
    t_total = bsz * seq
    chunk = min(MOE_CHUNK, t_total)
    nchunks = t_total // chunk
    table_blocks = chunk * TOP_K // MOE_ROWS + N_EXPERTS + 2
    for l in range(depth):
        d_ff = w2.shape[2]
        params = dict(
            sinks=sinks[l], rel_bias=rel_bias, gmix=norm_mix[l][None], win=w_in[l].astype(bf16), bin=b_in[l][None],
            convw=conv_w[l], gq2=jnp.tile(q_norm[l], 2)[None] * (HEAD_DIM ** -0.5), gk2=jnp.tile(k_norm[l], 2)[None],
            gc=out_norm_conv[l][None], ga=out_norm_attn[l][None], wout=w_out[l].astype(bf16), bout=b_out[l][None],
            gffn=norm_ffn[l][None], wrt=w_router[l].T, br=b_router[l][:, None])
        x1r, hr, ids_t, gates_t, rank_t, counts_t, w1p, w2p = _mixer(x, params, bucket, chunk, w1[l], w2[l])
        counts = counts_t[:, :, 0]
        padded = (counts + MOE_ROWS - 1) // MOE_ROWS * MOE_ROWS
        pad_starts = jnp.cumsum(padded, axis=1) - padded
        tables = _block_tables(padded // MOE_ROWS, table_blocks)
        tiles_per_chunk = ids_t.shape[0] // nchunks
        tile_starts = jnp.repeat(pad_starts, tiles_per_chunk, axis=0)
        dest_t = rank_t
        for e in range(N_EXPERTS):
            dest_t = dest_t + jnp.where(ids_t == e, tile_starts[:, e:e + 1, None], 0)
        b1p = jnp.transpose(b1[l].reshape(N_EXPERTS, 2 * d_ff // MXU_COLS, LANES, 2), (0, 1, 3, 2))
        b1p = b1p.reshape(N_EXPERTS, 1, 2 * d_ff)
        out = _moe(x1r, hr, _token_major(dest_t, nchunks), _token_major(gates_t, nchunks), tables, w1p, w2p, b1p,
                   b2[l][:, None, :], chunk, table_blocks)
        x = out.reshape(bsz, seq, d_model)
    return x
```

```python
import functools
import math

import jax
import jax.numpy as jnp
import numpy as np
from jax import lax
from jax.experimental import pallas as pl
from jax.experimental.pallas import tpu as pltpu

HEAD_DIM = 64
N_Q_HEADS = 8
N_KV_HEADS = 2
Q_PER_KV = N_Q_HEADS // N_KV_HEADS
CONV_K = 3
WINDOW = 128
ATT_BLOCK = 128
N_BUCKETS = 32
MAX_DISTANCE = 128
N_EXPERTS = 32
TOP_K = 4
TOP_K_SHIFT = 2
SWIGLU_LIMIT = 7.0
SWIGLU_ALPHA = 1.702
EPS = 1e-5
MASK_VALUE = -1e30

LANES = 128
SUBLANES = 8
MXU_COLS = 256
VMEM_LIMIT_BYTES = 56 * 1024 * 1024

MIX_ROWS = 512
MOE_CHUNK = 4096
MOE_ROWS = 128
WEIGHT_SLOTS = 3
OUT_ROWS = 128
PREP_ROWS = 256
INVERT_UNROLL = 16
SCATTER_UNROLL = 4


def _rms(x, g):
    return x * lax.rsqrt(jnp.mean(x * x, axis=-1, keepdims=True) + EPS) * g


def _half_head_norm(t, gain2, lo):
    t2 = t * t
    s_lo = jnp.sum(jnp.where(lo, t2, 0.0), axis=-1, keepdims=True)
    s_hi = jnp.sum(jnp.where(lo, 0.0, t2), axis=-1, keepdims=True)
    r = jnp.where(lo, lax.rsqrt(s_lo * (1.0 / HEAD_DIM) + EPS), lax.rsqrt(s_hi * (1.0 / HEAD_DIM) + EPS))
    return t * r * gain2


def _store_row_tiles(ref, val):
    rows = val.shape[0]
    for kk in range(val.shape[1] // LANES):
        ref[pl.ds(kk, rows, stride=SUBLANES), :] = val[:, kk * LANES:(kk + 1) * LANES]


def _wprep_kernel(w1_ref, w2_ref, perm_ref, w1o_ref, w2o_ref):
    perm = perm_ref[...]
    for c in range(w1_ref.shape[1] // MXU_COLS):
        cols = pl.ds(c * MXU_COLS, MXU_COLS)
        t = w1_ref[:, cols].astype(jnp.bfloat16)
        w1o_ref[:, cols] = jnp.dot(t, perm, preferred_element_type=jnp.float32).astype(jnp.bfloat16)
    w2o_ref[...] = w2_ref[...].astype(jnp.bfloat16)


def _wprep_perm():
    j = np.arange(MXU_COLS)
    src = np.where(j < LANES, 2 * j, 2 * (j - LANES) + 1)
    return jnp.asarray((np.arange(MXU_COLS)[:, None] == src[None, :]).astype(np.float32), jnp.bfloat16)


def _wprep(w1, w2):
    n_e, d_model, two_ff = w1.shape
    d_ff, d_out = w2.shape[1], w2.shape[2]
    assert d_model == d_ff, "one row grid serves both weight tensors"
    w1o, w2o = pl.pallas_call(
        _wprep_kernel,
        grid=(n_e * d_model // PREP_ROWS,),
        in_specs=[pl.BlockSpec((PREP_ROWS, two_ff), lambda r: (r, 0)),
                  pl.BlockSpec((PREP_ROWS, d_out), lambda r: (r, 0)),
                  pl.BlockSpec((MXU_COLS, MXU_COLS), lambda r: (0, 0))],
        out_specs=[pl.BlockSpec((PREP_ROWS, two_ff), lambda r: (r, 0)),
                   pl.BlockSpec((PREP_ROWS, d_out), lambda r: (r, 0))],
        out_shape=[jax.ShapeDtypeStruct((n_e * d_model, two_ff), jnp.bfloat16),
                   jax.ShapeDtypeStruct((n_e * d_ff, d_out), jnp.bfloat16)],
        compiler_params=pltpu.CompilerParams(dimension_semantics=("arbitrary",)),
        name="wprep",
    )(w1.reshape(n_e * d_model, two_ff), w2.reshape(n_e * d_ff, d_out), _wprep_perm())
    return w1o.reshape(w1.shape), w2o.reshape(w2.shape)


def _mixer_kernel(tiles_per_seq, tiles_per_chunk,
                  sinks_ref, relb_ref,
                  x_ref, xnext_ref, gmix_ref, win_ref, bin_ref, convw_ref, gq_ref, gk_ref, bucket_ref,
                  gc_ref, ga_ref, wout_ref, bout_ref, gffn_ref, wrt_ref, br_ref, tri_ref,
                  w1_ref, w2_ref, perm_ref,
                  x1_ref, h2_ref, ids_ref, gates_ref, rank_ref, counts_ref, w1o_ref, w2o_ref,
                  kd_scr, vd_scr, cu_scr, bias_scr, qm_scr, cnt_scr, proj_scr):
    rows = x_ref.shape[1]
    d_model = x_ref.shape[2]
    conv_w = d_model // 2
    in_w = win_ref.shape[1]
    nblk = rows // ATT_BLOCK
    tile = pl.program_id(0)
    s = lax.rem(tile, tiles_per_seq)

    def project(h, j):
        cols = pl.ds(j * MXU_COLS, MXU_COLS)
        proj_scr[:, cols] = jnp.dot(h, win_ref[:, cols], preferred_element_type=jnp.float32) + bin_ref[:, cols]

    @pl.when(tile == 0)
    def _first_projection():
        h0 = _rms(x_ref[0], gmix_ref[...]).astype(jnp.bfloat16)
        for j in range(in_w // MXU_COLS):
            project(h0, j)

    @pl.when(tile == 0)
    def _build_bias():
        bk = bucket_ref[...]
        col = lax.broadcasted_iota(jnp.int32, bk.shape, 1)
        accs = [jnp.full(bk.shape, MASK_VALUE, jnp.float32) for _ in range(N_Q_HEADS)]
        for bb in range(N_BUCKETS):
            hit = bk == bb
            for h in range(N_Q_HEADS):
                accs[h] = jnp.where(hit, relb_ref[bb, h], accs[h])
        for h in range(N_Q_HEADS):
            g, j = divmod(h, Q_PER_KV)
            bias_scr[0, g, pl.ds(j * ATT_BLOCK, ATT_BLOCK), :] = accs[h]
            bias_scr[1, g, pl.ds(j * ATT_BLOCK, ATT_BLOCK), :] = jnp.where(col < ATT_BLOCK, MASK_VALUE, accs[h])

    @pl.when(s == 0)
    def _reset_state():
        kd_scr[:, pl.ds(0, ATT_BLOCK), :] = jnp.zeros((N_KV_HEADS, ATT_BLOCK, LANES), kd_scr.dtype)
        vd_scr[:, pl.ds(0, ATT_BLOCK), :] = jnp.zeros((N_KV_HEADS, ATT_BLOCK, LANES), vd_scr.dtype)
        cu_scr[...] = jnp.zeros(cu_scr.shape, cu_scr.dtype)

    @pl.when(lax.rem(tile, tiles_per_chunk) == 0)
    def _reset_counts():
        cnt_scr[...] = jnp.zeros(cnt_scr.shape, cnt_scr.dtype)

    b_gate = proj_scr[:, 0:conv_w]
    c_gate = proj_scr[:, conv_w:2 * conv_w]
    u = proj_scr[:, 2 * conv_w:3 * conv_w]
    q = proj_scr[:, 3 * conv_w:4 * conv_w]
    k = proj_scr[:, 4 * conv_w:4 * conv_w + LANES]
    v = proj_scr[:, 4 * conv_w + LANES:4 * conv_w + 2 * LANES]

    cu = c_gate * u
    row = lax.broadcasted_iota(jnp.int32, cu.shape, 0)
    prev2 = cu_scr[SUBLANES - 2:SUBLANES - 1, :]
    prev1 = cu_scr[SUBLANES - 1:SUBLANES, :]
    r1 = jnp.where(row == 0, prev1, pltpu.roll(cu, 1, 0))
    r2 = jnp.where(row == 0, prev2, jnp.where(row == 1, prev1, pltpu.roll(cu, 2, 0)))
    y_conv = b_gate * (convw_ref[2:3, :] * cu + convw_ref[1:2, :] * r1 + convw_ref[0:1, :] * r2)
    cu_scr[...] = cu[rows - SUBLANES:rows, :]

    lane = lax.broadcasted_iota(jnp.int32, (rows, LANES), 1)
    lo = lane < HEAD_DIM
    kn = _half_head_norm(k, gk_ref[...], lo)
    krot = pltpu.roll(kn, HEAD_DIM, 1)
    vrot = pltpu.roll(v, HEAD_DIM, 1)
    kd_scr[0, pl.ds(ATT_BLOCK, rows), :] = jnp.where(lo, kn, krot).astype(kd_scr.dtype)
    kd_scr[1, pl.ds(ATT_BLOCK, rows), :] = jnp.where(lo, krot, kn).astype(kd_scr.dtype)
    vd_scr[0, pl.ds(ATT_BLOCK, rows), :] = jnp.where(lo, v, vrot).astype(vd_scr.dtype)
    vd_scr[1, pl.ds(ATT_BLOCK, rows), :] = jnp.where(lo, vrot, v).astype(vd_scr.dtype)
    for c in range(conv_w // LANES):
        qn = _half_head_norm(q[:, c * LANES:(c + 1) * LANES], gq_ref[...], lo)
        qm_scr[2 * c] = jnp.where(lo, qn, 0.0).astype(qm_scr.dtype)
        qm_scr[2 * c + 1] = jnp.where(lo, 0.0, qn).astype(qm_scr.dtype)

    lo_b = lax.broadcasted_iota(jnp.int32, (ATT_BLOCK, LANES), 1) < HEAD_DIM
    h_next = _rms(xnext_ref[0], gmix_ref[...]).astype(jnp.bfloat16)
    next_groups = iter(range(in_w // MXU_COLS))
    ya_blocks = []
    for n in range(nblk):
        first = jnp.where(s == 0, 1, 0) if n == 0 else 0
        tiles = []
        for g in range(N_KV_HEADS):
            qs = jnp.concatenate([qm_scr[Q_PER_KV * g + j, pl.ds(n * ATT_BLOCK, ATT_BLOCK), :]
                                  for j in range(Q_PER_KV)], axis=0)
            kw = kd_scr[g, pl.ds(n * ATT_BLOCK, 2 * ATT_BLOCK), :]
            vw = vd_scr[g, pl.ds(n * ATT_BLOCK, 2 * ATT_BLOCK), :]
            sc = lax.dot_general(qs, kw, (((1,), (1,)), ((), ())), preferred_element_type=jnp.float32)
            logits = sc + bias_scr[first, g]
            ps, inv = [], []
            for j in range(Q_PER_KV):
                lj = logits[j * ATT_BLOCK:(j + 1) * ATT_BLOCK]
                sink = sinks_ref[Q_PER_KV * g + j]
                m = jnp.maximum(jnp.max(lj, axis=-1, keepdims=True), sink)
                p = jnp.exp(lj - m)
                den = jnp.sum(p, axis=-1, keepdims=True) + jnp.exp(sink - m)
                ps.append(p.astype(jnp.bfloat16))
                inv.append(1.0 / den)
            o = jnp.dot(jnp.concatenate(ps, axis=0), vw, preferred_element_type=jnp.float32)
            on = [o[j * ATT_BLOCK:(j + 1) * ATT_BLOCK] * inv[j] for j in range(Q_PER_KV)]
            tiles.append(jnp.where(lo_b, on[0], on[1]))
            tiles.append(jnp.where(lo_b, on[2], on[3]))
            j = next(next_groups, None)
            if j is not None:
                project(h_next, j)
        ya_blocks.append(jnp.concatenate(tiles, axis=-1))
    y_attn = jnp.concatenate(ya_blocks, axis=0)
    for j in next_groups:
        project(h_next, j)

    for g in range(N_KV_HEADS):
        kd_scr[g, pl.ds(0, ATT_BLOCK), :] = kd_scr[g, pl.ds(rows, ATT_BLOCK), :]
        vd_scr[g, pl.ds(0, ATT_BLOCK), :] = vd_scr[g, pl.ds(rows, ATT_BLOCK), :]

    mix = jnp.concatenate([_rms(y_conv, gc_ref[...]), _rms(y_attn, ga_ref[...])], axis=-1)
    x1 = (x_ref[0] + jnp.dot(mix.astype(jnp.bfloat16), wout_ref[...], preferred_element_type=jnp.float32)
          + bout_ref[...])
    _store_row_tiles(x1_ref, x1)
    h2 = _rms(x1, gffn_ref[...])
    _store_row_tiles(h2_ref, h2)
    h_hi = h2.astype(jnp.bfloat16)
    h_lo = (h2 - h_hi.astype(jnp.float32)).astype(jnp.bfloat16)
    wr = wrt_ref[...]
    w_hi = wr.astype(jnp.bfloat16)
    w_lo = (wr - w_hi.astype(jnp.float32)).astype(jnp.bfloat16)
    nt = (((1,), (1,)), ((), ()))
    lt = (lax.dot_general(w_hi, h_hi, nt, preferred_element_type=jnp.float32)
          + lax.dot_general(w_hi, h_lo, nt, preferred_element_type=jnp.float32)
          + lax.dot_general(w_lo, h_hi, nt, preferred_element_type=jnp.float32)) + br_ref[...]
    eid = lax.broadcasted_iota(jnp.int32, lt.shape, 0)
    vals, idxs, hits = [], [], []
    for _ in range(TOP_K):
        m = jnp.max(lt, axis=0, keepdims=True)
        idx = jnp.min(jnp.where(lt == m, eid, N_EXPERTS), axis=0, keepdims=True)
        hit = eid == idx
        vals.append(m)
        idxs.append(idx)
        hits.append(hit)
        lt = jnp.where(hit, -jnp.inf, lt)
    es = [jnp.exp(vv - vals[0]) for vv in vals]
    tot = es[0] + es[1] + es[2] + es[3]
    ids_ref[0] = jnp.concatenate(idxs, axis=0)
    gates_ref[0] = jnp.concatenate([e / tot for e in es], axis=0)

    _wprep_kernel(w1_ref, w2_ref, perm_ref, w1o_ref, w2o_ref)

    member = jnp.where(hits[0] | hits[1] | hits[2] | hits[3], 1.0, 0.0)
    before = jnp.dot(member.astype(jnp.bfloat16), tri_ref[...], preferred_element_type=jnp.float32)
    running = cnt_scr[...][:, 0:1]
    rank_e = before + running
    rank_ref[0] = jnp.concatenate([jnp.sum(jnp.where(hh, rank_e, 0.0), axis=0, keepdims=True) for hh in hits],
                                  axis=0).astype(jnp.int32)
    new_counts = cnt_scr[...] + jnp.sum(member, axis=1, keepdims=True)
    cnt_scr[...] = new_counts
    counts_ref[0] = new_counts.astype(jnp.int32)


def _mixer(x, p, bucket, chunk, w1, w2):
    n_e, _, two_ff = w1.shape
    d_ff, d_out = w2.shape[1], w2.shape[2]
    n_wtiles = n_e * d_ff // PREP_ROWS
    w1f = w1.reshape(n_e * w1.shape[1], two_ff)
    w2f = w2.reshape(n_e * d_ff, d_out)
    bsz, seq, d_model = x.shape
    rows = min(MIX_ROWS, seq)
    ns = seq // rows
    tiles_per_chunk = chunk // rows
    nchunks = bsz * seq // chunk
    conv_w = d_model // 2
    in_w = p["win"].shape[1]
    tri = jnp.asarray(np.triu(np.ones((rows, rows), np.float32), 1), jnp.bfloat16)
    n_tiles = bsz * ns
    full = lambda shape: pl.BlockSpec(shape, lambda i: (0,) * len(shape))
    smem = pl.BlockSpec(memory_space=pltpu.SMEM)
    tok_spec = pl.BlockSpec((1, rows, d_model), lambda i: (i // ns, i % ns, 0))
    next_spec = pl.BlockSpec((1, rows, d_model),
                             lambda i: (jnp.minimum(i + 1, n_tiles - 1) // ns, jnp.minimum(i + 1, n_tiles - 1) % ns, 0))
    tile_spec = pl.BlockSpec((rows * SUBLANES, LANES), lambda i: (i, 0))
    sel_spec = pl.BlockSpec((1, TOP_K, rows), lambda i: (i, 0, 0))
    cnt_spec = pl.BlockSpec((1, N_EXPERTS, LANES), lambda i: (i // tiles_per_chunk, 0, 0))
    t_total = bsz * seq
    sel_shape = (n_tiles, TOP_K, rows)
    wtile = lambda i: (jnp.minimum(i, n_wtiles - 1), 0)
    w1_spec = pl.BlockSpec((PREP_ROWS, two_ff), wtile)
    w2_spec = pl.BlockSpec((PREP_ROWS, d_out), wtile)
    outs = pl.pallas_call(
        functools.partial(_mixer_kernel, ns, tiles_per_chunk),
        grid=(n_tiles,),
        in_specs=[smem, smem, tok_spec, next_spec, full((1, d_model)), full((d_model, in_w)), full((1, in_w)),
                  full((CONV_K, conv_w)), full((1, LANES)), full((1, LANES)), full((ATT_BLOCK, 2 * ATT_BLOCK)),
                  full((1, conv_w)), full((1, conv_w)), full((d_model, d_model)), full((1, d_model)),
                  full((1, d_model)), full((N_EXPERTS, d_model)), full((N_EXPERTS, 1)), full((rows, rows)),
                  w1_spec, w2_spec, full((MXU_COLS, MXU_COLS))],
        out_specs=[tile_spec, tile_spec, sel_spec, sel_spec, sel_spec, cnt_spec, w1_spec, w2_spec],
        out_shape=[jax.ShapeDtypeStruct((t_total * SUBLANES, LANES), jnp.float32),
                   jax.ShapeDtypeStruct((t_total * SUBLANES, LANES), jnp.float32),
                   jax.ShapeDtypeStruct(sel_shape, jnp.int32),
                   jax.ShapeDtypeStruct(sel_shape, jnp.float32),
                   jax.ShapeDtypeStruct(sel_shape, jnp.int32),
                   jax.ShapeDtypeStruct((nchunks, N_EXPERTS, LANES), jnp.int32),
                   jax.ShapeDtypeStruct(w1f.shape, jnp.bfloat16),
                   jax.ShapeDtypeStruct(w2f.shape, jnp.bfloat16)],
        scratch_shapes=[pltpu.VMEM((N_KV_HEADS, ATT_BLOCK + rows, LANES), jnp.bfloat16),
                        pltpu.VMEM((N_KV_HEADS, ATT_BLOCK + rows, LANES), jnp.bfloat16),
                        pltpu.VMEM((SUBLANES, conv_w), jnp.float32),
                        pltpu.VMEM((2, N_KV_HEADS, Q_PER_KV * ATT_BLOCK, 2 * ATT_BLOCK), jnp.float32),
                        pltpu.VMEM((N_Q_HEADS, rows, LANES), jnp.bfloat16),
                        pltpu.VMEM((N_EXPERTS, LANES), jnp.float32),
                        pltpu.VMEM((rows, in_w), jnp.float32)],
        compiler_params=pltpu.CompilerParams(dimension_semantics=("arbitrary",),
                                             vmem_limit_bytes=VMEM_LIMIT_BYTES),
        name="mixer",
    )(p["sinks"], p["rel_bias"], x, x, p["gmix"], p["win"], p["bin"], p["convw"], p["gq2"], p["gk2"], bucket,
      p["gc"], p["ga"], p["wout"], p["bout"], p["gffn"], p["wrt"], p["br"], tri, w1f, w2f, _wprep_perm())
    if n_tiles >= n_wtiles:
        prepared = (outs[6].reshape(w1.shape), outs[7].reshape(w2.shape))
    else:
        prepared = _wprep(w1, w2)
    return tuple(outs[:6]) + prepared


def _moe_kernel(nbc_ref, bexp_ref, bfirst_ref, bslot_ref, bnext_ref, firstexp_ref,
                x1_hbm, h_hbm, dest_hbm, gate_hbm, fill_hbm, w1_hbm, w2_hbm, b1_ref, b2_ref,
                out_hbm,
                acc, hbuf, w1buf, w2buf, gbuf0, gbuf1, rbuf0, rbuf1, obuf, dest_s, gate_s, list_s,
                io_sem, w_sem, l_sem, o_sem):
    chunk = hbuf.shape[0] // SUBLANES - 1
    chunk_rows = chunk * SUBLANES
    n_assign = TOP_K * chunk
    pad_block = bexp_ref.shape[1] - 1
    d_model = w1buf.shape[1]
    d_ff = w2buf.shape[1]
    nk = d_model // LANES
    c = pl.program_id(0)
    row0 = pl.multiple_of(c * chunk_rows, SUBLANES)

    def weight_copies(e, slot):
        return (pltpu.make_async_copy(w1_hbm.at[e], w1buf.at[slot], w_sem.at[0, slot]),
                pltpu.make_async_copy(w2_hbm.at[e], w2buf.at[slot], w_sem.at[1, slot]))

    def x1_copy():
        return pltpu.make_async_copy(x1_hbm.at[pl.ds(row0, chunk_rows)], acc.at[pl.ds(0, chunk_rows)], io_sem.at[0])

    def h_copy(cc):
        rows = pl.ds(pl.multiple_of(cc * chunk_rows, SUBLANES), chunk_rows)
        return pltpu.make_async_copy(h_hbm.at[rows], hbuf.at[pl.ds(0, chunk_rows)], io_sem.at[1])

    def list_copies():
        return (pltpu.make_async_copy(dest_hbm.at[c], dest_s, l_sem.at[0]),
                pltpu.make_async_copy(gate_hbm.at[c], gate_s.at[:, pl.ds(0, n_assign)], l_sem.at[1]),
                pltpu.make_async_copy(fill_hbm, list_s, l_sem.at[2]))

    for cp in list_copies() + (x1_copy(),):
        cp.start()

    @pl.when(c == 0)
    def _first_chunk():
        h_copy(0).start()
        for cp in weight_copies(firstexp_ref[0], 0):
            cp.start()

    gate_s[0, n_assign] = jnp.float32(0.0)
    for cp in list_copies():
        cp.wait()

    def invert_body(i, carry):
        for jj in range(INVERT_UNROLL):
            a = i * INVERT_UNROLL + jj
            list_s[0, dest_s[0, a]] = a
        return carry

    lax.fori_loop(0, n_assign // INVERT_UNROLL, invert_body, 0)

    acc[pl.ds(chunk_rows, SUBLANES), :] = jnp.zeros((SUBLANES, LANES), jnp.float32)
    hbuf[pl.ds(chunk_rows, SUBLANES), :] = jnp.zeros((SUBLANES, LANES), jnp.float32)
    rbuf1[...] = jnp.zeros(rbuf1.shape, rbuf1.dtype)
    x1_copy().wait()
    h_copy(c).wait()

    def token_rows(a):
        return pl.ds(pl.multiple_of((a >> TOP_K_SHIFT) * SUBLANES, SUBLANES), SUBLANES)

    def weights_step(g):
        @pl.when(bfirst_ref[c, g] == 1)
        def _():
            slot = bslot_ref[c, g]
            for cp in weight_copies(bexp_ref[c, g], slot):
                cp.wait()
            nxt = bnext_ref[c, g]

            @pl.when(nxt >= 0)
            def _():
                for cp in weight_copies(nxt, jnp.where(slot == WEIGHT_SLOTS - 1, 0, slot + 1)):
                    cp.start()

    def gather_rows(g, gbuf, j_lo, j_hi):
        s0 = g * MOE_ROWS
        for j in range(j_lo, j_hi):
            gbuf[pl.ds(j * SUBLANES, SUBLANES), :] = hbuf[token_rows(list_s[0, s0 + j]), :]

    def scatter_rows(g, rbuf, j_lo, j_hi):
        s0 = g * MOE_ROWS
        for j0 in range(j_lo, j_hi, SCATTER_UNROLL):
            dsts, vals = [], []
            for j in range(j0, j0 + SCATTER_UNROLL):
                a = list_s[0, s0 + j]
                dst = token_rows(a)
                dsts.append(dst)
                vals.append(acc[dst, :] + gate_s[0, a] * rbuf[pl.ds(j * SUBLANES, SUBLANES), :])
            for dst, val in zip(dsts, vals):
                acc[dst, :] = val

    n_hid = 2 * d_ff // MXU_COLS
    n_out = d_model // MXU_COLS
    half = MOE_ROWS // 2
    rows_hid = half // n_hid
    rows_out = half // n_out

    def stage(g_gather, gbuf_in, g_compute, gbuf, rbuf, g_scatter, rbuf_out):
        e = bexp_ref[c, g_compute]
        slot = bslot_ref[c, g_compute]
        xs = [gbuf[pl.ds(kk, MOE_ROWS, stride=SUBLANES), :] for kk in range(nk)]
        xb = jnp.concatenate(xs, axis=-1).astype(jnp.bfloat16)
        acts = []
        for cg in range(n_hid):
            cols = pl.ds(cg * MXU_COLS, MXU_COLS)
            hid = jnp.dot(xb, w1buf[slot, :, cols], preferred_element_type=jnp.float32) + b1_ref[e, :, cols]
            glu = jnp.minimum(hid[:, :LANES], SWIGLU_LIMIT)
            lin = jnp.clip(hid[:, LANES:], -SWIGLU_LIMIT, SWIGLU_LIMIT)
            acts.append(glu * jax.nn.sigmoid(SWIGLU_ALPHA * glu) * (lin + 1.0))
            gather_rows(g_gather, gbuf_in, cg * rows_hid, (cg + 1) * rows_hid)
            scatter_rows(g_scatter, rbuf_out, cg * rows_hid, (cg + 1) * rows_hid)
        act = jnp.concatenate(acts, axis=-1).astype(jnp.bfloat16)
        for og in range(n_out):
            cols = pl.ds(og * MXU_COLS, MXU_COLS)
            y = jnp.dot(act, w2buf[slot, :, cols], preferred_element_type=jnp.float32) + b2_ref[e, :, cols]
            for kk in range(MXU_COLS // LANES):
                rbuf[pl.ds(og * (MXU_COLS // LANES) + kk, MOE_ROWS, stride=SUBLANES), :] = (
                    y[:, kk * LANES:(kk + 1) * LANES])
            gather_rows(g_gather, gbuf_in, half + og * rows_out, half + (og + 1) * rows_out)
            scatter_rows(g_scatter, rbuf_out, half + og * rows_out, half + (og + 1) * rows_out)

    gather_rows(0, gbuf0, 0, MOE_ROWS)

    def pair_body(p, carry):
        g0 = 2 * p
        @pl.when(bfirst_ref[c, g0] + bfirst_ref[c, g0 + 1] > 0)
        def _():
            weights_step(g0)
            weights_step(g0 + 1)

        stage(g0 + 1, gbuf1, g0, gbuf0, rbuf0, jnp.where(p == 0, pad_block, g0 - 1), rbuf1)
        stage(g0 + 2, gbuf0, g0 + 1, gbuf1, rbuf1, g0, rbuf0)
        return carry

    npairs = (nbc_ref[c] + 1) // 2
    lax.fori_loop(0, npairs, pair_body, 0)

    @pl.when(c + 1 < pl.num_programs(0))
    def _next_chunk_rows():
        h_copy(c + 1).start()

    scatter_rows(2 * npairs - 1, rbuf1, 0, MOE_ROWS)

    def out_copy(grp, buf):
        return pltpu.make_async_copy(obuf.at[buf], out_hbm.at[pl.ds(c * chunk + grp * OUT_ROWS, OUT_ROWS)],
                                     o_sem.at[buf])

    def out_body(gp, carry):
        for buf in range(2):
            grp = 2 * gp + buf

            @pl.when(gp > 0)
            def _():
                out_copy(grp - 2, buf).wait()

            first = grp * (OUT_ROWS * SUBLANES)
            obuf[buf] = jnp.concatenate(
                [acc[pl.ds(first + kk, OUT_ROWS, stride=SUBLANES), :] for kk in range(nk)], axis=-1)
            out_copy(grp, buf).start()
        return carry

    n_groups = chunk // OUT_ROWS
    lax.fori_loop(0, n_groups // 2, out_body, 0)
    for buf in range(2):
        out_copy(n_groups - 2 + buf, buf).wait()


def _block_tables(nblk, table_blocks):
    nchunks = nblk.shape[0]
    bend = jnp.cumsum(nblk, axis=1)
    nbc = bend[:, -1]
    g = jnp.arange(table_blocks, dtype=jnp.int32)
    nonempty = nblk > 0
    last_exp = jnp.max(jnp.where(nonempty, jnp.arange(N_EXPERTS, dtype=jnp.int32)[None, :], 0), axis=1)
    bexp = jnp.sum((g[None, :, None] >= bend[:, None, :]).astype(jnp.int32), axis=-1)
    bexp = jnp.where(g[None, :] >= nbc[:, None], last_exp[:, None], bexp)
    bstart = bend - nblk
    is_exp = bexp[:, :, None] == jnp.arange(N_EXPERTS, dtype=jnp.int32)[None, None, :]
    per_block = lambda table: jnp.sum(jnp.where(is_exp, table[:, None, :], 0), axis=-1)
    bfirst = ((g[None, :] == per_block(bstart)) & (g[None, :] < nbc[:, None])).astype(jnp.int32)
    flat = nonempty.reshape(-1)
    pair_slot = ((jnp.cumsum(flat.astype(jnp.int32)) - 1) % WEIGHT_SLOTS).reshape(nchunks, N_EXPERTS)
    n_pairs = flat.shape[0]
    idx = jnp.where(flat, jnp.arange(n_pairs, dtype=jnp.int32), n_pairs)
    later = jnp.concatenate([lax.cummin(idx[::-1])[::-1][1:], jnp.full((1,), n_pairs, jnp.int32)])
    pair_next = jnp.where(later < n_pairs, later % N_EXPERTS, -1).reshape(nchunks, N_EXPERTS)
    bslot = per_block(pair_slot)
    bnext = per_block(pair_next)
    first_exp = (jnp.min(idx) % N_EXPERTS).reshape(1)
    return nbc.astype(jnp.int32), bexp, bfirst, bslot.astype(jnp.int32), bnext.astype(jnp.int32), first_exp


def _moe(x1r, hr, dest, gates, tables, w1, w2, b1, b2, chunk, table_blocks):
    t_total = x1r.shape[0] // SUBLANES
    nchunks = t_total // chunk
    d_model = w1.shape[1]
    d_ff = w2.shape[1]
    n_assign = TOP_K * chunk
    n_slots = table_blocks * MOE_ROWS
    fill = jnp.full((1, n_slots), n_assign, jnp.int32)
    anyspace = pl.BlockSpec(memory_space=pl.ANY)
    grid_spec = pltpu.PrefetchScalarGridSpec(
        num_scalar_prefetch=6,
        grid=(nchunks,),
        in_specs=[anyspace] * 7 + [pl.BlockSpec((N_EXPERTS, 1, 2 * d_ff), lambda c, *_: (0, 0, 0)),
                                   pl.BlockSpec((N_EXPERTS, 1, d_model), lambda c, *_: (0, 0, 0))],
        out_specs=anyspace,
        scratch_shapes=[pltpu.VMEM(((chunk + 1) * SUBLANES, LANES), jnp.float32),
                        pltpu.VMEM(((chunk + 1) * SUBLANES, LANES), jnp.float32),
                        pltpu.VMEM((WEIGHT_SLOTS, d_model, 2 * d_ff), jnp.bfloat16),
                        pltpu.VMEM((WEIGHT_SLOTS, d_ff, d_model), jnp.bfloat16),
                        pltpu.VMEM((MOE_ROWS * SUBLANES, LANES), jnp.float32),
                        pltpu.VMEM((MOE_ROWS * SUBLANES, LANES), jnp.float32),
                        pltpu.VMEM((MOE_ROWS * SUBLANES, LANES), jnp.float32),
                        pltpu.VMEM((MOE_ROWS * SUBLANES, LANES), jnp.float32),
                        pltpu.VMEM((2, OUT_ROWS, d_model), jnp.float32),
                        pltpu.SMEM((1, n_assign), jnp.int32),
                        pltpu.SMEM((1, n_assign + LANES), jnp.float32),
                        pltpu.SMEM((1, n_slots), jnp.int32),
                        pltpu.SemaphoreType.DMA((2,)),
                        pltpu.SemaphoreType.DMA((2, WEIGHT_SLOTS)),
                        pltpu.SemaphoreType.DMA((3,)),
                        pltpu.SemaphoreType.DMA((2,))],
    )
    return pl.pallas_call(
        _moe_kernel,
        grid_spec=grid_spec,
        out_shape=jax.ShapeDtypeStruct((t_total, d_model), jnp.float32),
        compiler_params=pltpu.CompilerParams(dimension_semantics=("arbitrary",),
                                             vmem_limit_bytes=VMEM_LIMIT_BYTES),
        name="moe",
    )(*tables, x1r, hr, dest, gates, fill, w1, w2, b1, b2)


def _t5_bucket_table():
    i = np.arange(ATT_BLOCK)[:, None]
    j = np.arange(2 * ATT_BLOCK)[None, :]
    rel = i + ATT_BLOCK - j
    max_exact = N_BUCKETS // 2
    nf = np.maximum(rel, 1).astype(np.float32)
    large = max_exact + (np.log(nf / max_exact) / np.float32(math.log(MAX_DISTANCE / max_exact))
                         * (N_BUCKETS - max_exact)).astype(np.int32)
    large = np.minimum(large, N_BUCKETS - 1)
    bucket = np.where(rel < max_exact, rel, large)
    return np.where((rel >= 0) & (rel < WINDOW), bucket, -1).astype(np.int32)


def _token_major(a, nchunks):
    return jnp.transpose(a, (0, 2, 1)).reshape(nchunks, 1, -1)


def kernel(x, norm_mix, w_in, b_in, conv_w, q_norm, k_norm, sinks, rel_bias, out_norm_conv, out_norm_attn,
           w_out, b_out, norm_ffn, w_router, b_router, w1, b1, w2, b2):
    bsz, seq, d_model = x.shape
    depth = w_in.shape[0]
    bf16 = jnp.bfloat16
    bucket = jnp.asarray(_t5_bucket_table())
    t_total = bsz * seq
    chunk = min(MOE_CHUNK, t_total)
    nchunks = t_total // chunk
    table_blocks = chunk * TOP_K // MOE_ROWS + N_EXPERTS + 2
    for l in range(depth):
        d_ff = w2.shape[2]
        params = dict(
            sinks=sinks[l], rel_bias=rel_bias, gmix=norm_mix[l][None], win=w_in[l].astype(bf16), bin=b_in[l][None],
            convw=conv_w[l], gq2=jnp.tile(q_norm[l], 2)[None] * (HEAD_DIM ** -0.5), gk2=jnp.tile(k_norm[l], 2)[None],
            gc=out_norm_conv[l][None], ga=out_norm_attn[l][None], wout=w_out[l].astype(bf16), bout=b_out[l][None],
            gffn=norm_ffn[l][None], wrt=w_router[l].T, br=b_router[l][:, None])
        x1r, hr, ids_t, gates_t, rank_t, counts_t, w1p, w2p = _mixer(x, params, bucket, chunk, w1[l], w2[l])
        counts = counts_t[:, :, 0]
        padded = (counts + MOE_ROWS - 1) // MOE_ROWS * MOE_ROWS
        pad_starts = jnp.cumsum(padded, axis=1) - padded
        tables = _block_tables(padded // MOE_ROWS, table_blocks)
        tiles_per_chunk = ids_t.shape[0] // nchunks
        tile_starts = jnp.repeat(pad_starts, tiles_per_chunk, axis=0)
        dest_t = rank_t
        for e in range(N_EXPERTS):
            dest_t = dest_t + jnp.where(ids_t == e, tile_starts[:, e:e + 1, None], 0)
        b1p = jnp.transpose(b1[l].reshape(N_EXPERTS, 2 * d_ff // MXU_COLS, LANES, 2), (0, 1, 3, 2))
        b1p = b1p.reshape(N_EXPERTS, 1, 2 * d_ff)
        out = _moe(x1r, hr, _token_major(dest_t, nchunks), _token_major(gates_t, nchunks), tables, w1p, w2p, b1p,
                   b2[l][:, None, :], chunk, table_blocks)
        x = out.reshape(bsz, seq, d_model)
    return x
```
